```python
import math
import jax, jax.numpy as jnp
from jax import lax
import numpy as np

D_MODEL = 1024
BATCH = 4
SEQ = 8192
DEPTH = 1

CTX_LEN = 256
GRID_W = 64
N_MOD = 6
EPS = 1e-6
RET_HEADS = 4
RET_DK = 128
RET_DV = 256
RET_QK = RET_HEADS * RET_DK
RET_V = RET_HEADS * RET_DV
RET_CHUNK = 128
ROPE_BASE = 10000.0
D_RNN = 1024
LRU_BLOCKS = 8
LRU_BS = D_RNN // LRU_BLOCKS
CONV_W = 4
CONV_LEFT = 2
LRU_C = 8.0
PEER_HEADS = 8
PEER_DKEY = 128
PEER_DHALF = PEER_DKEY // 2
PEER_NKEYS = 128
PEER_TOPK = 16
PEER_NEXP = PEER_NKEYS * PEER_NKEYS
PEER_BLOCK = 128
IN_SIZES = (RET_QK, RET_QK, RET_V, RET_V, D_RNN, D_RNN, D_MODEL, D_MODEL)
IN_COLS = sum(IN_SIZES)

kernel_name = 'hybrid_retention_rglru_peer_block'


def rms_norm(x, g):
    x32 = x.astype(jnp.float32)
    y = x32 * lax.rsqrt(jnp.mean(x32 * x32, axis=-1, keepdims=True) + EPS)
    return (y * g.astype(jnp.float32)).astype(x.dtype)


def modulate(h, shift, scale):
    return h * (1.0 + scale) + shift


def split_proj(p):
    idx = [int(s) for s in np.cumsum(IN_SIZES)[:-1]]
    return jnp.split(p, idx, axis=-1)


def head_norm(o):
    mu = jnp.mean(o, axis=-1, keepdims=True)
    var = jnp.mean(jnp.square(o - mu), axis=-1, keepdims=True)
    return (o - mu) * lax.rsqrt(var + EPS)


def axial_rope(t, rows):
    quarter = RET_DK // 4
    row = jnp.repeat(jnp.arange(rows), GRID_W)
    col = jnp.tile(jnp.arange(GRID_W), rows)
    inv = ROPE_BASE ** (-jnp.arange(quarter, dtype=jnp.float32) / quarter)

    def rot(u, pos):
        ang = pos.astype(jnp.float32)[:, None] * inv[None, :]
        cos = jnp.cos(ang)[None, :, None, :]
        sin = jnp.sin(ang)[None, :, None, :]
        u1, u2 = u[..., :quarter], u[..., quarter:]
        return jnp.concatenate([u1 * cos - u2 * sin, u2 * cos + u1 * sin], axis=-1)

    half = RET_DK // 2
    return jnp.concatenate([rot(t[..., :half], row), rot(t[..., half:], col)], axis=-1)


def retention_dir(q, k, v, log_gamma, s0, inclusive):
    b_, L, H, _ = q.shape
    dv = v.shape[-1]
    n = L // RET_CHUNK

    def chunks(t):
        return t.reshape(b_, n, RET_CHUNK, H, t.shape[-1]).transpose(1, 0, 3, 2, 4)

    idx = jnp.arange(RET_CHUNK, dtype=jnp.float32)
    diff = idx[:, None] - idx[None, :]
    keep = (diff >= 0) if inclusive else (diff > 0)
    dmat = jnp.where(keep[None], jnp.exp(log_gamma[:, None, None] * jnp.maximum(diff, 0.0)[None]), 0.0)
    xi = jnp.exp(log_gamma[:, None] * (idx + 1.0)[None])[..., None]
    zeta = jnp.exp(log_gamma[:, None] * (RET_CHUNK - 1.0 - idx)[None])[..., None]
    chunk_decay = jnp.exp(log_gamma * RET_CHUNK)[:, None, None]

    def step(S, blk):
        qb, kb, vb = blk
        scores = jnp.einsum('bhid,bhjd->bhij', qb, kb) * dmat
        o = jnp.einsum('bhij,bhjv->bhiv', scores, vb) + jnp.einsum('bhid,bhdv->bhiv', qb * xi, S)
        S = chunk_decay * S + jnp.einsum('bhjd,bhjv->bhdv', kb * zeta, vb)
        return S, o

    S, o = lax.scan(step, s0, (chunks(q), chunks(k), chunks(v)))
    o = o.transpose(1, 0, 3, 2, 4).reshape(b_, L, H, dv)
    return o, S


def retention_mixer(pc, px, ret_decay_l, rows):
    f32 = jnp.float32
    b_ = px[0].shape[0]

    def heads(t, d):
        return t.astype(f32).reshape(t.shape[0], t.shape[1], RET_HEADS, d)

    kscale = RET_DK ** -0.5
    qc, kc, vc = heads(pc[0], RET_DK), heads(pc[1], RET_DK) * kscale, heads(pc[2], RET_DV)
    qx = axial_rope(heads(px[0], RET_DK), rows)
    kx = axial_rope(heads(px[1], RET_DK), rows) * kscale
    vx = heads(px[2], RET_DV)
    log_gamma = jax.nn.log_sigmoid(ret_decay_l.astype(f32))
    zeros = jnp.zeros((b_, RET_HEADS, RET_DK, RET_DV), f32)

    def bidir(q, k, v, s0f, s0b):
        of, sf = retention_dir(q, k, v, log_gamma[0], s0f, True)
        ob, sb = retention_dir(q[:, ::-1], k[:, ::-1], v[:, ::-1], log_gamma[1], s0b, False)
        return of + ob[:, ::-1], sf, sb

    oc, sf, sb = bidir(qc, kc, vc, zeros, zeros)
    ox, _, _ = bidir(qx, kx, vx, sf, sb)
    return head_norm(oc), head_norm(ox)


def short_conv(u, w, b):
    L = u.shape[1]
    up = jnp.pad(u, ((0, 0), (CONV_LEFT, CONV_W - 1 - CONV_LEFT), (0, 0)))
    out = b.astype(jnp.float32)
    for tap in range(CONV_W):
        out = out + up[:, tap:tap + L] * w[tap].astype(jnp.float32)
    return out


def lru_coeffs(u, wa, ba, wx, bx, lam):
    f32 = jnp.float32
    b_, L, _ = u.shape
    ub = u.reshape(b_, L, LRU_BLOCKS, LRU_BS)
    r = jax.nn.sigmoid(jnp.einsum('blnc,ncd->blnd', ub, wa.astype(f32)).reshape(b_, L, D_RNN) + ba.astype(f32))
    i = jax.nn.sigmoid(jnp.einsum('blnc,ncd->blnd', ub, wx.astype(f32)).reshape(b_, L, D_RNN) + bx.astype(f32))
    log_a = -LRU_C * r * jax.nn.softplus(-lam.astype(f32))
    a = jnp.exp(log_a)
    bterm = jnp.sqrt(-jnp.expm1(2.0 * log_a)) * (i * u)
    return a, bterm


def _lin_combine(e1, e2):
    a1, b1 = e1
    a2, b2 = e2
    return a1 * a2, a2 * b1 + b2


def lru_scan(a, b, h0, reverse):
    A, Bc = lax.associative_scan(_lin_combine, (a, b), axis=1, reverse=reverse)
    h = A * h0[:, None, :] + Bc
    final = h[:, 0] if reverse else h[:, -1]
    return h, final


def lru_mixer(uc_in, ux_in, cw, cb, wa, ba, wx, bx, lam):
    uc = short_conv(uc_in.astype(jnp.float32), cw, cb)
    ux = short_conv(ux_in.astype(jnp.float32), cw, cb)
    ys_c, ys_x = [], []
    for d in range(2):
        rev = d == 1
        ac, bc = lru_coeffs(uc, wa[d], ba[d], wx[d], bx[d], lam[d])
        hc, fin = lru_scan(ac, bc, jnp.zeros_like(uc[:, 0]), rev)
        ax, bxx = lru_coeffs(ux, wa[d], ba[d], wx[d], bx[d], lam[d])
        hx, _ = lru_scan(ax, bxx, fin, rev)
        ys_c.append(hc)
        ys_x.append(hx)
    return ys_c[0] + ys_c[1], ys_x[0] + ys_x[1]


def merge_branches(ret_o, lru_y, p, w_ret_out, w_lru_out, w_out):
    dt = p[3].dtype
    b_, L = ret_o.shape[0], ret_o.shape[1]
    ret = (ret_o.reshape(b_, L, RET_V).astype(dt) * jax.nn.silu(p[3])) @ w_ret_out
    lru = (lru_y.astype(dt) * jax.nn.gelu(p[5])) @ w_lru_out
    y = jax.nn.sigmoid(p[6]) * ret + jax.nn.sigmoid(p[7]) * lru
    return y @ w_out


def peer_ffn(h, wq, keys, u_tab, v_tab):
    b_, L, D = h.shape
    tok = h.reshape(-1, PEER_BLOCK, D)
    K = PEER_TOPK

    def block(xb):
        q = (xb @ wq).reshape(PEER_BLOCK, PEER_HEADS, 2, PEER_DHALF)
        s = jnp.einsum('thpd,hpkd->thpk', q, keys).astype(jnp.float32)
        s1, i1 = lax.top_k(s[:, :, 0], K)
        s2, i2 = lax.top_k(s[:, :, 1], K)
        cand = (s1[..., :, None] + s2[..., None, :]).reshape(PEER_BLOCK, PEER_HEADS, K * K)
        top, ci = lax.top_k(cand, K)
        e = jnp.take_along_axis(i1, ci // K, axis=-1) * PEER_NKEYS + jnp.take_along_axis(i2, ci % K, axis=-1)
        g = jax.nn.softmax(top, axis=-1).astype(xb.dtype)
        act = jax.nn.gelu(jnp.einsum('thkd,td->thk', u_tab[e], xb))
        return jnp.einsum('thk,thkd->td', g * act, v_tab[e])

    return lax.map(block, tok).reshape(b_, L, D)


def setup_inputs(seed: int = 0) -> dict:
    key = jax.random.key(seed)
    ks = jax.random.split(key, 32)
    f = jnp.float32
    D = D_MODEL

    def nrm(k, shape, s):
        return jax.random.normal(k, shape, f) * s

    gamma = 1.0 - jnp.exp2(-5.0 - jnp.arange(RET_HEADS, dtype=f))
    gamma_logit = jnp.log(gamma) - jnp.log1p(-gamma)
    ret_decay = jnp.broadcast_to(gamma_logit, (DEPTH, 2, RET_HEADS)) + nrm(ks[9], (DEPTH, 2, RET_HEADS), 0.01)
    a8 = jax.random.uniform(ks[17], (DEPTH, 2, D_RNN), f, 0.9, 0.999)
    s_lam = a8 ** (1.0 / LRU_C)
    lru_lambda = jnp.log(s_lam) - jnp.log1p(-s_lam)
    return {
        'x': nrm(ks[0], (BATCH, SEQ, D), 1.0),
        'c': nrm(ks[1], (BATCH, D), 1.0),
        'ctx': nrm(ks[2], (BATCH, CTX_LEN, D), 1.0),
        'c_ctx': nrm(ks[3], (D,), 1.0),
        'mod_w': nrm(ks[4], (DEPTH, D, N_MOD * D), 0.5 * D ** -0.5),
        'mod_b': nrm(ks[5], (DEPTH, N_MOD * D), 0.02),
        'norm1_g': 1.0 + nrm(ks[6], (DEPTH, D), 0.02),
        'norm2_g': 1.0 + nrm(ks[7], (DEPTH, D), 0.02),
        'w_in': nrm(ks[8], (DEPTH, D, IN_COLS), D ** -0.5),
        'ret_decay': ret_decay,
        'conv_w': nrm(ks[10], (DEPTH, CONV_W, D_RNN), CONV_W ** -0.5),
        'conv_b': nrm(ks[11], (DEPTH, D_RNN), 0.02),
        'lru_wa': nrm(ks[12], (DEPTH, 2, LRU_BLOCKS, LRU_BS, LRU_BS), LRU_BS ** -0.5),
        'lru_ba': nrm(ks[13], (DEPTH, 2, D_RNN), 0.1),
        'lru_wx': nrm(ks[14], (DEPTH, 2, LRU_BLOCKS, LRU_BS, LRU_BS), LRU_BS ** -0.5),
        'lru_bx': nrm(ks[15], (DEPTH, 2, D_RNN), 0.1),
        'lru_lambda': lru_lambda,
        'w_ret_out': nrm(ks[18], (DEPTH, RET_V, D), RET_V ** -0.5),
        'w_lru_out': nrm(ks[19], (DEPTH, D_RNN, D), D_RNN ** -0.5),
        'w_out': nrm(ks[20], (DEPTH, D, D), D ** -0.5),
        'peer_wq': nrm(ks[21], (DEPTH, D, PEER_HEADS * PEER_DKEY), D ** -0.5),
        'peer_keys': nrm(ks[22], (DEPTH, PEER_HEADS, 2, PEER_NKEYS, PEER_DHALF), PEER_DHALF ** -0.5),
        'peer_u': nrm(ks[23], (DEPTH, PEER_NEXP, D), D ** -0.5),
        'peer_v': nrm(ks[24], (DEPTH, PEER_NEXP, D), PEER_HEADS ** -0.5),
        'final_g': 1.0 + nrm(ks[25], (D,), 0.02),
    }


def reference(x, c, ctx, c_ctx, mod_w, mod_b, norm1_g, norm2_g, w_in, ret_decay, conv_w, conv_b,
              lru_wa, lru_ba, lru_wx, lru_bx, lru_lambda, w_ret_out, w_lru_out, w_out,
              peer_wq, peer_keys, peer_u, peer_v, final_g):
    n_tok = x.shape[1]
    rows = n_tok // GRID_W
    sc = jax.nn.silu(c)
    scc = jax.nn.silu(c_ctx)
    for l in range(DEPTH):
        last = l == DEPTH - 1
        mod_x = (sc @ mod_w[l] + mod_b[l])[:, None, :]
        mod_c = (scc @ mod_w[l] + mod_b[l])[None, None, :]
        sh1x, sc1x, g1x, sh2x, sc2x, g2x = jnp.split(mod_x, N_MOD, axis=-1)
        sh1c, sc1c, g1c, sh2c, sc2c, g2c = jnp.split(mod_c, N_MOD, axis=-1)

        hx = modulate(rms_norm(x, norm1_g[l]), sh1x, sc1x)
        hc = modulate(rms_norm(ctx, norm1_g[l]), sh1c, sc1c)
        px = split_proj(hx @ w_in[l])
        pc = split_proj(hc @ w_in[l])
        ret_c, ret_x = retention_mixer(pc, px, ret_decay[l], rows)
        lru_c, lru_x = lru_mixer(pc[4], px[4], conv_w[l], conv_b[l], lru_wa[l], lru_ba[l],
                                 lru_wx[l], lru_bx[l], lru_lambda[l])
        x = x + g1x * merge_branches(ret_x, lru_x, px, w_ret_out[l], w_lru_out[l], w_out[l])
        if not last:
            ctx = ctx + g1c * merge_branches(ret_c, lru_c, pc, w_ret_out[l], w_lru_out[l], w_out[l])
            hc2 = modulate(rms_norm(ctx, norm2_g[l]), sh2c, sc2c)
            ctx = ctx + g2c * peer_ffn(hc2, peer_wq[l], peer_keys[l], peer_u[l], peer_v[l])

        hx2 = modulate(rms_norm(x, norm2_g[l]), sh2x, sc2x)
        x = x + g2x * peer_ffn(hx2, peer_wq[l], peer_keys[l], peer_u[l], peer_v[l])
    return rms_norm(x, final_g)
```

```python
import functools
import math

import jax
import jax.numpy as jnp
import numpy as np
from jax import lax
from jax.experimental import pallas as pl
from jax.experimental.pallas import tpu as pltpu

F32 = jnp.float32
BF16 = jnp.bfloat16

EPS = 1e-6
GRID_W = 64
ROPE_BASE = 10000.0
RET_HEADS = 4
RET_DK = 128
RET_DV = 256
RET_CHUNK = 128
LRU_BLOCKS = 8
LRU_BS = 128
LRU_C = 8.0
CONV_W = 4
PEER_HEADS = 8
PEER_DHALF = 64
PEER_NKEYS = 128
PEER_TOPK = 16
COL_Q, COL_K, COL_V, COL_RGATE, COL_LRUX, COL_LRUG, COL_BGR, COL_BGL = 0, 512, 1024, 2048, 3072, 4096, 5120, 6144
IN_COLS = 7168

V7X_VMEM_BYTES = 64 * 1024 * 1024
LANES = 128
SUBLANES = 8

NEG_INF = float("-inf")


def _vmem_limit(nbytes):
    return int(min(max(nbytes, 16 * 1024 * 1024), V7X_VMEM_BYTES - 6 * 1024 * 1024))


def _dot(a, b):
    return jnp.dot(a, b, preferred_element_type=F32)


def _split_bf16(a):
    hi = a.astype(BF16)
    lo = (a - hi.astype(F32)).astype(BF16)
    return hi, lo


def _dot3(a_hi, a_lo, b_hi, b_lo):
    return _dot(a_hi, b_hi) + (_dot(a_hi, b_lo) + _dot(a_lo, b_hi))


def _mod_body(c_ref, w_ref, b_ref, o_ref):
    c = c_ref[...]
    sc = c * jax.nn.sigmoid(c)
    c_hi, c_lo = _split_bf16(sc)
    w_hi, w_lo = _split_bf16(w_ref[...])
    o_ref[...] = _dot3(c_hi, c_lo, w_hi, w_lo) + b_ref[...]


def _modulation(c_all, mod_w, mod_b):
    rows, d = c_all.shape
    n = mod_w.shape[1]
    tn = 1536
    return pl.pallas_call(
        _mod_body,
        grid=(n // tn,),
        in_specs=[pl.BlockSpec((rows, d), lambda j: (0, 0)),
                  pl.BlockSpec((d, tn), lambda j: (0, j)),
                  pl.BlockSpec((1, tn), lambda j: (0, j))],
        out_specs=pl.BlockSpec((rows, tn), lambda j: (0, j)),
        out_shape=jax.ShapeDtypeStruct((rows, n), F32),
        compiler_params=pltpu.CompilerParams(
            dimension_semantics=("arbitrary",), vmem_limit_bytes=_vmem_limit(40 << 20)),
        name="modulation",
    )(c_all, mod_w, mod_b)


def _rms_mod(x, g, shift, scale):
    ms = jnp.mean(x * x, axis=-1, keepdims=True)
    h = x * lax.rsqrt(ms + EPS) * g
    return h * (1.0 + scale) + shift


def _inproj_body(x_ref, g_ref, sh_ref, sc_ref, w_ref, o_ref, h_scr):
    @pl.when(pl.program_id(1) == 0)
    def _():
        h_scr[...] = _rms_mod(x_ref[...], g_ref[...], sh_ref[0], sc_ref[0]).astype(BF16)

    o_ref[...] = _dot(h_scr[...], w_ref[...])


def _in_projection(x2, g, shift, scale, w_bf, seq_len):
    t, d = x2.shape
    n = w_bf.shape[1]
    tm = min(1024, seq_len)
    tn = 1024
    per_batch = seq_len // tm
    if shift.shape[0] == 1:
        mod_map = lambda i, j: (0, 0, 0)
    else:
        mod_map = lambda i, j: (i // per_batch, 0, 0)
    return pl.pallas_call(
        _inproj_body,
        grid=(t // tm, n // tn),
        in_specs=[pl.BlockSpec((tm, d), lambda i, j: (i, 0)),
                  pl.BlockSpec((1, d), lambda i, j: (0, 0)),
                  pl.BlockSpec((1, 1, d), mod_map),
                  pl.BlockSpec((1, 1, d), mod_map),
                  pl.BlockSpec((d, tn), lambda i, j: (0, j))],
        out_specs=pl.BlockSpec((tm, tn), lambda i, j: (i, j)),
        out_shape=jax.ShapeDtypeStruct((t, n), F32),
        scratch_shapes=[pltpu.VMEM((tm, d), BF16)],
        compiler_params=pltpu.CompilerParams(
            dimension_semantics=("parallel", "arbitrary"), vmem_limit_bytes=_vmem_limit(40 << 20)),
        name="in_projection",
    )(x2, g, shift, scale, w_bf)


def _rope_tables(n_tok):
    quarter = RET_DK // 4
    pos = jnp.arange(n_tok)
    row = (pos // GRID_W).astype(F32)
    col = (pos % GRID_W).astype(F32)
    inv = ROPE_BASE ** (-jnp.arange(quarter, dtype=F32) / quarter)
    ar = row[:, None] * inv[None, :]
    ac = col[:, None] * inv[None, :]
    cos_t = jnp.concatenate([jnp.cos(ar), jnp.cos(ar), jnp.cos(ac), jnp.cos(ac)], axis=-1)
    sin_t = jnp.concatenate([-jnp.sin(ar), jnp.sin(ar), -jnp.sin(ac), jnp.sin(ac)], axis=-1)
    return cos_t, sin_t


def _rope(t, cos_t, sin_t, lane_low):
    swapped = jnp.where(lane_low, pltpu.roll(t, 96, 1), pltpu.roll(t, 32, 1))
    return t * cos_t + swapped * sin_t


def _ret_body(lg_ref, qf_ref, kf_ref, vf_ref, qb_ref, kb_ref, vb_ref,
              cosf_ref, sinf_ref, cosb_ref, sinb_ref, s0f_ref, s0b_ref,
              of_ref, ob_ref, sff_ref, sfb_ref, sf_scr, sb_scr, *, use_rope, n_batch):
    c = pl.program_id(0)
    n = pl.num_programs(0)
    C = RET_CHUNK

    @pl.when(c == 0)
    def _():
        sf_scr[...] = s0f_ref[...]
        sb_scr[...] = s0b_ref[...]

    ii = lax.broadcasted_iota(jnp.int32, (C, C), 0)
    jj = lax.broadcasted_iota(jnp.int32, (C, C), 1)
    col_i = ii.astype(F32)
    lane = lax.broadcasted_iota(jnp.int32, (C, RET_DK), 1)
    lane_low = (lane % 64) < 32
    kscale = RET_DK ** -0.5

    tabs = []
    for d in range(2):
        per_head = []
        for h in range(RET_HEADS):
            lg = lg_ref[d, h]
            if d == 0:
                diff = (ii - jj).astype(F32)
                keep = ii >= jj
                xi = jnp.exp(lg * (col_i + 1.0))
                zeta = jnp.exp(lg * (C - 1.0 - col_i))
            else:
                diff = (jj - ii).astype(F32)
                keep = jj > ii
                xi = jnp.exp(lg * (C - col_i))
                zeta = jnp.exp(lg * col_i)
            dmat = jnp.where(keep, jnp.exp(lg * jnp.maximum(diff, 0.0)), 0.0)
            cdec = jnp.exp(jnp.full((1, RET_DV), lg * C, F32))
            per_head.append((dmat, xi, zeta, cdec))
        tabs.append(per_head)

    refs = ((qf_ref, kf_ref, vf_ref, cosf_ref, sinf_ref, sf_scr, of_ref),
            (qb_ref, kb_ref, vb_ref, cosb_ref, sinb_ref, sb_scr, ob_ref))

    def batch_body(b, carry):
        for d in range(2):
            q_ref, k_ref, v_ref, cos_ref, sin_ref, s_scr, o_ref = refs[d]
            for h in range(RET_HEADS):
                dmat, xi, zeta, cdec = tabs[d][h]
                q = q_ref[b, :, h * RET_DK:(h + 1) * RET_DK]
                k = k_ref[b, :, h * RET_DK:(h + 1) * RET_DK]
                v = v_ref[b, :, h * RET_DV:(h + 1) * RET_DV].astype(BF16)
                if use_rope:
                    q = _rope(q, cos_ref[...], sin_ref[...], lane_low)
                    k = _rope(k, cos_ref[...], sin_ref[...], lane_low)
                k = k * kscale
                s_old = s_scr[b, h]
                scores = lax.dot_general(q.astype(BF16), k.astype(BF16), (((1,), (1,)), ((), ())),
                                         preferred_element_type=F32) * dmat
                o = _dot(scores.astype(BF16), v) + _dot((q * xi).astype(BF16), s_old.astype(BF16))
                kz_t = (k * zeta).T.astype(BF16)
                s_scr[b, h] = cdec * s_old + _dot(kz_t, v)
                o_ref[b, :, h * RET_DV:(h + 1) * RET_DV] = o
        return carry

    lax.fori_loop(0, n_batch, batch_body, 0)

    @pl.when(c == n - 1)
    def _():
        sff_ref[...] = sf_scr[...]
        sfb_ref[...] = sb_scr[...]


def _retention(p3, lg, s0f, s0b, cos_t, sin_t, use_rope):
    bsz, seq, _ = p3.shape
    n = seq // RET_CHUNK
    C = RET_CHUNK
    qk_w = RET_HEADS * RET_DK
    v_w = RET_HEADS * RET_DV
    fwd = lambda c: c
    bwd = lambda c: n - 1 - c

    def spec3(width, col_blk, order):
        return pl.BlockSpec((bsz, C, width), lambda c: (0, order(c), col_blk))

    tab_spec = lambda order: pl.BlockSpec((C, RET_DK), lambda c: (order(c), 0))
    st_spec = pl.BlockSpec((bsz, RET_HEADS, RET_DK, RET_DV), lambda c: (0, 0, 0, 0))
    st_shape = jax.ShapeDtypeStruct((bsz, RET_HEADS, RET_DK, RET_DV), F32)
    o_shape = jax.ShapeDtypeStruct((bsz, seq, v_w), F32)
    body = functools.partial(_ret_body, use_rope=use_rope, n_batch=bsz)
    return pl.pallas_call(
        body,
        grid=(n,),
        in_specs=[pl.BlockSpec(memory_space=pltpu.SMEM),
                  spec3(qk_w, COL_Q // qk_w, fwd), spec3(qk_w, COL_K // qk_w, fwd), spec3(v_w, COL_V // v_w, fwd),
                  spec3(qk_w, COL_Q // qk_w, bwd), spec3(qk_w, COL_K // qk_w, bwd), spec3(v_w, COL_V // v_w, bwd),
                  tab_spec(fwd), tab_spec(fwd), tab_spec(bwd), tab_spec(bwd),
                  st_spec, st_spec],
        out_specs=[pl.BlockSpec((bsz, C, v_w), lambda c: (0, c, 0)),
                   pl.BlockSpec((bsz, C, v_w), lambda c: (0, n - 1 - c, 0)),
                   st_spec, st_spec],
        out_shape=[o_shape, o_shape, st_shape, st_shape],
        scratch_shapes=[pltpu.VMEM((bsz, RET_HEADS, RET_DK, RET_DV), F32),
                        pltpu.VMEM((bsz, RET_HEADS, RET_DK, RET_DV), F32)],
        compiler_params=pltpu.CompilerParams(
            dimension_semantics=("arbitrary",), vmem_limit_bytes=_vmem_limit(52 << 20)),
        name="retention",
    )(lg, p3, p3, p3, p3, p3, p3, cos_t, sin_t, cos_t, sin_t, s0f, s0b)


def _lru_body(uc_ref, up_ref, un_ref, cw_ref, cb_ref, wa_ref, ba_ref, wx_ref, bx_ref, lam_ref, h0_ref,
              h_ref, fin_ref, ext_scr, a_scr, b_scr, carry_scr, *, reverse, t_blk):
    tb = pl.program_id(1)
    nt = pl.num_programs(1)
    pos = (nt - 1 - tb) if reverse else tb
    T = t_blk

    @pl.when(tb == 0)
    def _():
        carry_scr[...] = h0_ref[0]

    has_prev = (pos > 0).astype(F32)
    has_next = (pos < nt - 1).astype(F32)
    ext_scr[0:8, :] = up_ref[0] * has_prev
    ext_scr[8:8 + T, :] = uc_ref[0]
    ext_scr[8 + T:16 + T, :] = un_ref[0] * has_next
    u = (cb_ref[...]
         + ext_scr[6:6 + T, :] * cw_ref[0:1, :]
         + ext_scr[7:7 + T, :] * cw_ref[1:2, :]
         + ext_scr[8:8 + T, :] * cw_ref[2:3, :]
         + ext_scr[9:9 + T, :] * cw_ref[3:4, :])
    ub = u.astype(BF16)
    rr = []
    xx = []
    for nb in range(LRU_BLOCKS):
        blk = ub[:, nb * LRU_BS:(nb + 1) * LRU_BS]
        rr.append(_dot(blk, wa_ref[nb]))
        xx.append(_dot(blk, wx_ref[nb]))
    r = jax.nn.sigmoid(jnp.concatenate(rr, axis=-1) + ba_ref[...])
    gate_i = jax.nn.sigmoid(jnp.concatenate(xx, axis=-1) + bx_ref[...])
    z = -lam_ref[...]
    softplus = jnp.maximum(z, 0.0) + jnp.log1p(jnp.exp(-jnp.abs(z)))
    log_a = (-LRU_C) * r * softplus
    a_scr[...] = jnp.exp(log_a)
    th = jnp.tanh(log_a)
    b_scr[...] = jnp.sqrt(-2.0 * th / (1.0 - th)) * (gate_i * u)

    row = lax.broadcasted_iota(jnp.int32, (SUBLANES, a_scr.shape[1]), 0)
    n_grp = T // SUBLANES

    def step(i, hprev):
        g = (n_grp - 1 - i) if reverse else i
        r0 = pl.multiple_of(g * SUBLANES, SUBLANES)
        a = a_scr[pl.ds(r0, SUBLANES), :]
        b = b_scr[pl.ds(r0, SUBLANES), :]
        for s in (1, 2, 4):
            if reverse:
                valid = row < SUBLANES - s
                shift = SUBLANES - s
            else:
                valid = row >= s
                shift = s
            a_s = jnp.where(valid, pltpu.roll(a, shift, 0), 1.0)
            b_s = jnp.where(valid, pltpu.roll(b, shift, 0), 0.0)
            b = a * b_s + b
            a = a * a_s
        h = a * hprev + b
        h_ref[0, pl.ds(r0, SUBLANES), :] = h
        return h[0:1, :] if reverse else h[SUBLANES - 1:SUBLANES, :]

    hfin = lax.fori_loop(0, n_grp, step, carry_scr[...])
    carry_scr[...] = hfin
    fin_ref[0] = hfin


def _lru_direction(p3, cw, cb, wa_bf, ba, wx_bf, bx, lam, h0, reverse):
    bsz, seq, _ = p3.shape
    d = cw.shape[1]
    T = min(512, seq)
    nt = seq // T
    col = COL_LRUX // d
    hb = T // SUBLANES
    n_halo = seq // SUBLANES
    order = (lambda tb: nt - 1 - tb) if reverse else (lambda tb: tb)
    body = functools.partial(_lru_body, reverse=reverse, t_blk=T)
    vec = lambda: pl.BlockSpec((1, d), lambda b, tb: (0, 0))
    return pl.pallas_call(
        body,
        grid=(bsz, nt),
        in_specs=[pl.BlockSpec((1, T, d), lambda b, tb: (b, order(tb), col)),
                  pl.BlockSpec((1, SUBLANES, d), lambda b, tb: (b, jnp.maximum(order(tb) * hb - 1, 0), col)),
                  pl.BlockSpec((1, SUBLANES, d), lambda b, tb: (b, jnp.minimum((order(tb) + 1) * hb, n_halo - 1), col)),
                  pl.BlockSpec((CONV_W, d), lambda b, tb: (0, 0)),
                  vec(),
                  pl.BlockSpec((LRU_BLOCKS, LRU_BS, LRU_BS), lambda b, tb: (0, 0, 0)),
                  vec(),
                  pl.BlockSpec((LRU_BLOCKS, LRU_BS, LRU_BS), lambda b, tb: (0, 0, 0)),
                  vec(), vec(),
                  pl.BlockSpec((1, 1, d), lambda b, tb: (b, 0, 0))],
        out_specs=[pl.BlockSpec((1, T, d), lambda b, tb: (b, order(tb), 0)),
                   pl.BlockSpec((1, 1, d), lambda b, tb: (b, 0, 0))],
        out_shape=[jax.ShapeDtypeStruct((bsz, seq, d), F32), jax.ShapeDtypeStruct((bsz, 1, d), F32)],
        scratch_shapes=[pltpu.VMEM((T + 2 * SUBLANES, d), F32), pltpu.VMEM((T, d), F32),
                        pltpu.VMEM((T, d), F32), pltpu.VMEM((1, d), F32)],
        compiler_params=pltpu.CompilerParams(
            dimension_semantics=("parallel", "arbitrary"), vmem_limit_bytes=_vmem_limit(40 << 20)),
        name="rglru_rev" if reverse else "rglru_fwd",
    )(p3, p3, p3, cw, cb, wa_bf, ba, wx_bf, bx, lam, h0)


def _merge_body(x_ref, of_ref, ob_ref, hf_ref, hb_ref, rg_ref, lg_ref, bgr_ref, bgl_ref, g1_ref,
                wr_ref, wl_ref, wo_ref, o_ref):
    o = of_ref[...] + ob_ref[...]
    parts = []
    for h in range(RET_HEADS):
        oh = o[:, h * RET_DV:(h + 1) * RET_DV]
        mu = jnp.mean(oh, axis=-1, keepdims=True)
        cen = oh - mu
        var = jnp.mean(cen * cen, axis=-1, keepdims=True)
        parts.append(cen * lax.rsqrt(var + EPS))
    on = jnp.concatenate(parts, axis=-1)
    rg = rg_ref[...]
    ret = _dot((on * (rg * jax.nn.sigmoid(rg))).astype(BF16), wr_ref[...])
    lru = _dot(((hf_ref[...] + hb_ref[...]) * jax.nn.gelu(lg_ref[...])).astype(BF16), wl_ref[...])
    y = jax.nn.sigmoid(bgr_ref[...]) * ret + jax.nn.sigmoid(bgl_ref[...]) * lru
    o_ref[...] = x_ref[...] + g1_ref[0] * _dot(y.astype(BF16), wo_ref[...])


def _merge(x2, o_f, o_b, h_f, h_b, p2, g1, wr_bf, wl_bf, wo_bf, seq_len):
    t, d = x2.shape
    tm = 512
    per_batch = seq_len // tm
    tok = lambda: pl.BlockSpec((tm, d), lambda i: (i, 0))
    pcol = lambda off: pl.BlockSpec((tm, d), lambda i: (i, off // d))
    wspec = lambda: pl.BlockSpec((d, d), lambda i: (0, 0))
    return pl.pallas_call(
        _merge_body,
        grid=(t // tm,),
        in_specs=[tok(), tok(), tok(), tok(), tok(),
                  pcol(COL_RGATE), pcol(COL_LRUG), pcol(COL_BGR), pcol(COL_BGL),
                  pl.BlockSpec((1, 1, d), lambda i: (i // per_batch, 0, 0)),
                  wspec(), wspec(), wspec()],
        out_specs=tok(),
        out_shape=jax.ShapeDtypeStruct((t, d), F32),
        compiler_params=pltpu.CompilerParams(
            dimension_semantics=("parallel",), vmem_limit_bytes=_vmem_limit(52 << 20)),
        name="merge",
    )(x2, o_f, o_b, h_f, h_b, p2, p2, p2, p2, g1, wr_bf, wl_bf, wo_bf)


PEER_TM = 512
PEER_EBLK = 1024
PEER_SLABS = PEER_EBLK // PEER_NKEYS
PEER_LB = 256
N_CAND_ROWS = 16 + 7 * 8 + 8


def _extract_top(vals, order, n_take, on_take):
    def body(a, state):
        v, carry = state
        m = jnp.max(v, axis=0, keepdims=True)
        first = jnp.min(jnp.where(v == m, order, 1e9), axis=0, keepdims=True)
        hit = order == first
        carry = on_take(a.astype(F32), m, hit, carry)
        return jnp.where(hit, NEG_INF, v), carry
    return body, n_take


def _topk_keys(s):
    nk, n = s.shape
    order = lax.broadcasted_iota(jnp.int32, (nk, n), 0).astype(F32)
    row16 = lax.broadcasted_iota(jnp.int32, (PEER_TOPK, n), 0).astype(F32)

    def on_take(a, m, hit, carry):
        rank, top = carry
        return jnp.where(hit, a, rank), jnp.where(row16 == a, m, top)

    body, n_take = _extract_top(s, order, PEER_TOPK, on_take)
    init = (s, (jnp.full((nk, n), float(PEER_TOPK), F32), jnp.zeros((PEER_TOPK, n), F32)))
    _, (rank, top) = lax.fori_loop(0, n_take, body, init)
    return rank, top


def _route_head(s1, s2):
    n = s1.shape[1]
    r1, t1 = _topk_keys(s1)
    r2, t2 = _topk_keys(s2)
    row8 = lax.broadcasted_iota(jnp.int32, (SUBLANES, n), 0).astype(F32)
    row16 = lax.broadcasted_iota(jnp.int32, (PEER_TOPK, n), 0).astype(F32)
    cands = [t2 + t1[0:1]]
    flats = [row16]
    for a in range(1, 8):
        n_a = PEER_TOPK // (a + 1)
        cands.append(jnp.where(row8 < n_a, t2[0:8] + t1[a:a + 1], NEG_INF))
        flats.append(row8 + float(a * PEER_TOPK))
    cands.append(t1[8:16] + t2[0:1])
    flats.append((row8 + 8.0) * float(PEER_TOPK))
    cand = jnp.concatenate(cands, axis=0)
    flat = jnp.concatenate(flats, axis=0)

    def on_take(a, m, hit, sel):
        return jnp.where(hit, 1.0, sel)

    body, n_take = _extract_top(cand, flat, PEER_TOPK, on_take)
    _, sel = lax.fori_loop(0, n_take, body, (cand, jnp.zeros_like(cand)))

    cmax = cand[0:1]
    z = jnp.sum(sel * jnp.exp(cand - cmax), axis=0, keepdims=True)
    l_top = jnp.where(row8 == 0.0, jnp.sum(sel[0:16], axis=0, keepdims=True), 0.0)
    for a in range(1, 8):
        l_a = jnp.sum(sel[8 + 8 * a:16 + 8 * a], axis=0, keepdims=True)
        l_top = jnp.where(row8 == float(a), l_a, l_top)
    l_mat = jnp.concatenate([l_top, sel[72:80]], axis=0)
    l1 = jnp.zeros_like(r1)
    for a in range(PEER_TOPK):
        l1 = jnp.where(r1 == float(a), l_mat[a:a + 1], l1)
    c1 = jnp.exp(s1 - t1[0:1]) * (1.0 / z)
    e2 = jnp.exp(s2 - t2[0:1])
    return r2, e2, l1, c1


def _peer_body(x_ref, g_ref, sh_ref, sc_ref, g2_ref, fg_ref, wqh_ref, wql_ref, kh_ref, kl_ref, u_ref, vt_ref,
               o_ref, h2t_scr, s_scr, r2_scr, e2_scr, l1_scr, c1_scr, a_scr, wf_scr, acc_scr):
    j = pl.program_id(1)
    nj = pl.num_programs(1)
    tm = x_ref.shape[0]

    @pl.when(j == 0)
    def _():
        h2 = _rms_mod(x_ref[...], g_ref[...], sh_ref[0], sc_ref[0])
        h2t = h2.T
        h_hi, h_lo = _split_bf16(h2t)
        h2t_scr[...] = h_hi
        qt = _dot3(wqh_ref[...], wql_ref[...], h_hi, h_lo)
        q_hi, q_lo = _split_bf16(qt)
        for hp in range(2 * PEER_HEADS):
            sl = slice(hp * PEER_DHALF, (hp + 1) * PEER_DHALF)
            s_scr[hp] = _dot3(kh_ref[hp], kl_ref[hp], q_hi[sl], q_lo[sl])
        acc_scr[...] = jnp.zeros_like(acc_scr)

        def head_body(h, carry):
            for lb in range(tm // LANES):
                ls = slice(lb * LANES, (lb + 1) * LANES)
                r2, e2, l1, c1 = _route_head(s_scr[2 * h, :, ls], s_scr[2 * h + 1, :, ls])
                r2_scr[h, :, ls] = r2
                e2_scr[h, :, ls] = e2
                l1_scr[h, :, ls] = l1
                c1_scr[h, :, ls] = c1
            return carry

        lax.fori_loop(0, PEER_HEADS, head_body, 0)

    a_scr[...] = _dot(u_ref[...], h2t_scr[...])

    def slab_body(s, carry):
        i1 = j * PEER_SLABS + s
        r0 = pl.multiple_of(s * PEER_NKEYS, PEER_NKEYS)
        for lb in range(tm // PEER_LB):
            ls = slice(lb * PEER_LB, (lb + 1) * PEER_LB)
            w = jnp.zeros((PEER_NKEYS, PEER_LB), F32)
            for h in range(PEER_HEADS):
                lrow = l1_scr[h, pl.ds(i1, 1), ls]
                crow = c1_scr[h, pl.ds(i1, 1), ls]
                w = w + jnp.where(r2_scr[h, :, ls] < lrow, e2_scr[h, :, ls], 0.0) * crow
            act = jax.nn.gelu(a_scr[pl.ds(r0, PEER_NKEYS), ls])
            wf_scr[pl.ds(r0, PEER_NKEYS), ls] = (w * act).astype(BF16)
        return carry

    lax.fori_loop(0, PEER_SLABS, slab_body, 0)
    acc_scr[...] += _dot(vt_ref[...], wf_scr[...])

    @pl.when(j == nj - 1)
    def _():
        x2 = x_ref[...] + g2_ref[0] * acc_scr[...].T
        ms = jnp.mean(x2 * x2, axis=-1, keepdims=True)
        o_ref[...] = x2 * lax.rsqrt(ms + EPS) * fg_ref[...]


def _peer(x2, g, shift, scale, gate, final_g, wq_hi, wq_lo, k_hi, k_lo, u_bf, vt_bf, seq_len):
    t, d = x2.shape
    n_exp = u_bf.shape[0]
    tm = PEER_TM
    per_batch = seq_len // tm
    mod = lambda: pl.BlockSpec((1, 1, d), lambda i, j: (i // per_batch, 0, 0))
    vec = lambda: pl.BlockSpec((1, d), lambda i, j: (0, 0))
    nq = wq_hi.shape[0]
    tab = lambda: pltpu.VMEM((PEER_HEADS, PEER_NKEYS, tm), F32)
    return pl.pallas_call(
        _peer_body,
        grid=(t // tm, n_exp // PEER_EBLK),
        in_specs=[pl.BlockSpec((tm, d), lambda i, j: (i, 0)),
                  vec(), mod(), mod(), mod(), vec(),
                  pl.BlockSpec((nq, d), lambda i, j: (0, 0)),
                  pl.BlockSpec((nq, d), lambda i, j: (0, 0)),
                  pl.BlockSpec((2 * PEER_HEADS, PEER_NKEYS, PEER_DHALF), lambda i, j: (0, 0, 0)),
                  pl.BlockSpec((2 * PEER_HEADS, PEER_NKEYS, PEER_DHALF), lambda i, j: (0, 0, 0)),
                  pl.BlockSpec((PEER_EBLK, d), lambda i, j: (j, 0)),
                  pl.BlockSpec((d, PEER_EBLK), lambda i, j: (0, j))],
        out_specs=pl.BlockSpec((tm, d), lambda i, j: (i, 0)),
        out_shape=jax.ShapeDtypeStruct((t, d), F32),
        scratch_shapes=[pltpu.VMEM((d, tm), BF16),
                        pltpu.VMEM((2 * PEER_HEADS, PEER_NKEYS, tm), F32),
                        tab(), tab(), tab(), tab(),
                        pltpu.VMEM((PEER_EBLK, tm), F32),
                        pltpu.VMEM((PEER_EBLK, tm), BF16),
                        pltpu.VMEM((d, tm), F32)],
        compiler_params=pltpu.CompilerParams(
            dimension_semantics=("parallel", "arbitrary"), vmem_limit_bytes=_vmem_limit(56 << 20)),
        name="peer",
    )(x2, g, shift, scale, gate, final_g, wq_hi, wq_lo, k_hi, k_lo, u_bf, vt_bf)


def kernel(x, c, ctx, c_ctx, mod_w, mod_b, norm1_g, norm2_g, w_in, ret_decay, conv_w, conv_b, lru_wa, lru_ba,
           lru_wx, lru_bx, lru_lambda, w_ret_out, w_lru_out, w_out, peer_wq, peer_keys, peer_u, peer_v, final_g):
    bsz, seq, d = x.shape
    ctx_len = ctx.shape[1]
    l = 0

    pad = (-(bsz + 1)) % SUBLANES
    c_all = jnp.concatenate([c, c_ctx[None, :], jnp.zeros((pad, d), F32)], axis=0)
    mod = _modulation(c_all, mod_w[l], mod_b[l][None, :])
    mod_x = mod[:bsz].reshape(bsz, 1, 6, d)
    sh1x, sc1x, g1x, sh2x, sc2x, g2x = [mod_x[:, :, i, :] for i in range(6)]
    mod_c = mod[bsz:bsz + 1].reshape(1, 1, 6, d)
    sh1c, sc1c = mod_c[:, :, 0, :], mod_c[:, :, 1, :]

    w_in_bf = w_in[l].astype(BF16)
    g1 = norm1_g[l][None, :]
    x2 = x.reshape(bsz * seq, d)
    ctx2 = ctx.reshape(bsz * ctx_len, d)
    px = _in_projection(x2, g1, sh1x, sc1x, w_in_bf, seq)
    pc = _in_projection(ctx2, g1, sh1c, sc1c, w_in_bf, ctx_len)
    px3 = px.reshape(bsz, seq, IN_COLS)
    pc3 = pc.reshape(bsz, ctx_len, IN_COLS)

    lg = jax.nn.log_sigmoid(ret_decay[l].astype(F32))
    cos_t, sin_t = _rope_tables(seq)
    zeros_s = jnp.zeros((bsz, RET_HEADS, RET_DK, RET_DV), F32)
    _, _, sf, sb = _retention(pc3, lg, zeros_s, zeros_s, cos_t[:ctx_len], sin_t[:ctx_len], use_rope=False)
    o_f, o_b, _, _ = _retention(px3, lg, sf, sb, cos_t, sin_t, use_rope=True)

    cw = conv_w[l]
    cb = conv_b[l][None, :]
    h_dirs = []
    for dr in range(2):
        wa_bf = lru_wa[l, dr].astype(BF16)
        wx_bf = lru_wx[l, dr].astype(BF16)
        ba = lru_ba[l, dr][None, :]
        bx = lru_bx[l, dr][None, :]
        lam = lru_lambda[l, dr][None, :]
        h0 = jnp.zeros((bsz, 1, cw.shape[1]), F32)
        _, fin = _lru_direction(pc3, cw, cb, wa_bf, ba, wx_bf, bx, lam, h0, reverse=bool(dr))
        hx, _ = _lru_direction(px3, cw, cb, wa_bf, ba, wx_bf, bx, lam, fin, reverse=bool(dr))
        h_dirs.append(hx.reshape(bsz * seq, -1))

    x1 = _merge(x2, o_f.reshape(bsz * seq, -1), o_b.reshape(bsz * seq, -1), h_dirs[0], h_dirs[1], px, g1x,
                w_ret_out[l].astype(BF16), w_lru_out[l].astype(BF16), w_out[l].astype(BF16), seq)

    wq_t = peer_wq[l].T
    wq_hi = wq_t.astype(BF16)
    wq_lo = (wq_t - wq_hi.astype(F32)).astype(BF16)
    keys = peer_keys[l].reshape(2 * PEER_HEADS, PEER_NKEYS, PEER_DHALF)
    k_hi = keys.astype(BF16)
    k_lo = (keys - k_hi.astype(F32)).astype(BF16)
    u_bf = peer_u[l].astype(BF16)
    vt_bf = peer_v[l].T.astype(BF16)
    out = _peer(x1, norm2_g[l][None, :], sh2x, sc2x, g2x, final_g[None, :], wq_hi, wq_lo, k_hi, k_lo,
                u_bf, vt_bf, seq)
    return out.reshape(bsz, seq, d)
```

```python
import functools
import math

import jax
import jax.numpy as jnp
import numpy as np
from jax import lax
from jax.experimental import pallas as pl
from jax.experimental.pallas import tpu as pltpu

F32 = jnp.float32
BF16 = jnp.bfloat16

EPS = 1e-6
GRID_W = 64
ROPE_BASE = 10000.0
RET_HEADS = 4
RET_DK = 128
RET_DV = 256
RET_CHUNK = 128
LRU_BLOCKS = 8
LRU_BS = 128
LRU_C = 8.0
CONV_W = 4
PEER_HEADS = 8
PEER_DHALF = 64
PEER_NKEYS = 128
PEER_TOPK = 16
COL_Q, COL_K, COL_V, COL_RGATE, COL_LRUX, COL_LRUG, COL_BGR, COL_BGL = 0, 512, 1024, 2048, 3072, 4096, 5120, 6144
IN_COLS = 7168

V7X_VMEM_BYTES = 64 * 1024 * 1024
LANES = 128
SUBLANES = 8

NEG_INF = float("-inf")


def _vmem_limit(nbytes):
    return int(min(max(nbytes, 16 * 1024 * 1024), V7X_VMEM_BYTES - 6 * 1024 * 1024))


def _dot(a, b):
    return jnp.dot(a, b, preferred_element_type=F32)


def _split_bf16(a):
    hi = a.astype(BF16)
    lo = (a - hi.astype(F32)).astype(BF16)
    return hi, lo


def _dot3(a_hi, a_lo, b_hi, b_lo):
    return _dot(a_hi, b_hi) + (_dot(a_hi, b_lo) + _dot(a_lo, b_hi))


def _mod_body(c_ref, w_ref, b_ref, o_ref):
    c = c_ref[...]
    sc = c * jax.nn.sigmoid(c)
    c_hi, c_lo = _split_bf16(sc)
    w_hi, w_lo = _split_bf16(w_ref[...])
    o_ref[...] = _dot3(c_hi, c_lo, w_hi, w_lo) + b_ref[...]


def _modulation(c_all, mod_w, mod_b):
    rows, d = c_all.shape
    n = mod_w.shape[1]
    tn = 1536
    return pl.pallas_call(
        _mod_body,
        grid=(n // tn,),
        in_specs=[pl.BlockSpec((rows, d), lambda j: (0, 0)),
                  pl.BlockSpec((d, tn), lambda j: (0, j)),
                  pl.BlockSpec((1, tn), lambda j: (0, j))],
        out_specs=pl.BlockSpec((rows, tn), lambda j: (0, j)),
        out_shape=jax.ShapeDtypeStruct((rows, n), F32),
        compiler_params=pltpu.CompilerParams(
            dimension_semantics=("arbitrary",), vmem_limit_bytes=_vmem_limit(40 << 20)),
        name="modulation",
    )(c_all, mod_w, mod_b)


def _rms_mod(x, g, shift, scale):
    ms = jnp.mean(x * x, axis=-1, keepdims=True)
    h = x * lax.rsqrt(ms + EPS) * g
    return h * (1.0 + scale) + shift


def _inproj_body(x_ref, g_ref, sh_ref, sc_ref, w_ref, o_ref, h_scr):
    @pl.when(pl.program_id(1) == 0)
    def _():
        h_scr[...] = _rms_mod(x_ref[...], g_ref[...], sh_ref[0], sc_ref[0]).astype(BF16)

    o_ref[...] = _dot(h_scr[...], w_ref[...])


def _in_projection(x2, g, shift, scale, w_bf, seq_len):
    t, d = x2.shape
    n = w_bf.shape[1]
    tm = min(1024, seq_len)
    tn = 1024
    per_batch = seq_len // tm
    if shift.shape[0] == 1:
        mod_map = lambda i, j: (0, 0, 0)
    else:
        mod_map = lambda i, j: (i // per_batch, 0, 0)
    return pl.pallas_call(
        _inproj_body,
        grid=(t // tm, n // tn),
        in_specs=[pl.BlockSpec((tm, d), lambda i, j: (i, 0)),
                  pl.BlockSpec((1, d), lambda i, j: (0, 0)),
                  pl.BlockSpec((1, 1, d), mod_map),
                  pl.BlockSpec((1, 1, d), mod_map),
                  pl.BlockSpec((d, tn), lambda i, j: (0, j))],
        out_specs=pl.BlockSpec((tm, tn), lambda i, j: (i, j)),
        out_shape=jax.ShapeDtypeStruct((t, n), F32),
        scratch_shapes=[pltpu.VMEM((tm, d), BF16)],
        compiler_params=pltpu.CompilerParams(
            dimension_semantics=("parallel", "arbitrary"), vmem_limit_bytes=_vmem_limit(40 << 20)),
        name="in_projection",
    )(x2, g, shift, scale, w_bf)


def _rope_tables(n_tok):
    quarter = RET_DK // 4
    pos = jnp.arange(n_tok)
    row = (pos // GRID_W).astype(F32)
    col = (pos % GRID_W).astype(F32)
    inv = ROPE_BASE ** (-jnp.arange(quarter, dtype=F32) / quarter)
    ar = row[:, None] * inv[None, :]
    ac = col[:, None] * inv[None, :]
    cos_t = jnp.concatenate([jnp.cos(ar), jnp.cos(ar), jnp.cos(ac), jnp.cos(ac)], axis=-1)
    sin_t = jnp.concatenate([-jnp.sin(ar), jnp.sin(ar), -jnp.sin(ac), jnp.sin(ac)], axis=-1)
    return cos_t, sin_t


def _rope(t, cos_t, sin_t, lane_low):
    swapped = jnp.where(lane_low, pltpu.roll(t, 96, 1), pltpu.roll(t, 32, 1))
    return t * cos_t + swapped * sin_t


def _ret_body(lg_ref, qf_ref, kf_ref, vf_ref, qb_ref, kb_ref, vb_ref,
              cosf_ref, sinf_ref, cosb_ref, sinb_ref, s0f_ref, s0b_ref,
              of_ref, ob_ref, sff_ref, sfb_ref, sf_scr, sb_scr, *, use_rope, n_batch):
    c = pl.program_id(0)
    n = pl.num_programs(0)
    C = RET_CHUNK

    @pl.when(c == 0)
    def _():
        sf_scr[...] = s0f_ref[...]
        sb_scr[...] = s0b_ref[...]

    ii = lax.broadcasted_iota(jnp.int32, (C, C), 0)
    jj = lax.broadcasted_iota(jnp.int32, (C, C), 1)
    col_i = ii.astype(F32)
    lane = lax.broadcasted_iota(jnp.int32, (C, RET_DK), 1)
    lane_low = (lane % 64) < 32
    kscale = RET_DK ** -0.5

    tabs = []
    for d in range(2):
        per_head = []
        for h in range(RET_HEADS):
            lg = lg_ref[d, h]
            if d == 0:
                diff = (ii - jj).astype(F32)
                keep = ii >= jj
                xi = jnp.exp(lg * (col_i + 1.0))
                zeta = jnp.exp(lg * (C - 1.0 - col_i))
            else:
                diff = (jj - ii).astype(F32)
                keep = jj > ii
                xi = jnp.exp(lg * (C - col_i))
                zeta = jnp.exp(lg * col_i)
            dmat = jnp.where(keep, jnp.exp(lg * jnp.maximum(diff, 0.0)), 0.0)
            cdec = jnp.exp(jnp.full((1, RET_DV), lg * C, F32))
            per_head.append((dmat, xi, zeta, cdec))
        tabs.append(per_head)

    refs = ((qf_ref, kf_ref, vf_ref, cosf_ref, sinf_ref, sf_scr, of_ref),
            (qb_ref, kb_ref, vb_ref, cosb_ref, sinb_ref, sb_scr, ob_ref))

    def batch_body(b, carry):
        for d in range(2):
            q_ref, k_ref, v_ref, cos_ref, sin_ref, s_scr, o_ref = refs[d]
            for h in range(RET_HEADS):
                dmat, xi, zeta, cdec = tabs[d][h]
                q = q_ref[b, :, h * RET_DK:(h + 1) * RET_DK]
                k = k_ref[b, :, h * RET_DK:(h + 1) * RET_DK]
                v = v_ref[b, :, h * RET_DV:(h + 1) * RET_DV].astype(BF16)
                if use_rope:
                    q = _rope(q, cos_ref[...], sin_ref[...], lane_low)
                    k = _rope(k, cos_ref[...], sin_ref[...], lane_low)
                k = k * kscale
                s_old = s_scr[b, h]
                scores = lax.dot_general(q.astype(BF16), k.astype(BF16), (((1,), (1,)), ((), ())),
                                         preferred_element_type=F32) * dmat
                o = _dot(scores.astype(BF16), v) + _dot((q * xi).astype(BF16), s_old.astype(BF16))
                kz_t = (k * zeta).T.astype(BF16)
                s_scr[b, h] = cdec * s_old + _dot(kz_t, v)
                o_ref[b, :, h * RET_DV:(h + 1) * RET_DV] = o
        return carry

    lax.fori_loop(0, n_batch, batch_body, 0)

    @pl.when(c == n - 1)
    def _():
        sff_ref[...] = sf_scr[...]
        sfb_ref[...] = sb_scr[...]


def _retention(p3, lg, s0f, s0b, cos_t, sin_t, use_rope):
    bsz, seq, _ = p3.shape
    n = seq // RET_CHUNK
    C = RET_CHUNK
    qk_w = RET_HEADS * RET_DK
    v_w = RET_HEADS * RET_DV
    fwd = lambda c: c
    bwd = lambda c: n - 1 - c

    def spec3(width, col_blk, order):
        return pl.BlockSpec((bsz, C, width), lambda c: (0, order(c), col_blk))

    tab_spec = lambda order: pl.BlockSpec((C, RET_DK), lambda c: (order(c), 0))
    st_spec = pl.BlockSpec((bsz, RET_HEADS, RET_DK, RET_DV), lambda c: (0, 0, 0, 0))
    st_shape = jax.ShapeDtypeStruct((bsz, RET_HEADS, RET_DK, RET_DV), F32)
    o_shape = jax.ShapeDtypeStruct((bsz, seq, v_w), F32)
    body = functools.partial(_ret_body, use_rope=use_rope, n_batch=bsz)
    return pl.pallas_call(
        body,
        grid=(n,),
        in_specs=[pl.BlockSpec(memory_space=pltpu.SMEM),
                  spec3(qk_w, COL_Q // qk_w, fwd), spec3(qk_w, COL_K // qk_w, fwd), spec3(v_w, COL_V // v_w, fwd),
                  spec3(qk_w, COL_Q // qk_w, bwd), spec3(qk_w, COL_K // qk_w, bwd), spec3(v_w, COL_V // v_w, bwd),
                  tab_spec(fwd), tab_spec(fwd), tab_spec(bwd), tab_spec(bwd),
                  st_spec, st_spec],
        out_specs=[pl.BlockSpec((bsz, C, v_w), lambda c: (0, c, 0)),
                   pl.BlockSpec((bsz, C, v_w), lambda c: (0, n - 1 - c, 0)),
                   st_spec, st_spec],
        out_shape=[o_shape, o_shape, st_shape, st_shape],
        scratch_shapes=[pltpu.VMEM((bsz, RET_HEADS, RET_DK, RET_DV), F32),
                        pltpu.VMEM((bsz, RET_HEADS, RET_DK, RET_DV), F32)],
        compiler_params=pltpu.CompilerParams(
            dimension_semantics=("arbitrary",), vmem_limit_bytes=_vmem_limit(52 << 20)),
        name="retention",
    )(lg, p3, p3, p3, p3, p3, p3, cos_t, sin_t, cos_t, sin_t, s0f, s0b)


def _lru_body(uc_ref, up_ref, un_ref, cw_ref, cb_ref, wa_ref, ba_ref, wx_ref, bx_ref, lam_ref, h0_ref,
              h_ref, fin_ref, ext_scr, a_scr, b_scr, carry_scr, *, reverse, t_blk):
    tb = pl.program_id(1)
    nt = pl.num_programs(1)
    pos = (nt - 1 - tb) if reverse else tb
    T = t_blk

    @pl.when(tb == 0)
    def _():
        carry_scr[...] = h0_ref[0]

    has_prev = (pos > 0).astype(F32)
    has_next = (pos < nt - 1).astype(F32)
    ext_scr[0:8, :] = up_ref[0] * has_prev
    ext_scr[8:8 + T, :] = uc_ref[0]
    ext_scr[8 + T:16 + T, :] = un_ref[0] * has_next
    u = (cb_ref[...]
         + ext_scr[6:6 + T, :] * cw_ref[0:1, :]
         + ext_scr[7:7 + T, :] * cw_ref[1:2, :]
         + ext_scr[8:8 + T, :] * cw_ref[2:3, :]
         + ext_scr[9:9 + T, :] * cw_ref[3:4, :])
    ub = u.astype(BF16)
    rr = []
    xx = []
    for nb in range(LRU_BLOCKS):
        blk = ub[:, nb * LRU_BS:(nb + 1) * LRU_BS]
        rr.append(_dot(blk, wa_ref[nb]))
        xx.append(_dot(blk, wx_ref[nb]))
    r = jax.nn.sigmoid(jnp.concatenate(rr, axis=-1) + ba_ref[...])
    gate_i = jax.nn.sigmoid(jnp.concatenate(xx, axis=-1) + bx_ref[...])
    z = -lam_ref[...]
    softplus = jnp.maximum(z, 0.0) + jnp.log1p(jnp.exp(-jnp.abs(z)))
    log_a = (-LRU_C) * r * softplus
    a_scr[...] = jnp.exp(log_a)
    th = jnp.tanh(log_a)
    b_scr[...] = jnp.sqrt(-2.0 * th / (1.0 - th)) * (gate_i * u)

    row = lax.broadcasted_iota(jnp.int32, (SUBLANES, a_scr.shape[1]), 0)
    n_grp = T // SUBLANES

    def step(i, hprev):
        g = (n_grp - 1 - i) if reverse else i
        r0 = pl.multiple_of(g * SUBLANES, SUBLANES)
        a = a_scr[pl.ds(r0, SUBLANES), :]
        b = b_scr[pl.ds(r0, SUBLANES), :]
        for s in (1, 2, 4):
            if reverse:
                valid = row < SUBLANES - s
                shift = SUBLANES - s
            else:
                valid = row >= s
                shift = s
            a_s = jnp.where(valid, pltpu.roll(a, shift, 0), 1.0)
            b_s = jnp.where(valid, pltpu.roll(b, shift, 0), 0.0)
            b = a * b_s + b
            a = a * a_s
        h = a * hprev + b
        h_ref[0, pl.ds(r0, SUBLANES), :] = h
        return h[0:1, :] if reverse else h[SUBLANES - 1:SUBLANES, :]

    hfin = lax.fori_loop(0, n_grp, step, carry_scr[...])
    carry_scr[...] = hfin
    fin_ref[0] = hfin


def _lru_direction(p3, cw, cb, wa_bf, ba, wx_bf, bx, lam, h0, reverse):
    bsz, seq, _ = p3.shape
    d = cw.shape[1]
    T = min(512, seq)
    nt = seq // T
    col = COL_LRUX // d
    hb = T // SUBLANES
    n_halo = seq // SUBLANES
    order = (lambda tb: nt - 1 - tb) if reverse else (lambda tb: tb)
    body = functools.partial(_lru_body, reverse=reverse, t_blk=T)
    vec = lambda: pl.BlockSpec((1, d), lambda b, tb: (0, 0))
    return pl.pallas_call(
        body,
        grid=(bsz, nt),
        in_specs=[pl.BlockSpec((1, T, d), lambda b, tb: (b, order(tb), col)),
                  pl.BlockSpec((1, SUBLANES, d), lambda b, tb: (b, jnp.maximum(order(tb) * hb - 1, 0), col)),
                  pl.BlockSpec((1, SUBLANES, d), lambda b, tb: (b, jnp.minimum((order(tb) + 1) * hb, n_halo - 1), col)),
                  pl.BlockSpec((CONV_W, d), lambda b, tb: (0, 0)),
                  vec(),
                  pl.BlockSpec((LRU_BLOCKS, LRU_BS, LRU_BS), lambda b, tb: (0, 0, 0)),
                  vec(),
                  pl.BlockSpec((LRU_BLOCKS, LRU_BS, LRU_BS), lambda b, tb: (0, 0, 0)),
                  vec(), vec(),
                  pl.BlockSpec((1, 1, d), lambda b, tb: (b, 0, 0))],
        out_specs=[pl.BlockSpec((1, T, d), lambda b, tb: (b, order(tb), 0)),
                   pl.BlockSpec((1, 1, d), lambda b, tb: (b, 0, 0))],
        out_shape=[jax.ShapeDtypeStruct((bsz, seq, d), F32), jax.ShapeDtypeStruct((bsz, 1, d), F32)],
        scratch_shapes=[pltpu.VMEM((T + 2 * SUBLANES, d), F32), pltpu.VMEM((T, d), F32),
                        pltpu.VMEM((T, d), F32), pltpu.VMEM((1, d), F32)],
        compiler_params=pltpu.CompilerParams(
            dimension_semantics=("parallel", "arbitrary"), vmem_limit_bytes=_vmem_limit(40 << 20)),
        name="rglru_rev" if reverse else "rglru_fwd",
    )(p3, p3, p3, cw, cb, wa_bf, ba, wx_bf, bx, lam, h0)


def _merge_body(x_ref, of_ref, ob_ref, hf_ref, hb_ref, rg_ref, lg_ref, bgr_ref, bgl_ref, g1_ref,
                wr_ref, wl_ref, wo_ref, o_ref):
    o = of_ref[...] + ob_ref[...]
    parts = []
    for h in range(RET_HEADS):
        oh = o[:, h * RET_DV:(h + 1) * RET_DV]
        mu = jnp.mean(oh, axis=-1, keepdims=True)
        cen = oh - mu
        var = jnp.mean(cen * cen, axis=-1, keepdims=True)
        parts.append(cen * lax.rsqrt(var + EPS))
    on = jnp.concatenate(parts, axis=-1)
    rg = rg_ref[...]
    ret = _dot((on * (rg * jax.nn.sigmoid(rg))).astype(BF16), wr_ref[...])
    lru = _dot(((hf_ref[...] + hb_ref[...]) * jax.nn.gelu(lg_ref[...])).astype(BF16), wl_ref[...])
    y = jax.nn.sigmoid(bgr_ref[...]) * ret + jax.nn.sigmoid(bgl_ref[...]) * lru
    o_ref[...] = x_ref[...] + g1_ref[0] * _dot(y.astype(BF16), wo_ref[...])


def _merge(x2, o_f, o_b, h_f, h_b, p2, g1, wr_bf, wl_bf, wo_bf, seq_len):
    t, d = x2.shape
    tm = 512
    per_batch = seq_len // tm
    tok = lambda: pl.BlockSpec((tm, d), lambda i: (i, 0))
    pcol = lambda off: pl.BlockSpec((tm, d), lambda i: (i, off // d))
    wspec = lambda: pl.BlockSpec((d, d), lambda i: (0, 0))
    return pl.pallas_call(
        _merge_body,
        grid=(t // tm,),
        in_specs=[tok(), tok(), tok(), tok(), tok(),
                  pcol(COL_RGATE), pcol(COL_LRUG), pcol(COL_BGR), pcol(COL_BGL),
                  pl.BlockSpec((1, 1, d), lambda i: (i // per_batch, 0, 0)),
                  wspec(), wspec(), wspec()],
        out_specs=tok(),
        out_shape=jax.ShapeDtypeStruct((t, d), F32),
        compiler_params=pltpu.CompilerParams(
            dimension_semantics=("parallel",), vmem_limit_bytes=_vmem_limit(52 << 20)),
        name="merge",
    )(x2, o_f, o_b, h_f, h_b, p2, p2, p2, p2, g1, wr_bf, wl_bf, wo_bf)


PEER_TM = 512
PEER_EBLK = 1024
PEER_SLABS = PEER_EBLK // PEER_NKEYS
PEER_MM_PIECES = 8
N_CAND_ROWS = 16 + 7 * 8 + 8


_GELU_K1 = 2.0 * math.sqrt(2.0 / math.pi)
_GELU_K2 = _GELU_K1 * 0.044715


def _gelu_tanh(x):
    return x / (1.0 + jnp.exp(x * (-_GELU_K1 - _GELU_K2 * (x * x))))


def _extract_top(vals, order, n_take, on_take):
    def body(a, state):
        v, carry = state
        m = jnp.max(v, axis=0, keepdims=True)
        first = jnp.min(jnp.where(v == m, order, 1e9), axis=0, keepdims=True)
        hit = order == first
        carry = on_take(a.astype(F32), m, hit, carry)
        return jnp.where(hit, NEG_INF, v), carry
    return body, n_take


def _topk_keys(s):
    nk, n = s.shape
    order = lax.broadcasted_iota(jnp.int32, (nk, n), 0).astype(F32)
    row16 = lax.broadcasted_iota(jnp.int32, (PEER_TOPK, n), 0).astype(F32)

    def on_take(a, m, hit, carry):
        rank, top = carry
        return jnp.where(hit, a, rank), jnp.where(row16 == a, m, top)

    body, n_take = _extract_top(s, order, PEER_TOPK, on_take)
    init = (s, (jnp.full((nk, n), float(PEER_TOPK), F32), jnp.zeros((PEER_TOPK, n), F32)))
    _, (rank, top) = lax.fori_loop(0, n_take, body, init)
    return rank, top


def _route_head(s1, s2):
    n = s1.shape[1]
    r1, t1 = _topk_keys(s1)
    r2, t2 = _topk_keys(s2)
    row8 = lax.broadcasted_iota(jnp.int32, (SUBLANES, n), 0).astype(F32)
    row16 = lax.broadcasted_iota(jnp.int32, (PEER_TOPK, n), 0).astype(F32)
    cands = [t2 + t1[0:1]]
    flats = [row16]
    for a in range(1, 8):
        n_a = PEER_TOPK // (a + 1)
        cands.append(jnp.where(row8 < n_a, t2[0:8] + t1[a:a + 1], NEG_INF))
        flats.append(row8 + float(a * PEER_TOPK))
    cands.append(t1[8:16] + t2[0:1])
    flats.append((row8 + 8.0) * float(PEER_TOPK))
    cand = jnp.concatenate(cands, axis=0)
    flat = jnp.concatenate(flats, axis=0)

    def on_take(a, m, hit, sel):
        return jnp.where(hit, 1.0, sel)

    body, n_take = _extract_top(cand, flat, PEER_TOPK, on_take)
    _, sel = lax.fori_loop(0, n_take, body, (cand, jnp.zeros_like(cand)))

    cmax = cand[0:1]
    z = jnp.sum(sel * jnp.exp(cand - cmax), axis=0, keepdims=True)
    l_top = jnp.where(row8 == 0.0, jnp.sum(sel[0:16], axis=0, keepdims=True), 0.0)
    for a in range(1, 8):
        l_a = jnp.sum(sel[8 + 8 * a:16 + 8 * a], axis=0, keepdims=True)
        l_top = jnp.where(row8 == float(a), l_a, l_top)
    l_mat = jnp.concatenate([l_top, sel[72:80]], axis=0)
    l1 = jnp.zeros_like(r1)
    for a in range(PEER_TOPK):
        l1 = jnp.where(r1 == float(a), l_mat[a:a + 1], l1)
    c1 = jnp.exp(s1 - t1[0:1]) * (1.0 / z)
    e2 = jnp.exp(s2 - t2[0:1])
    return r2, e2, l1, c1


def _peer_body(x_ref, g_ref, sh_ref, sc_ref, g2_ref, fg_ref, wqh_ref, wql_ref, kh_ref, kl_ref, u_ref, vt_ref,
               o_ref, h2t_scr, s_scr, r2_scr, e2_scr, l1_scr, c1_scr, lrow_scr, crow_scr, a_scr, wf_scr, acc_scr):
    j = pl.program_id(1)
    nj = pl.num_programs(1)
    tm = x_ref.shape[0]

    @pl.when(j == 0)
    def _():
        h2 = _rms_mod(x_ref[...], g_ref[...], sh_ref[0], sc_ref[0])
        h2t = h2.T
        h_hi, h_lo = _split_bf16(h2t)
        h2t_scr[...] = h_hi
        qt = _dot3(wqh_ref[...], wql_ref[...], h_hi, h_lo)
        q_hi, q_lo = _split_bf16(qt)
        for hp in range(2 * PEER_HEADS):
            sl = slice(hp * PEER_DHALF, (hp + 1) * PEER_DHALF)
            s_scr[hp] = _dot3(kh_ref[hp], kl_ref[hp], q_hi[sl], q_lo[sl])
        acc_scr[...] = jnp.zeros_like(acc_scr)
        a_scr[...] = jnp.zeros_like(a_scr)
        wf_scr[...] = jnp.zeros_like(wf_scr)

        def head_body(h, carry):
            for lb in range(tm // LANES):
                ls = slice(lb * LANES, (lb + 1) * LANES)
                r2, e2, l1, c1 = _route_head(s_scr[2 * h, :, ls], s_scr[2 * h + 1, :, ls])
                r2_scr[h, lb] = r2.astype(BF16)
                e2_scr[h, lb] = e2.astype(BF16)
                l1_scr[h, :, ls] = l1
                c1_scr[h, :, ls] = c1
            return carry

        lax.fori_loop(0, PEER_HEADS, head_body, 0)

    def stages(new, old):
        n_lb = tm // LANES
        n_units = n_lb * PEER_SLABS
        n_pieces = PEER_MM_PIECES
        mrows = PEER_EBLK // n_pieces
        drows = acc_scr.shape[0] // n_pieces

        base = pl.multiple_of(jnp.clip((j - 1) * PEER_SLABS, 0, PEER_NKEYS - PEER_SLABS), PEER_SLABS)
        for h in range(PEER_HEADS):
            lrow_scr[h] = l1_scr[h, pl.ds(base, PEER_SLABS), :]
            crow_scr[h] = c1_scr[h, pl.ds(base, PEER_SLABS), :]

        def mm1_piece(c):
            rs = slice(c * mrows, (c + 1) * mrows)
            res = _dot(u_ref[rs, :], h2t_scr[...])
            for q in range(n_lb):
                a_scr[new, q, rs, :] = res[:, q * LANES:(q + 1) * LANES]

        def mm2_piece(c):
            rs = slice(c * drows, (c + 1) * drows)
            rhs = jnp.concatenate([wf_scr[new, q] for q in range(n_lb)], axis=1)
            acc_scr[rs, :] += _dot(vt_ref[rs, :], rhs)

        every = n_units // n_pieces
        for unit in range(n_units):
            if unit % every == 0:
                mm1_piece(unit // every)
            if unit % every == every // 2:
                mm2_piece(unit // every)
            lb, s = divmod(unit, PEER_SLABS)
            ls = slice(lb * LANES, (lb + 1) * LANES)
            rs = slice(s * PEER_NKEYS, (s + 1) * PEER_NKEYS)
            w = None
            for h in range(PEER_HEADS):
                lrow = jnp.broadcast_to(lrow_scr[h, s:s + 1, ls], (PEER_NKEYS, LANES)).astype(BF16)
                crow = jnp.broadcast_to(crow_scr[h, s:s + 1, ls], (PEER_NKEYS, LANES)).astype(BF16)
                term = jnp.where(r2_scr[h, lb] < lrow, e2_scr[h, lb], jnp.zeros((), BF16)) * crow
                w = term if w is None else w + term
            act = _gelu_tanh(a_scr[old, lb, rs, :]).astype(BF16)
            wf_scr[old, lb, rs, :] = w * act

    @pl.when(j % 2 == 0)
    def _():
        stages(0, 1)

    @pl.when(j % 2 == 1)
    def _():
        stages(1, 0)

    @pl.when(j == nj - 1)
    def _():
        x2 = x_ref[...] + g2_ref[0] * acc_scr[...].T
        ms = jnp.mean(x2 * x2, axis=-1, keepdims=True)
        o_ref[...] = x2 * lax.rsqrt(ms + EPS) * fg_ref[...]


def _peer(x2, g, shift, scale, gate, final_g, wq_hi, wq_lo, k_hi, k_lo, u_bf, vt_bf, seq_len):
    t, d = x2.shape
    n_exp = u_bf.shape[0]
    tm = PEER_TM
    per_batch = seq_len // tm
    mod = lambda: pl.BlockSpec((1, 1, d), lambda i, j: (i // per_batch, 0, 0))
    vec = lambda: pl.BlockSpec((1, d), lambda i, j: (0, 0))
    nq = wq_hi.shape[0]
    tab = lambda: pltpu.VMEM((PEER_HEADS, PEER_NKEYS, tm), F32)
    nblk = n_exp // PEER_EBLK
    return pl.pallas_call(
        _peer_body,
        grid=(t // tm, nblk + 2),
        in_specs=[pl.BlockSpec((tm, d), lambda i, j: (i, 0)),
                  vec(), mod(), mod(), mod(), vec(),
                  pl.BlockSpec((nq, d), lambda i, j: (0, 0)),
                  pl.BlockSpec((nq, d), lambda i, j: (0, 0)),
                  pl.BlockSpec((2 * PEER_HEADS, PEER_NKEYS, PEER_DHALF), lambda i, j: (0, 0, 0)),
                  pl.BlockSpec((2 * PEER_HEADS, PEER_NKEYS, PEER_DHALF), lambda i, j: (0, 0, 0)),
                  pl.BlockSpec((PEER_EBLK, d), lambda i, j: (jnp.minimum(j, nblk - 1), 0)),
                  pl.BlockSpec((d, PEER_EBLK), lambda i, j: (0, jnp.clip(j - 2, 0, nblk - 1)))],
        out_specs=pl.BlockSpec((tm, d), lambda i, j: (i, 0)),
        out_shape=jax.ShapeDtypeStruct((t, d), F32),
        scratch_shapes=[pltpu.VMEM((d, tm), BF16),
                        pltpu.VMEM((2 * PEER_HEADS, PEER_NKEYS, tm), F32),
                        pltpu.VMEM((PEER_HEADS, tm // LANES, PEER_NKEYS, LANES), BF16),
                        pltpu.VMEM((PEER_HEADS, tm // LANES, PEER_NKEYS, LANES), BF16),
                        tab(), tab(),
                        pltpu.VMEM((PEER_HEADS, PEER_SLABS, tm), F32),
                        pltpu.VMEM((PEER_HEADS, PEER_SLABS, tm), F32),
                        pltpu.VMEM((2, tm // LANES, PEER_EBLK, LANES), F32),
                        pltpu.VMEM((2, tm // LANES, PEER_EBLK, LANES), BF16),
                        pltpu.VMEM((d, tm), F32)],
        compiler_params=pltpu.CompilerParams(
            dimension_semantics=("parallel", "arbitrary"), vmem_limit_bytes=_vmem_limit(56 << 20)),
        name="peer",
    )(x2, g, shift, scale, gate, final_g, wq_hi, wq_lo, k_hi, k_lo, u_bf, vt_bf)


def kernel(x, c, ctx, c_ctx, mod_w, mod_b, norm1_g, norm2_g, w_in, ret_decay, conv_w, conv_b, lru_wa, lru_ba,
           lru_wx, lru_bx, lru_lambda, w_ret_out, w_lru_out, w_out, peer_wq, peer_keys, peer_u, peer_v, final_g):
    bsz, seq, d = x.shape
    ctx_len = ctx.shape[1]
    l = 0

    pad = (-(bsz + 1)) % SUBLANES
    c_all = jnp.concatenate([c, c_ctx[None, :], jnp.zeros((pad, d), F32)], axis=0)
    mod = _modulation(c_all, mod_w[l], mod_b[l][None, :])
    mod_x = mod[:bsz].reshape(bsz, 1, 6, d)
    sh1x, sc1x, g1x, sh2x, sc2x, g2x = [mod_x[:, :, i, :] for i in range(6)]
    mod_c = mod[bsz:bsz + 1].reshape(1, 1, 6, d)
    sh1c, sc1c = mod_c[:, :, 0, :], mod_c[:, :, 1, :]

    w_in_bf = w_in[l].astype(BF16)
    g1 = norm1_g[l][None, :]
    x2 = x.reshape(bsz * seq, d)
    ctx2 = ctx.reshape(bsz * ctx_len, d)
    px = _in_projection(x2, g1, sh1x, sc1x, w_in_bf, seq)
    pc = _in_projection(ctx2, g1, sh1c, sc1c, w_in_bf, ctx_len)
    px3 = px.reshape(bsz, seq, IN_COLS)
    pc3 = pc.reshape(bsz, ctx_len, IN_COLS)

    lg = jax.nn.log_sigmoid(ret_decay[l].astype(F32))
    cos_t, sin_t = _rope_tables(seq)
    zeros_s = jnp.zeros((bsz, RET_HEADS, RET_DK, RET_DV), F32)
    _, _, sf, sb = _retention(pc3, lg, zeros_s, zeros_s, cos_t[:ctx_len], sin_t[:ctx_len], use_rope=False)
    o_f, o_b, _, _ = _retention(px3, lg, sf, sb, cos_t, sin_t, use_rope=True)

    cw = conv_w[l]
    cb = conv_b[l][None, :]
    h_dirs = []
    for dr in range(2):
        wa_bf = lru_wa[l, dr].astype(BF16)
        wx_bf = lru_wx[l, dr].astype(BF16)
        ba = lru_ba[l, dr][None, :]
        bx = lru_bx[l, dr][None, :]
        lam = lru_lambda[l, dr][None, :]
        h0 = jnp.zeros((bsz, 1, cw.shape[1]), F32)
        _, fin = _lru_direction(pc3, cw, cb, wa_bf, ba, wx_bf, bx, lam, h0, reverse=bool(dr))
        hx, _ = _lru_direction(px3, cw, cb, wa_bf, ba, wx_bf, bx, lam, fin, reverse=bool(dr))
        h_dirs.append(hx.reshape(bsz * seq, -1))

    x1 = _merge(x2, o_f.reshape(bsz * seq, -1), o_b.reshape(bsz * seq, -1), h_dirs[0], h_dirs[1], px, g1x,
                w_ret_out[l].astype(BF16), w_lru_out[l].astype(BF16), w_out[l].astype(BF16), seq)

    wq_t = peer_wq[l].T
    wq_hi = wq_t.astype(BF16)
    wq_lo = (wq_t - wq_hi.astype(F32)).astype(BF16)
    keys = peer_keys[l].reshape(2 * PEER_HEADS, PEER_NKEYS, PEER_DHALF)
    k_hi = keys.astype(BF16)
    k_lo = (keys - k_hi.astype(F32)).astype(BF16)
    u_bf = peer_u[l].astype(BF16)
    vt_bf = peer_v[l].T.astype(BF16)
    out = _peer(x1, norm2_g[l][None, :], sh2x, sc2x, g2x, final_g[None, :], wq_hi, wq_lo, k_hi, k_lo,
                u_bf, vt_bf, seq)
    return out.reshape(bsz, seq, d)
```

```python
import functools
import math

import jax
import jax.numpy as jnp
import numpy as np
from jax import lax
from jax.experimental import pallas as pl
from jax.experimental.pallas import tpu as pltpu

F32 = jnp.float32
BF16 = jnp.bfloat16

EPS = 1e-6
GRID_W = 64
ROPE_BASE = 10000.0
RET_HEADS = 4
RET_DK = 128
RET_DV = 256
RET_CHUNK = 128
LRU_BLOCKS = 8
LRU_BS = 128
LRU_C = 8.0
CONV_W = 4
PEER_HEADS = 8
PEER_DHALF = 64
PEER_NKEYS = 128
PEER_TOPK = 16
SLAB_QK, SLAB_V, SLAB_RGATE, SLAB_LRUX, SLAB_LRUG, SLAB_BGR, SLAB_BGL = range(7)

V7X_VMEM_BYTES = 64 * 1024 * 1024
LANES = 128
SUBLANES = 8

NEG_INF = float("-inf")


def _vmem_limit(nbytes):
    return int(min(max(nbytes, 16 * 1024 * 1024), V7X_VMEM_BYTES - 6 * 1024 * 1024))


def _dot(a, b):
    return jnp.dot(a, b, preferred_element_type=F32)


def _split_bf16(a):
    hi = a.astype(BF16)
    lo = (a - hi.astype(F32)).astype(BF16)
    return hi, lo


def _dot3(a_hi, a_lo, b_hi, b_lo):
    return _dot(a_hi, b_hi) + (_dot(a_hi, b_lo) + _dot(a_lo, b_hi))


def _mod_body(c_ref, w_ref, b_ref, o_ref):
    c = c_ref[...]
    sc = c * jax.nn.sigmoid(c)
    c_hi, c_lo = _split_bf16(sc)
    w_hi, w_lo = _split_bf16(w_ref[...])
    o_ref[...] = _dot3(c_hi, c_lo, w_hi, w_lo) + b_ref[...]


def _modulation(c_all, mod_w, mod_b):
    rows, d = c_all.shape
    n = mod_w.shape[1]
    tn = 1536
    return pl.pallas_call(
        _mod_body,
        grid=(n // tn,),
        in_specs=[pl.BlockSpec((rows, d), lambda j: (0, 0)),
                  pl.BlockSpec((d, tn), lambda j: (0, j)),
                  pl.BlockSpec((1, tn), lambda j: (0, j))],
        out_specs=pl.BlockSpec((rows, tn), lambda j: (0, j)),
        out_shape=jax.ShapeDtypeStruct((rows, n), F32),
        compiler_params=pltpu.CompilerParams(
            dimension_semantics=("arbitrary",), vmem_limit_bytes=_vmem_limit(40 << 20)),
        name="modulation",
    )(c_all, mod_w, mod_b)


def _rms_mod(x, g, shift, scale):
    ms = jnp.mean(x * x, axis=-1, keepdims=True)
    h = x * lax.rsqrt(ms + EPS) * g
    return h * (1.0 + scale) + shift


def _inproj_body(x_ref, g_ref, sh_ref, sc_ref, w_ref, o_ref, h_scr):
    @pl.when(pl.program_id(1) == 0)
    def _():
        h_scr[...] = _rms_mod(x_ref[...], g_ref[...], sh_ref[0], sc_ref[0]).astype(BF16)

    o_ref[0] = _dot(h_scr[...], w_ref[...])


def _in_projection(x2, g, shift, scale, w_bf, seq_len):
    t, d = x2.shape
    n = w_bf.shape[1]
    tm = min(1024, seq_len)
    tn = 1024
    per_batch = seq_len // tm
    if shift.shape[0] == 1:
        mod_map = lambda i, j: (0, 0, 0)
    else:
        mod_map = lambda i, j: (i // per_batch, 0, 0)
    return pl.pallas_call(
        _inproj_body,
        grid=(t // tm, n // tn),
        in_specs=[pl.BlockSpec((tm, d), lambda i, j: (i, 0)),
                  pl.BlockSpec((1, d), lambda i, j: (0, 0)),
                  pl.BlockSpec((1, 1, d), mod_map),
                  pl.BlockSpec((1, 1, d), mod_map),
                  pl.BlockSpec((d, tn), lambda i, j: (0, j))],
        out_specs=pl.BlockSpec((1, tm, tn), lambda i, j: (j, i, 0)),
        out_shape=jax.ShapeDtypeStruct((n // tn, t, tn), F32),
        scratch_shapes=[pltpu.VMEM((tm, d), BF16)],
        compiler_params=pltpu.CompilerParams(
            dimension_semantics=("parallel", "arbitrary"), vmem_limit_bytes=_vmem_limit(40 << 20)),
        name="in_projection",
    )(x2, g, shift, scale, w_bf)


def _rope_tables(n_tok):
    quarter = RET_DK // 4
    pos = jnp.arange(n_tok)
    row = (pos // GRID_W).astype(F32)
    col = (pos % GRID_W).astype(F32)
    inv = ROPE_BASE ** (-jnp.arange(quarter, dtype=F32) / quarter)
    ar = row[:, None] * inv[None, :]
    ac = col[:, None] * inv[None, :]
    cos_t = jnp.concatenate([jnp.cos(ar), jnp.cos(ar), jnp.cos(ac), jnp.cos(ac)], axis=-1)
    sin_t = jnp.concatenate([-jnp.sin(ar), jnp.sin(ar), -jnp.sin(ac), jnp.sin(ac)], axis=-1)
    return cos_t, sin_t


def _rope(t, cos_t, sin_t, lane_low):
    swapped = jnp.where(lane_low, pltpu.roll(t, 96, 1), pltpu.roll(t, 32, 1))
    return t * cos_t + swapped * sin_t


def _ret_body(lg_ref, qkf_ref, vf_ref, qkb_ref, vb_ref,
              cosf_ref, sinf_ref, cosb_ref, sinb_ref, s0f_ref, s0b_ref,
              of_ref, ob_ref, sff_ref, sfb_ref, sf_scr, sb_scr, *, use_rope, n_batch):
    c = pl.program_id(0)
    n = pl.num_programs(0)
    C = RET_CHUNK

    @pl.when(c == 0)
    def _():
        sf_scr[...] = s0f_ref[...]
        sb_scr[...] = s0b_ref[...]

    ii = lax.broadcasted_iota(jnp.int32, (C, C), 0)
    jj = lax.broadcasted_iota(jnp.int32, (C, C), 1)
    col_i = ii.astype(F32)
    lane = lax.broadcasted_iota(jnp.int32, (C, RET_DK), 1)
    lane_low = (lane % 64) < 32
    kscale = RET_DK ** -0.5

    tabs = []
    for d in range(2):
        per_head = []
        for h in range(RET_HEADS):
            lg = lg_ref[d, h]
            if d == 0:
                diff = (ii - jj).astype(F32)
                keep = ii >= jj
                xi = jnp.exp(lg * (col_i + 1.0))
                zeta = jnp.exp(lg * (C - 1.0 - col_i))
            else:
                diff = (jj - ii).astype(F32)
                keep = jj > ii
                xi = jnp.exp(lg * (C - col_i))
                zeta = jnp.exp(lg * col_i)
            dmat = jnp.where(keep, jnp.exp(lg * jnp.maximum(diff, 0.0)), 0.0)
            cdec = jnp.exp(jnp.full((1, RET_DV), lg * C, F32))
            per_head.append((dmat, xi, zeta, cdec))
        tabs.append(per_head)

    refs = ((qkf_ref, vf_ref, cosf_ref, sinf_ref, sf_scr, of_ref),
            (qkb_ref, vb_ref, cosb_ref, sinb_ref, sb_scr, ob_ref))
    k_off = RET_HEADS * RET_DK

    def batch_body(b, carry):
        for d in range(2):
            qk_ref, v_ref, cos_ref, sin_ref, s_scr, o_ref = refs[d]
            for h in range(RET_HEADS):
                dmat, xi, zeta, cdec = tabs[d][h]
                q = qk_ref[0, b, :, h * RET_DK:(h + 1) * RET_DK]
                k = qk_ref[0, b, :, k_off + h * RET_DK:k_off + (h + 1) * RET_DK]
                v = v_ref[0, b, :, h * RET_DV:(h + 1) * RET_DV].astype(BF16)
                if use_rope:
                    q = _rope(q, cos_ref[...], sin_ref[...], lane_low)
                    k = _rope(k, cos_ref[...], sin_ref[...], lane_low)
                k = k * kscale
                s_old = s_scr[b, h]
                scores = lax.dot_general(q.astype(BF16), k.astype(BF16), (((1,), (1,)), ((), ())),
                                         preferred_element_type=F32) * dmat
                o = _dot(scores.astype(BF16), v) + _dot((q * xi).astype(BF16), s_old.astype(BF16))
                kz_t = (k * zeta).T.astype(BF16)
                s_scr[b, h] = cdec * s_old + _dot(kz_t, v)
                o_ref[b, :, h * RET_DV:(h + 1) * RET_DV] = o
        return carry

    lax.fori_loop(0, n_batch, batch_body, 0)

    @pl.when(c == n - 1)
    def _():
        sff_ref[...] = sf_scr[...]
        sfb_ref[...] = sb_scr[...]


def _retention(p4, lg, s0f, s0b, cos_t, sin_t, use_rope):
    _, bsz, seq, d = p4.shape
    n = seq // RET_CHUNK
    C = RET_CHUNK
    v_w = RET_HEADS * RET_DV
    fwd = lambda c: c
    bwd = lambda c: n - 1 - c

    def slab_spec(slab, order):
        return pl.BlockSpec((1, bsz, C, d), lambda c: (slab, 0, order(c), 0))

    tab_spec = lambda order: pl.BlockSpec((C, RET_DK), lambda c: (order(c), 0))
    st_spec = pl.BlockSpec((bsz, RET_HEADS, RET_DK, RET_DV), lambda c: (0, 0, 0, 0))
    st_shape = jax.ShapeDtypeStruct((bsz, RET_HEADS, RET_DK, RET_DV), F32)
    o_shape = jax.ShapeDtypeStruct((bsz, seq, v_w), F32)
    body = functools.partial(_ret_body, use_rope=use_rope, n_batch=bsz)
    return pl.pallas_call(
        body,
        grid=(n,),
        in_specs=[pl.BlockSpec(memory_space=pltpu.SMEM),
                  slab_spec(SLAB_QK, fwd), slab_spec(SLAB_V, fwd),
                  slab_spec(SLAB_QK, bwd), slab_spec(SLAB_V, bwd),
                  tab_spec(fwd), tab_spec(fwd), tab_spec(bwd), tab_spec(bwd),
                  st_spec, st_spec],
        out_specs=[pl.BlockSpec((bsz, C, v_w), lambda c: (0, c, 0)),
                   pl.BlockSpec((bsz, C, v_w), lambda c: (0, n - 1 - c, 0)),
                   st_spec, st_spec],
        out_shape=[o_shape, o_shape, st_shape, st_shape],
        scratch_shapes=[pltpu.VMEM((bsz, RET_HEADS, RET_DK, RET_DV), F32),
                        pltpu.VMEM((bsz, RET_HEADS, RET_DK, RET_DV), F32)],
        compiler_params=pltpu.CompilerParams(
            dimension_semantics=("arbitrary",), vmem_limit_bytes=_vmem_limit(52 << 20)),
        name="retention",
    )(lg, p4, p4, p4, p4, cos_t, sin_t, cos_t, sin_t, s0f, s0b)


def _lru_body(uc_ref, up_ref, un_ref, cw_ref, cb_ref, wa_ref, ba_ref, wx_ref, bx_ref, lam_ref, h0_ref,
              h_ref, fin_ref, ext_scr, a_scr, b_scr, carry_scr, *, reverse, t_blk):
    tb = pl.program_id(1)
    nt = pl.num_programs(1)
    pos = (nt - 1 - tb) if reverse else tb
    T = t_blk

    @pl.when(tb == 0)
    def _():
        carry_scr[...] = h0_ref[0]

    has_prev = (pos > 0).astype(F32)
    has_next = (pos < nt - 1).astype(F32)
    ext_scr[0:8, :] = up_ref[0, 0] * has_prev
    ext_scr[8:8 + T, :] = uc_ref[0, 0]
    ext_scr[8 + T:16 + T, :] = un_ref[0, 0] * has_next
    u = (cb_ref[...]
         + ext_scr[6:6 + T, :] * cw_ref[0:1, :]
         + ext_scr[7:7 + T, :] * cw_ref[1:2, :]
         + ext_scr[8:8 + T, :] * cw_ref[2:3, :]
         + ext_scr[9:9 + T, :] * cw_ref[3:4, :])
    ub = u.astype(BF16)
    rr = []
    xx = []
    for nb in range(LRU_BLOCKS):
        blk = ub[:, nb * LRU_BS:(nb + 1) * LRU_BS]
        rr.append(_dot(blk, wa_ref[nb]))
        xx.append(_dot(blk, wx_ref[nb]))
    r = jax.nn.sigmoid(jnp.concatenate(rr, axis=-1) + ba_ref[...])
    gate_i = jax.nn.sigmoid(jnp.concatenate(xx, axis=-1) + bx_ref[...])
    z = -lam_ref[...]
    softplus = jnp.maximum(z, 0.0) + jnp.log1p(jnp.exp(-jnp.abs(z)))
    log_a = (-LRU_C) * r * softplus
    a_scr[...] = jnp.exp(log_a)
    th = jnp.tanh(log_a)
    b_scr[...] = jnp.sqrt(-2.0 * th / (1.0 - th)) * (gate_i * u)

    row = lax.broadcasted_iota(jnp.int32, (SUBLANES, a_scr.shape[1]), 0)
    n_grp = T // SUBLANES

    def step(i, hprev):
        g = (n_grp - 1 - i) if reverse else i
        r0 = pl.multiple_of(g * SUBLANES, SUBLANES)
        a = a_scr[pl.ds(r0, SUBLANES), :]
        b = b_scr[pl.ds(r0, SUBLANES), :]
        for s in (1, 2, 4):
            if reverse:
                valid = row < SUBLANES - s
                shift = SUBLANES - s
            else:
                valid = row >= s
                shift = s
            a_s = jnp.where(valid, pltpu.roll(a, shift, 0), 1.0)
            b_s = jnp.where(valid, pltpu.roll(b, shift, 0), 0.0)
            b = a * b_s + b
            a = a * a_s
        h = a * hprev + b
        h_ref[0, pl.ds(r0, SUBLANES), :] = h
        return h[0:1, :] if reverse else h[SUBLANES - 1:SUBLANES, :]

    hfin = lax.fori_loop(0, n_grp, step, carry_scr[...])
    carry_scr[...] = hfin
    fin_ref[0] = hfin


def _lru_direction(p4, cw, cb, wa_bf, ba, wx_bf, bx, lam, h0, reverse):
    _, bsz, seq, d = p4.shape
    T = min(512, seq)
    nt = seq // T
    sl = SLAB_LRUX
    hb = T // SUBLANES
    n_halo = seq // SUBLANES
    order = (lambda tb: nt - 1 - tb) if reverse else (lambda tb: tb)
    body = functools.partial(_lru_body, reverse=reverse, t_blk=T)
    vec = lambda: pl.BlockSpec((1, d), lambda b, tb: (0, 0))
    return pl.pallas_call(
        body,
        grid=(bsz, nt),
        in_specs=[pl.BlockSpec((1, 1, T, d), lambda b, tb: (sl, b, order(tb), 0)),
                  pl.BlockSpec((1, 1, SUBLANES, d), lambda b, tb: (sl, b, jnp.maximum(order(tb) * hb - 1, 0), 0)),
                  pl.BlockSpec((1, 1, SUBLANES, d),
                               lambda b, tb: (sl, b, jnp.minimum((order(tb) + 1) * hb, n_halo - 1), 0)),
                  pl.BlockSpec((CONV_W, d), lambda b, tb: (0, 0)),
                  vec(),
                  pl.BlockSpec((LRU_BLOCKS, LRU_BS, LRU_BS), lambda b, tb: (0, 0, 0)),
                  vec(),
                  pl.BlockSpec((LRU_BLOCKS, LRU_BS, LRU_BS), lambda b, tb: (0, 0, 0)),
                  vec(), vec(),
                  pl.BlockSpec((1, 1, d), lambda b, tb: (b, 0, 0))],
        out_specs=[pl.BlockSpec((1, T, d), lambda b, tb: (b, order(tb), 0)),
                   pl.BlockSpec((1, 1, d), lambda b, tb: (b, 0, 0))],
        out_shape=[jax.ShapeDtypeStruct((bsz, seq, d), F32), jax.ShapeDtypeStruct((bsz, 1, d), F32)],
        scratch_shapes=[pltpu.VMEM((T + 2 * SUBLANES, d), F32), pltpu.VMEM((T, d), F32),
                        pltpu.VMEM((T, d), F32), pltpu.VMEM((1, d), F32)],
        compiler_params=pltpu.CompilerParams(
            dimension_semantics=("parallel", "arbitrary"), vmem_limit_bytes=_vmem_limit(40 << 20)),
        name="rglru_rev" if reverse else "rglru_fwd",
    )(p4, p4, p4, cw, cb, wa_bf, ba, wx_bf, bx, lam, h0)


def _merge_body(x_ref, of_ref, ob_ref, hf_ref, hb_ref, rg_ref, lg_ref, bgr_ref, bgl_ref, g1_ref,
                wr_ref, wl_ref, wo_ref, o_ref):
    o = of_ref[...] + ob_ref[...]
    parts = []
    for h in range(RET_HEADS):
        oh = o[:, h * RET_DV:(h + 1) * RET_DV]
        mu = jnp.mean(oh, axis=-1, keepdims=True)
        cen = oh - mu
        var = jnp.mean(cen * cen, axis=-1, keepdims=True)
        parts.append(cen * lax.rsqrt(var + EPS))
    on = jnp.concatenate(parts, axis=-1)
    rg = rg_ref[0]
    ret = _dot((on * (rg * jax.nn.sigmoid(rg))).astype(BF16), wr_ref[...])
    lru = _dot(((hf_ref[...] + hb_ref[...]) * jax.nn.gelu(lg_ref[0])).astype(BF16), wl_ref[...])
    y = jax.nn.sigmoid(bgr_ref[0]) * ret + jax.nn.sigmoid(bgl_ref[0]) * lru
    o_ref[...] = x_ref[...] + g1_ref[0] * _dot(y.astype(BF16), wo_ref[...])


def _merge(x2, o_f, o_b, h_f, h_b, p2, g1, wr_bf, wl_bf, wo_bf, seq_len):
    t, d = x2.shape
    tm = 512
    per_batch = seq_len // tm
    tok = lambda: pl.BlockSpec((tm, d), lambda i: (i, 0))
    pcol = lambda slab: pl.BlockSpec((1, tm, d), lambda i: (slab, i, 0))
    wspec = lambda: pl.BlockSpec((d, d), lambda i: (0, 0))
    return pl.pallas_call(
        _merge_body,
        grid=(t // tm,),
        in_specs=[tok(), tok(), tok(), tok(), tok(),
                  pcol(SLAB_RGATE), pcol(SLAB_LRUG), pcol(SLAB_BGR), pcol(SLAB_BGL),
                  pl.BlockSpec((1, 1, d), lambda i: (i // per_batch, 0, 0)),
                  wspec(), wspec(), wspec()],
        out_specs=tok(),
        out_shape=jax.ShapeDtypeStruct((t, d), F32),
        compiler_params=pltpu.CompilerParams(
            dimension_semantics=("parallel",), vmem_limit_bytes=_vmem_limit(52 << 20)),
        name="merge",
    )(x2, o_f, o_b, h_f, h_b, p2, p2, p2, p2, g1, wr_bf, wl_bf, wo_bf)


PEER_TM = 512
PEER_EBLK = 1024
PEER_SLABS = PEER_EBLK // PEER_NKEYS
PEER_MM_PIECES = 8
N_CAND_ROWS = 16 + 7 * 8 + 8


_GELU_K1 = 2.0 * math.sqrt(2.0 / math.pi)
_GELU_K2 = _GELU_K1 * 0.044715


def _gelu_tanh(x):
    return x / (1.0 + jnp.exp(x * (-_GELU_K1 - _GELU_K2 * (x * x))))


def _extract_top(vals, order, n_take, on_take):
    def body(a, state):
        v, carry = state
        m = jnp.max(v, axis=0, keepdims=True)
        first = jnp.min(jnp.where(v == m, order, 1e9), axis=0, keepdims=True)
        hit = order == first
        carry = on_take(jnp.asarray(a, F32), m, hit, carry)
        return jnp.where(hit, NEG_INF, v), carry
    return body, n_take


def _topk_keys(s):
    nk, n = s.shape
    order = lax.broadcasted_iota(jnp.int32, (nk, n), 0).astype(F32)
    row16 = lax.broadcasted_iota(jnp.int32, (PEER_TOPK, n), 0).astype(F32)

    def on_take(a, m, hit, carry):
        rank, top = carry
        return jnp.where(hit, a, rank), jnp.where(row16 == a, m, top)

    body, n_take = _extract_top(s, order, PEER_TOPK, on_take)
    init = (s, (jnp.full((nk, n), float(PEER_TOPK), F32), jnp.zeros((PEER_TOPK, n), F32)))
    _, (rank, top) = lax.fori_loop(0, n_take, body, init)
    return rank, top


def _route_head(s1, s2):
    n = s1.shape[1]
    r1, t1 = _topk_keys(s1)
    r2, t2 = _topk_keys(s2)
    row8 = lax.broadcasted_iota(jnp.int32, (SUBLANES, n), 0).astype(F32)
    row16 = lax.broadcasted_iota(jnp.int32, (PEER_TOPK, n), 0).astype(F32)
    cands = [t2 + t1[0:1]]
    flats = [row16]
    for a in range(1, 8):
        n_a = PEER_TOPK // (a + 1)
        cands.append(jnp.where(row8 < n_a, t2[0:8] + t1[a:a + 1], NEG_INF))
        flats.append(row8 + float(a * PEER_TOPK))
    cands.append(t1[8:16] + t2[0:1])
    flats.append((row8 + 8.0) * float(PEER_TOPK))
    cand = jnp.concatenate(cands, axis=0)
    flat = jnp.concatenate(flats, axis=0)

    def on_take(a, m, hit, sel):
        return jnp.where(hit, 1.0, sel)

    body, n_take = _extract_top(cand, flat, PEER_TOPK, on_take)
    _, sel = lax.fori_loop(0, n_take, body, (cand, jnp.zeros_like(cand)))

    cmax = cand[0:1]
    z = jnp.sum(sel * jnp.exp(cand - cmax), axis=0, keepdims=True)
    l_top = jnp.where(row8 == 0.0, jnp.sum(sel[0:16], axis=0, keepdims=True), 0.0)
    for a in range(1, 8):
        l_a = jnp.sum(sel[8 + 8 * a:16 + 8 * a], axis=0, keepdims=True)
        l_top = jnp.where(row8 == float(a), l_a, l_top)
    l_mat = jnp.concatenate([l_top, sel[72:80]], axis=0)
    l1 = jnp.zeros_like(r1)
    for a in range(PEER_TOPK):
        l1 = jnp.where(r1 == float(a), l_mat[a:a + 1], l1)
    c1 = jnp.exp(s1 - t1[0:1]) * (1.0 / z)
    e2 = jnp.exp(s2 - t2[0:1])
    return r2, e2, l1, c1


def _peer_body(x_ref, g_ref, sh_ref, sc_ref, g2_ref, fg_ref, wqh_ref, wql_ref, kh_ref, kl_ref, u_ref, vt_ref,
               o_ref, h2t_scr, s_scr, r2_scr, e2_scr, l1_scr, c1_scr, lrow_scr, crow_scr, a_scr, wf_scr, acc_scr):
    j = pl.program_id(1)
    nj = pl.num_programs(1)
    tm = x_ref.shape[0]

    @pl.when(j == 0)
    def _():
        h2 = _rms_mod(x_ref[...], g_ref[...], sh_ref[0], sc_ref[0])
        h2t = h2.T
        h_hi, h_lo = _split_bf16(h2t)
        h2t_scr[...] = h_hi
        qt = _dot3(wqh_ref[...], wql_ref[...], h_hi, h_lo)
        q_hi, q_lo = _split_bf16(qt)
        for hp in range(2 * PEER_HEADS):
            sl = slice(hp * PEER_DHALF, (hp + 1) * PEER_DHALF)
            s_scr[hp] = _dot3(kh_ref[hp], kl_ref[hp], q_hi[sl], q_lo[sl])
        acc_scr[...] = jnp.zeros_like(acc_scr)
        a_scr[...] = jnp.zeros_like(a_scr)
        wf_scr[...] = jnp.zeros_like(wf_scr)

        def head_body(h, carry):
            for lb in range(tm // LANES):
                ls = slice(lb * LANES, (lb + 1) * LANES)
                r2, e2, l1, c1 = _route_head(s_scr[2 * h, :, ls], s_scr[2 * h + 1, :, ls])
                r2_scr[h, lb] = r2.astype(BF16)
                e2_scr[h, lb] = e2.astype(BF16)
                l1_scr[h, :, ls] = l1
                c1_scr[h, :, ls] = c1
            return carry

        lax.fori_loop(0, PEER_HEADS, head_body, 0)

    def stages(new, old):
        n_lb = tm // LANES
        n_units = n_lb * PEER_SLABS
        n_pieces = PEER_MM_PIECES
        mrows = PEER_EBLK // n_pieces
        drows = acc_scr.shape[0] // n_pieces

        base = pl.multiple_of(jnp.clip((j - 1) * PEER_SLABS, 0, PEER_NKEYS - PEER_SLABS), PEER_SLABS)
        for h in range(PEER_HEADS):
            lrow_scr[h] = l1_scr[h, pl.ds(base, PEER_SLABS), :]
            crow_scr[h] = c1_scr[h, pl.ds(base, PEER_SLABS), :]

        def mm1_piece(c):
            rs = slice(c * mrows, (c + 1) * mrows)
            res = _dot(u_ref[rs, :], h2t_scr[...])
            for q in range(n_lb):
                a_scr[new, q, rs, :] = res[:, q * LANES:(q + 1) * LANES]

        def mm2_piece(c):
            rs = slice(c * drows, (c + 1) * drows)
            rhs = jnp.concatenate([wf_scr[new, q] for q in range(n_lb)], axis=1)
            acc_scr[rs, :] += _dot(vt_ref[0, rs, :], rhs)

        every = n_units // n_pieces
        for unit in range(n_units):
            if unit % every == 0:
                mm1_piece(unit // every)
            if unit % every == every // 2:
                mm2_piece(unit // every)
            lb, s = divmod(unit, PEER_SLABS)
            ls = slice(lb * LANES, (lb + 1) * LANES)
            rs = slice(s * PEER_NKEYS, (s + 1) * PEER_NKEYS)
            w = None
            for h in range(PEER_HEADS):
                lrow = jnp.broadcast_to(lrow_scr[h, s:s + 1, ls], (PEER_NKEYS, LANES)).astype(BF16)
                crow = jnp.broadcast_to(crow_scr[h, s:s + 1, ls], (PEER_NKEYS, LANES)).astype(BF16)
                term = jnp.where(r2_scr[h, lb] < lrow, e2_scr[h, lb], jnp.zeros((), BF16)) * crow
                w = term if w is None else w + term
            act = _gelu_tanh(a_scr[old, lb, rs, :]).astype(BF16)
            wf_scr[old, lb, rs, :] = w * act

    @pl.when(j % 2 == 0)
    def _():
        stages(0, 1)

    @pl.when(j % 2 == 1)
    def _():
        stages(1, 0)

    @pl.when(j == nj - 1)
    def _():
        x2 = x_ref[...] + g2_ref[0] * acc_scr[...].T
        ms = jnp.mean(x2 * x2, axis=-1, keepdims=True)
        o_ref[...] = x2 * lax.rsqrt(ms + EPS) * fg_ref[...]


def _peer(x2, g, shift, scale, gate, final_g, wq_hi, wq_lo, k_hi, k_lo, u_bf, vt_bf, seq_len):
    t, d = x2.shape
    n_exp = u_bf.shape[0]
    tm = PEER_TM
    per_batch = seq_len // tm
    mod = lambda: pl.BlockSpec((1, 1, d), lambda i, j: (i // per_batch, 0, 0))
    vec = lambda: pl.BlockSpec((1, d), lambda i, j: (0, 0))
    nq = wq_hi.shape[0]
    tab = lambda: pltpu.VMEM((PEER_HEADS, PEER_NKEYS, tm), F32)
    nblk = n_exp // PEER_EBLK
    return pl.pallas_call(
        _peer_body,
        grid=(t // tm, nblk + 2),
        in_specs=[pl.BlockSpec((tm, d), lambda i, j: (i, 0)),
                  vec(), mod(), mod(), mod(), vec(),
                  pl.BlockSpec((nq, d), lambda i, j: (0, 0)),
                  pl.BlockSpec((nq, d), lambda i, j: (0, 0)),
                  pl.BlockSpec((2 * PEER_HEADS, PEER_NKEYS, PEER_DHALF), lambda i, j: (0, 0, 0)),
                  pl.BlockSpec((2 * PEER_HEADS, PEER_NKEYS, PEER_DHALF), lambda i, j: (0, 0, 0)),
                  pl.BlockSpec((PEER_EBLK, d), lambda i, j: (jnp.minimum(j, nblk - 1), 0)),
                  pl.BlockSpec((1, d, PEER_EBLK), lambda i, j: (jnp.clip(j - 2, 0, nblk - 1), 0, 0))],
        out_specs=pl.BlockSpec((tm, d), lambda i, j: (i, 0)),
        out_shape=jax.ShapeDtypeStruct((t, d), F32),
        scratch_shapes=[pltpu.VMEM((d, tm), BF16),
                        pltpu.VMEM((2 * PEER_HEADS, PEER_NKEYS, tm), F32),
                        pltpu.VMEM((PEER_HEADS, tm // LANES, PEER_NKEYS, LANES), BF16),
                        pltpu.VMEM((PEER_HEADS, tm // LANES, PEER_NKEYS, LANES), BF16),
                        tab(), tab(),
                        pltpu.VMEM((PEER_HEADS, PEER_SLABS, tm), F32),
                        pltpu.VMEM((PEER_HEADS, PEER_SLABS, tm), F32),
                        pltpu.VMEM((2, tm // LANES, PEER_EBLK, LANES), F32),
                        pltpu.VMEM((2, tm // LANES, PEER_EBLK, LANES), BF16),
                        pltpu.VMEM((d, tm), F32)],
        compiler_params=pltpu.CompilerParams(
            dimension_semantics=("parallel", "arbitrary"), vmem_limit_bytes=_vmem_limit(56 << 20)),
        name="peer",
    )(x2, g, shift, scale, gate, final_g, wq_hi, wq_lo, k_hi, k_lo, u_bf, vt_bf)


def kernel(x, c, ctx, c_ctx, mod_w, mod_b, norm1_g, norm2_g, w_in, ret_decay, conv_w, conv_b, lru_wa, lru_ba,
           lru_wx, lru_bx, lru_lambda, w_ret_out, w_lru_out, w_out, peer_wq, peer_keys, peer_u, peer_v, final_g):
    bsz, seq, d = x.shape
    ctx_len = ctx.shape[1]
    l = 0

    pad = (-(bsz + 1)) % SUBLANES
    c_all = jnp.concatenate([c, c_ctx[None, :], jnp.zeros((pad, d), F32)], axis=0)
    mod = _modulation(c_all, mod_w[l], mod_b[l][None, :])
    mod_x = mod[:bsz].reshape(bsz, 1, 6, d)
    sh1x, sc1x, g1x, sh2x, sc2x, g2x = [mod_x[:, :, i, :] for i in range(6)]
    mod_c = mod[bsz:bsz + 1].reshape(1, 1, 6, d)
    sh1c, sc1c = mod_c[:, :, 0, :], mod_c[:, :, 1, :]

    w_in_bf = w_in[l].astype(BF16)
    g1 = norm1_g[l][None, :]
    x2 = x.reshape(bsz * seq, d)
    ctx2 = ctx.reshape(bsz * ctx_len, d)
    px = _in_projection(x2, g1, sh1x, sc1x, w_in_bf, seq)
    pc = _in_projection(ctx2, g1, sh1c, sc1c, w_in_bf, ctx_len)
    px3 = px.reshape(px.shape[0], bsz, seq, d)
    pc3 = pc.reshape(pc.shape[0], bsz, ctx_len, d)

    lg = jax.nn.log_sigmoid(ret_decay[l].astype(F32))
    cos_t, sin_t = _rope_tables(seq)
    zeros_s = jnp.zeros((bsz, RET_HEADS, RET_DK, RET_DV), F32)
    _, _, sf, sb = _retention(pc3, lg, zeros_s, zeros_s, cos_t[:ctx_len], sin_t[:ctx_len], use_rope=False)
    o_f, o_b, _, _ = _retention(px3, lg, sf, sb, cos_t, sin_t, use_rope=True)

    cw = conv_w[l]
    cb = conv_b[l][None, :]
    h_dirs = []
    for dr in range(2):
        wa_bf = lru_wa[l, dr].astype(BF16)
        wx_bf = lru_wx[l, dr].astype(BF16)
        ba = lru_ba[l, dr][None, :]
        bx = lru_bx[l, dr][None, :]
        lam = lru_lambda[l, dr][None, :]
        h0 = jnp.zeros((bsz, 1, cw.shape[1]), F32)
        _, fin = _lru_direction(pc3, cw, cb, wa_bf, ba, wx_bf, bx, lam, h0, reverse=bool(dr))
        hx, _ = _lru_direction(px3, cw, cb, wa_bf, ba, wx_bf, bx, lam, fin, reverse=bool(dr))
        h_dirs.append(hx.reshape(bsz * seq, -1))

    x1 = _merge(x2, o_f.reshape(bsz * seq, -1), o_b.reshape(bsz * seq, -1), h_dirs[0], h_dirs[1], px, g1x,
                w_ret_out[l].astype(BF16), w_lru_out[l].astype(BF16), w_out[l].astype(BF16), seq)

    wq_t = peer_wq[l].T
    wq_hi = wq_t.astype(BF16)
    wq_lo = (wq_t - wq_hi.astype(F32)).astype(BF16)
    keys = peer_keys[l].reshape(2 * PEER_HEADS, PEER_NKEYS, PEER_DHALF)
    k_hi = keys.astype(BF16)
    k_lo = (keys - k_hi.astype(F32)).astype(BF16)
    u_bf = peer_u[l].astype(BF16)
    vt_bf = peer_v[l].astype(BF16).reshape(-1, PEER_EBLK, d).transpose(0, 2, 1)
    out = _peer(x1, norm2_g[l][None, :], sh2x, sc2x, g2x, final_g[None, :], wq_hi, wq_lo, k_hi, k_lo,
                u_bf, vt_bf, seq)
    return out.reshape(bsz, seq, d)
```

```python
import functools
import math

import jax
import jax.numpy as jnp
import numpy as np
from jax import lax
from jax.experimental import pallas as pl
from jax.experimental.pallas import tpu as pltpu

F32 = jnp.float32
BF16 = jnp.bfloat16

EPS = 1e-6
GRID_W = 64
ROPE_BASE = 10000.0
RET_HEADS = 4
RET_DK = 128
RET_DV = 256
RET_CHUNK = 128
LRU_BLOCKS = 8
LRU_BS = 128
LRU_C = 8.0
CONV_W = 4
PEER_HEADS = 8
PEER_DHALF = 64
PEER_NKEYS = 128
PEER_TOPK = 16
SLAB_QK, SLAB_V, SLAB_RGATE, SLAB_LRUX, SLAB_LRUG, SLAB_BGR, SLAB_BGL = range(7)

V7X_VMEM_BYTES = 64 * 1024 * 1024
LANES = 128
SUBLANES = 8

NEG_INF = float("-inf")


def _vmem_limit(nbytes):
    return int(min(max(nbytes, 16 * 1024 * 1024), V7X_VMEM_BYTES - 6 * 1024 * 1024))


def _dot(a, b):
    return jnp.dot(a, b, preferred_element_type=F32)


def _split_bf16(a):
    hi = a.astype(BF16)
    lo = (a - hi.astype(F32)).astype(BF16)
    return hi, lo


def _dot3(a_hi, a_lo, b_hi, b_lo):
    return _dot(a_hi, b_hi) + (_dot(a_hi, b_lo) + _dot(a_lo, b_hi))


def _mod_body(c_ref, w_ref, b_ref, o_ref):
    c = c_ref[...]
    sc = c * jax.nn.sigmoid(c)
    c_hi, c_lo = _split_bf16(sc)
    w_hi, w_lo = _split_bf16(w_ref[...])
    o_ref[...] = _dot3(c_hi, c_lo, w_hi, w_lo) + b_ref[...]


def _modulation(c_all, mod_w, mod_b):
    rows, d = c_all.shape
    n = mod_w.shape[1]
    tn = 1536
    return pl.pallas_call(
        _mod_body,
        grid=(n // tn,),
        in_specs=[pl.BlockSpec((rows, d), lambda j: (0, 0)),
                  pl.BlockSpec((d, tn), lambda j: (0, j)),
                  pl.BlockSpec((1, tn), lambda j: (0, j))],
        out_specs=pl.BlockSpec((rows, tn), lambda j: (0, j)),
        out_shape=jax.ShapeDtypeStruct((rows, n), F32),
        compiler_params=pltpu.CompilerParams(
            dimension_semantics=("arbitrary",), vmem_limit_bytes=_vmem_limit(40 << 20)),
        name="modulation",
    )(c_all, mod_w, mod_b)


def _rms_mod(x, g, shift, scale):
    ms = jnp.mean(x * x, axis=-1, keepdims=True)
    h = x * lax.rsqrt(ms + EPS) * g
    return h * (1.0 + scale) + shift


def _inproj_body(x_ref, g_ref, sh_ref, sc_ref, w_ref, o_ref, h_scr):
    @pl.when(pl.program_id(1) == 0)
    def _():
        h_scr[...] = _rms_mod(x_ref[...], g_ref[...], sh_ref[0], sc_ref[0]).astype(BF16)

    o_ref[0] = _dot(h_scr[...], w_ref[...])


def _in_projection(x2, g, shift, scale, w_bf, seq_len):
    t, d = x2.shape
    n = w_bf.shape[1]
    tm = min(1024, seq_len)
    tn = 1024
    per_batch = seq_len // tm
    if shift.shape[0] == 1:
        mod_map = lambda i, j: (0, 0, 0)
    else:
        mod_map = lambda i, j: (i // per_batch, 0, 0)
    return pl.pallas_call(
        _inproj_body,
        grid=(t // tm, n // tn),
        in_specs=[pl.BlockSpec((tm, d), lambda i, j: (i, 0)),
                  pl.BlockSpec((1, d), lambda i, j: (0, 0)),
                  pl.BlockSpec((1, 1, d), mod_map),
                  pl.BlockSpec((1, 1, d), mod_map),
                  pl.BlockSpec((d, tn), lambda i, j: (0, j))],
        out_specs=pl.BlockSpec((1, tm, tn), lambda i, j: (j, i, 0)),
        out_shape=jax.ShapeDtypeStruct((n // tn, t, tn), F32),
        scratch_shapes=[pltpu.VMEM((tm, d), BF16)],
        compiler_params=pltpu.CompilerParams(
            dimension_semantics=("parallel", "arbitrary"), vmem_limit_bytes=_vmem_limit(40 << 20)),
        name="in_projection",
    )(x2, g, shift, scale, w_bf)


def _rope_tables(n_tok):
    quarter = RET_DK // 4
    pos = jnp.arange(n_tok)
    row = (pos // GRID_W).astype(F32)
    col = (pos % GRID_W).astype(F32)
    inv = ROPE_BASE ** (-jnp.arange(quarter, dtype=F32) / quarter)
    ar = row[:, None] * inv[None, :]
    ac = col[:, None] * inv[None, :]
    cos_t = jnp.concatenate([jnp.cos(ar), jnp.cos(ar), jnp.cos(ac), jnp.cos(ac)], axis=-1)
    sin_t = jnp.concatenate([-jnp.sin(ar), jnp.sin(ar), -jnp.sin(ac), jnp.sin(ac)], axis=-1)
    return cos_t, sin_t


def _rope(t, cos_t, sin_t, lane_low):
    swapped = jnp.where(lane_low, pltpu.roll(t, 96, 1), pltpu.roll(t, 32, 1))
    return t * cos_t + swapped * sin_t


def _ret_body(lg_ref, qkf_ref, vf_ref, qkb_ref, vb_ref,
              cosf_ref, sinf_ref, cosb_ref, sinb_ref, s0f_ref, s0b_ref,
              of_ref, ob_ref, sff_ref, sfb_ref, sf_scr, sb_scr, *, use_rope, n_batch):
    c = pl.program_id(0)
    n = pl.num_programs(0)
    C = RET_CHUNK

    @pl.when(c == 0)
    def _():
        sf_scr[...] = s0f_ref[...]
        sb_scr[...] = s0b_ref[...]

    ii = lax.broadcasted_iota(jnp.int32, (C, C), 0)
    jj = lax.broadcasted_iota(jnp.int32, (C, C), 1)
    col_i = ii.astype(F32)
    lane = lax.broadcasted_iota(jnp.int32, (C, RET_DK), 1)
    lane_low = (lane % 64) < 32
    kscale = RET_DK ** -0.5

    tabs = []
    for d in range(2):
        per_head = []
        for h in range(RET_HEADS):
            lg = lg_ref[d, h]
            if d == 0:
                diff = (ii - jj).astype(F32)
                keep = ii >= jj
                xi = jnp.exp(lg * (col_i + 1.0))
                zeta = jnp.exp(lg * (C - 1.0 - col_i))
            else:
                diff = (jj - ii).astype(F32)
                keep = jj > ii
                xi = jnp.exp(lg * (C - col_i))
                zeta = jnp.exp(lg * col_i)
            dmat = jnp.where(keep, jnp.exp(lg * jnp.maximum(diff, 0.0)), 0.0)
            cdec = jnp.exp(jnp.full((1, RET_DV), lg * C, F32))
            per_head.append((dmat, xi, zeta, cdec))
        tabs.append(per_head)

    refs = ((qkf_ref, vf_ref, cosf_ref, sinf_ref, sf_scr, of_ref),
            (qkb_ref, vb_ref, cosb_ref, sinb_ref, sb_scr, ob_ref))
    k_off = RET_HEADS * RET_DK

    def batch_body(b, carry):
        for d in range(2):
            qk_ref, v_ref, cos_ref, sin_ref, s_scr, o_ref = refs[d]
            for h in range(RET_HEADS):
                dmat, xi, zeta, cdec = tabs[d][h]
                q = qk_ref[0, b, :, h * RET_DK:(h + 1) * RET_DK]
                k = qk_ref[0, b, :, k_off + h * RET_DK:k_off + (h + 1) * RET_DK]
                v = v_ref[0, b, :, h * RET_DV:(h + 1) * RET_DV].astype(BF16)
                if use_rope:
                    q = _rope(q, cos_ref[...], sin_ref[...], lane_low)
                    k = _rope(k, cos_ref[...], sin_ref[...], lane_low)
                k = k * kscale
                s_old = s_scr[b, h]
                scores = lax.dot_general(q.astype(BF16), k.astype(BF16), (((1,), (1,)), ((), ())),
                                         preferred_element_type=F32) * dmat
                o = _dot(scores.astype(BF16), v) + _dot((q * xi).astype(BF16), s_old.astype(BF16))
                kz_t = (k * zeta).T.astype(BF16)
                s_scr[b, h] = cdec * s_old + _dot(kz_t, v)
                o_ref[b, :, h * RET_DV:(h + 1) * RET_DV] = o
        return carry

    lax.fori_loop(0, n_batch, batch_body, 0)

    @pl.when(c == n - 1)
    def _():
        sff_ref[...] = sf_scr[...]
        sfb_ref[...] = sb_scr[...]


def _retention(p4, lg, s0f, s0b, cos_t, sin_t, use_rope):
    _, bsz, seq, d = p4.shape
    n = seq // RET_CHUNK
    C = RET_CHUNK
    v_w = RET_HEADS * RET_DV
    fwd = lambda c: c
    bwd = lambda c: n - 1 - c

    def slab_spec(slab, order):
        return pl.BlockSpec((1, bsz, C, d), lambda c: (slab, 0, order(c), 0))

    tab_spec = lambda order: pl.BlockSpec((C, RET_DK), lambda c: (order(c), 0))
    st_spec = pl.BlockSpec((bsz, RET_HEADS, RET_DK, RET_DV), lambda c: (0, 0, 0, 0))
    st_shape = jax.ShapeDtypeStruct((bsz, RET_HEADS, RET_DK, RET_DV), F32)
    o_shape = jax.ShapeDtypeStruct((bsz, seq, v_w), F32)
    body = functools.partial(_ret_body, use_rope=use_rope, n_batch=bsz)
    return pl.pallas_call(
        body,
        grid=(n,),
        in_specs=[pl.BlockSpec(memory_space=pltpu.SMEM),
                  slab_spec(SLAB_QK, fwd), slab_spec(SLAB_V, fwd),
                  slab_spec(SLAB_QK, bwd), slab_spec(SLAB_V, bwd),
                  tab_spec(fwd), tab_spec(fwd), tab_spec(bwd), tab_spec(bwd),
                  st_spec, st_spec],
        out_specs=[pl.BlockSpec((bsz, C, v_w), lambda c: (0, c, 0)),
                   pl.BlockSpec((bsz, C, v_w), lambda c: (0, n - 1 - c, 0)),
                   st_spec, st_spec],
        out_shape=[o_shape, o_shape, st_shape, st_shape],
        scratch_shapes=[pltpu.VMEM((bsz, RET_HEADS, RET_DK, RET_DV), F32),
                        pltpu.VMEM((bsz, RET_HEADS, RET_DK, RET_DV), F32)],
        compiler_params=pltpu.CompilerParams(
            dimension_semantics=("arbitrary",), vmem_limit_bytes=_vmem_limit(52 << 20)),
        name="retention",
    )(lg, p4, p4, p4, p4, cos_t, sin_t, cos_t, sin_t, s0f, s0b)


def _lru_body(uc_ref, up_ref, un_ref, cw_ref, cb_ref, wa_ref, ba_ref, wx_ref, bx_ref, lam_ref, h0_ref,
              h_ref, fin_ref, ext_scr, a_scr, b_scr, carry_scr, *, reverse, t_blk):
    tb = pl.program_id(1)
    nt = pl.num_programs(1)
    pos = (nt - 1 - tb) if reverse else tb
    T = t_blk

    @pl.when(tb == 0)
    def _():
        carry_scr[...] = h0_ref[0]

    has_prev = (pos > 0).astype(F32)
    has_next = (pos < nt - 1).astype(F32)
    ext_scr[0:8, :] = up_ref[0, 0] * has_prev
    ext_scr[8:8 + T, :] = uc_ref[0, 0]
    ext_scr[8 + T:16 + T, :] = un_ref[0, 0] * has_next
    u = (cb_ref[...]
         + ext_scr[6:6 + T, :] * cw_ref[0:1, :]
         + ext_scr[7:7 + T, :] * cw_ref[1:2, :]
         + ext_scr[8:8 + T, :] * cw_ref[2:3, :]
         + ext_scr[9:9 + T, :] * cw_ref[3:4, :])
    ub = u.astype(BF16)
    rr = []
    xx = []
    for nb in range(LRU_BLOCKS):
        blk = ub[:, nb * LRU_BS:(nb + 1) * LRU_BS]
        rr.append(_dot(blk, wa_ref[nb]))
        xx.append(_dot(blk, wx_ref[nb]))
    r = jax.nn.sigmoid(jnp.concatenate(rr, axis=-1) + ba_ref[...])
    gate_i = jax.nn.sigmoid(jnp.concatenate(xx, axis=-1) + bx_ref[...])
    z = -lam_ref[...]
    softplus = jnp.maximum(z, 0.0) + jnp.log1p(jnp.exp(-jnp.abs(z)))
    log_a = (-LRU_C) * r * softplus
    a_scr[...] = jnp.exp(log_a)
    th = jnp.tanh(log_a)
    b_scr[...] = jnp.sqrt(-2.0 * th / (1.0 - th)) * (gate_i * u)

    row = lax.broadcasted_iota(jnp.int32, (SUBLANES, a_scr.shape[1]), 0)
    n_grp = T // SUBLANES

    def step(i, hprev):
        g = (n_grp - 1 - i) if reverse else i
        r0 = pl.multiple_of(g * SUBLANES, SUBLANES)
        a = a_scr[pl.ds(r0, SUBLANES), :]
        b = b_scr[pl.ds(r0, SUBLANES), :]
        for s in (1, 2, 4):
            if reverse:
                valid = row < SUBLANES - s
                shift = SUBLANES - s
            else:
                valid = row >= s
                shift = s
            a_s = jnp.where(valid, pltpu.roll(a, shift, 0), 1.0)
            b_s = jnp.where(valid, pltpu.roll(b, shift, 0), 0.0)
            b = a * b_s + b
            a = a * a_s
        h = a * hprev + b
        h_ref[0, pl.ds(r0, SUBLANES), :] = h
        return h[0:1, :] if reverse else h[SUBLANES - 1:SUBLANES, :]

    hfin = lax.fori_loop(0, n_grp, step, carry_scr[...])
    carry_scr[...] = hfin
    fin_ref[0] = hfin


def _lru_direction(p4, cw, cb, wa_bf, ba, wx_bf, bx, lam, h0, reverse):
    _, bsz, seq, d = p4.shape
    T = min(512, seq)
    nt = seq // T
    sl = SLAB_LRUX
    hb = T // SUBLANES
    n_halo = seq // SUBLANES
    order = (lambda tb: nt - 1 - tb) if reverse else (lambda tb: tb)
    body = functools.partial(_lru_body, reverse=reverse, t_blk=T)
    vec = lambda: pl.BlockSpec((1, d), lambda b, tb: (0, 0))
    return pl.pallas_call(
        body,
        grid=(bsz, nt),
        in_specs=[pl.BlockSpec((1, 1, T, d), lambda b, tb: (sl, b, order(tb), 0)),
                  pl.BlockSpec((1, 1, SUBLANES, d), lambda b, tb: (sl, b, jnp.maximum(order(tb) * hb - 1, 0), 0)),
                  pl.BlockSpec((1, 1, SUBLANES, d),
                               lambda b, tb: (sl, b, jnp.minimum((order(tb) + 1) * hb, n_halo - 1), 0)),
                  pl.BlockSpec((CONV_W, d), lambda b, tb: (0, 0)),
                  vec(),
                  pl.BlockSpec((LRU_BLOCKS, LRU_BS, LRU_BS), lambda b, tb: (0, 0, 0)),
                  vec(),
                  pl.BlockSpec((LRU_BLOCKS, LRU_BS, LRU_BS), lambda b, tb: (0, 0, 0)),
                  vec(), vec(),
                  pl.BlockSpec((1, 1, d), lambda b, tb: (b, 0, 0))],
        out_specs=[pl.BlockSpec((1, T, d), lambda b, tb: (b, order(tb), 0)),
                   pl.BlockSpec((1, 1, d), lambda b, tb: (b, 0, 0))],
        out_shape=[jax.ShapeDtypeStruct((bsz, seq, d), F32), jax.ShapeDtypeStruct((bsz, 1, d), F32)],
        scratch_shapes=[pltpu.VMEM((T + 2 * SUBLANES, d), F32), pltpu.VMEM((T, d), F32),
                        pltpu.VMEM((T, d), F32), pltpu.VMEM((1, d), F32)],
        compiler_params=pltpu.CompilerParams(
            dimension_semantics=("parallel", "arbitrary"), vmem_limit_bytes=_vmem_limit(40 << 20)),
        name="rglru_rev" if reverse else "rglru_fwd",
    )(p4, p4, p4, cw, cb, wa_bf, ba, wx_bf, bx, lam, h0)


def _merge_body(x_ref, of_ref, ob_ref, hf_ref, hb_ref, rg_ref, lg_ref, bgr_ref, bgl_ref, g1_ref,
                wr_ref, wl_ref, wo_ref, o_ref):
    o = of_ref[...] + ob_ref[...]
    parts = []
    for h in range(RET_HEADS):
        oh = o[:, h * RET_DV:(h + 1) * RET_DV]
        mu = jnp.mean(oh, axis=-1, keepdims=True)
        cen = oh - mu
        var = jnp.mean(cen * cen, axis=-1, keepdims=True)
        parts.append(cen * lax.rsqrt(var + EPS))
    on = jnp.concatenate(parts, axis=-1)
    rg = rg_ref[0]
    ret = _dot((on * (rg * jax.nn.sigmoid(rg))).astype(BF16), wr_ref[...])
    lru = _dot(((hf_ref[...] + hb_ref[...]) * jax.nn.gelu(lg_ref[0])).astype(BF16), wl_ref[...])
    y = jax.nn.sigmoid(bgr_ref[0]) * ret + jax.nn.sigmoid(bgl_ref[0]) * lru
    o_ref[...] = x_ref[...] + g1_ref[0] * _dot(y.astype(BF16), wo_ref[...])


def _merge(x2, o_f, o_b, h_f, h_b, p2, g1, wr_bf, wl_bf, wo_bf, seq_len):
    t, d = x2.shape
    tm = 512
    per_batch = seq_len // tm
    tok = lambda: pl.BlockSpec((tm, d), lambda i: (i, 0))
    pcol = lambda slab: pl.BlockSpec((1, tm, d), lambda i: (slab, i, 0))
    wspec = lambda: pl.BlockSpec((d, d), lambda i: (0, 0))
    return pl.pallas_call(
        _merge_body,
        grid=(t // tm,),
        in_specs=[tok(), tok(), tok(), tok(), tok(),
                  pcol(SLAB_RGATE), pcol(SLAB_LRUG), pcol(SLAB_BGR), pcol(SLAB_BGL),
                  pl.BlockSpec((1, 1, d), lambda i: (i // per_batch, 0, 0)),
                  wspec(), wspec(), wspec()],
        out_specs=tok(),
        out_shape=jax.ShapeDtypeStruct((t, d), F32),
        compiler_params=pltpu.CompilerParams(
            dimension_semantics=("parallel",), vmem_limit_bytes=_vmem_limit(52 << 20)),
        name="merge",
    )(x2, o_f, o_b, h_f, h_b, p2, p2, p2, p2, g1, wr_bf, wl_bf, wo_bf)


PEER_TM = 512
PEER_EBLK = 1024
PEER_SLABS = PEER_EBLK // PEER_NKEYS
PEER_RB = 256


_GELU_K1 = 2.0 * math.sqrt(2.0 / math.pi)
_GELU_K2 = _GELU_K1 * 0.044715


def _gelu_tanh(x):
    return x / (1.0 + jnp.exp(x * (-_GELU_K1 - _GELU_K2 * (x * x))))


def _extract_top(vals, order, n_take, on_take):
    def body(a, state):
        v, carry = state
        m = jnp.max(v, axis=0, keepdims=True)
        first = jnp.min(jnp.where(v == m, order, 1e9), axis=0, keepdims=True)
        hit = order == first
        carry = on_take(jnp.asarray(a, F32), m, hit, carry)
        return jnp.where(hit, NEG_INF, v), carry
    return body, n_take


def _topk_keys(s):
    nk, n = s.shape
    order = lax.broadcasted_iota(jnp.int32, (nk, n), 0).astype(F32)
    row16 = lax.broadcasted_iota(jnp.int32, (PEER_TOPK, n), 0).astype(F32)

    def body(a, state):
        v, top, idx = state
        af = jnp.asarray(a, F32)
        m = jnp.max(v, axis=0, keepdims=True)
        first = jnp.min(jnp.where(v == m, order, 1e9), axis=0, keepdims=True)
        take = row16 == af
        return (jnp.where(order == first, NEG_INF, v), jnp.where(take, m, top), jnp.where(take, first, idx))

    zeros = jnp.zeros((PEER_TOPK, n), F32)
    _, top, idx = lax.fori_loop(0, PEER_TOPK, body, (s, zeros, zeros))
    return idx, top


def _by_rank(idx, values, fill, shape):
    order = lax.broadcasted_iota(jnp.int32, shape, 0).astype(F32)
    out = jnp.full(shape, fill, F32)
    for a in range(PEER_TOPK):
        out = jnp.where(order == idx[a:a + 1], values[a], out)
    return out


def _route_head(s1, s2):
    n = s1.shape[1]
    i1, t1 = _topk_keys(s1)
    i2, t2 = _topk_keys(s2)
    row8 = lax.broadcasted_iota(jnp.int32, (SUBLANES, n), 0).astype(F32)
    row16 = lax.broadcasted_iota(jnp.int32, (PEER_TOPK, n), 0).astype(F32)
    cands = [t2 + t1[0:1]]
    flats = [row16]
    for a in range(1, 8):
        n_a = PEER_TOPK // (a + 1)
        cands.append(jnp.where(row8 < n_a, t2[0:8] + t1[a:a + 1], NEG_INF))
        flats.append(row8 + float(a * PEER_TOPK))
    cands.append(t1[8:16] + t2[0:1])
    flats.append((row8 + 8.0) * float(PEER_TOPK))
    cand = jnp.concatenate(cands, axis=0)
    flat = jnp.concatenate(flats, axis=0)

    def on_take(a, m, hit, sel):
        return jnp.where(hit, 1.0, sel)

    body, n_take = _extract_top(cand, flat, PEER_TOPK, on_take)
    _, sel = lax.fori_loop(0, n_take, body, (cand, jnp.zeros_like(cand)))

    cmax = cand[0:1]
    z = jnp.sum(sel * jnp.exp(cand - cmax), axis=0, keepdims=True)
    l_top = jnp.where(row8 == 0.0, jnp.sum(sel[0:16], axis=0, keepdims=True), 0.0)
    for a in range(1, 8):
        l_a = jnp.sum(sel[8 + 8 * a:16 + 8 * a], axis=0, keepdims=True)
        l_top = jnp.where(row8 == float(a), l_a, l_top)
    l_mat = jnp.concatenate([l_top, sel[72:80]], axis=0)
    l1 = _by_rank(i1, [l_mat[a:a + 1] for a in range(PEER_TOPK)], 0.0, s1.shape)
    r2 = _by_rank(i2, [float(a) for a in range(PEER_TOPK)], float(PEER_TOPK), s2.shape)
    c1 = jnp.exp(s1 - t1[0:1]) * (1.0 / z)
    e2 = jnp.exp(s2 - t2[0:1])
    return r2, e2, l1, c1


def _peer_body(x_ref, g_ref, sh_ref, sc_ref, g2_ref, fg_ref, wqh_ref, wql_ref, kh_ref, kl_ref, u_ref, vt_ref,
               o_ref, h2t_scr, s_scr, r2_scr, e2_scr, l1_scr, c1_scr, a_scr, wf_scr, acc_scr):
    j = pl.program_id(1)
    nj = pl.num_programs(1)
    tm = x_ref.shape[0]

    @pl.when(j == 0)
    def _():
        h2 = _rms_mod(x_ref[...], g_ref[...], sh_ref[0], sc_ref[0])
        h2t = h2.T
        h_hi, h_lo = _split_bf16(h2t)
        h2t_scr[...] = h_hi
        qt = _dot3(wqh_ref[...], wql_ref[...], h_hi, h_lo)
        q_hi, q_lo = _split_bf16(qt)
        for hp in range(2 * PEER_HEADS):
            sl = slice(hp * PEER_DHALF, (hp + 1) * PEER_DHALF)
            s_scr[hp] = _dot3(kh_ref[hp], kl_ref[hp], q_hi[sl], q_lo[sl])
        acc_scr[...] = jnp.zeros_like(acc_scr)

        def head_body(h, carry):
            for pb in range(tm // PEER_RB):
                ps = slice(pb * PEER_RB, (pb + 1) * PEER_RB)
                r2, e2, l1, c1 = _route_head(s_scr[2 * h, :, ps], s_scr[2 * h + 1, :, ps])
                for q in range(PEER_RB // LANES):
                    qs = slice(q * LANES, (q + 1) * LANES)
                    r2_scr[h, pb * (PEER_RB // LANES) + q] = r2[:, qs]
                    e2_scr[h, pb * (PEER_RB // LANES) + q] = e2[:, qs]
                l1_scr[h, :, ps] = l1
                c1_scr[h, :, ps] = c1
            return carry

        lax.fori_loop(0, PEER_HEADS, head_body, 0)

    n_lb = tm // LANES

    res = _dot(u_ref[...], h2t_scr[...])
    for q in range(n_lb):
        a_scr[q] = res[:, q * LANES:(q + 1) * LANES]

    def slab_body(s, carry):
        i1 = j * PEER_SLABS + s
        r0 = pl.multiple_of(s * PEER_NKEYS, PEER_NKEYS)
        lrows = [l1_scr[h, pl.ds(i1, 1), :] for h in range(PEER_HEADS)]
        crows = [c1_scr[h, pl.ds(i1, 1), :] for h in range(PEER_HEADS)]
        for lb in range(n_lb):
            ls = slice(lb * LANES, (lb + 1) * LANES)
            w = None
            for h in range(PEER_HEADS):
                term = jnp.where(r2_scr[h, lb] < lrows[h][:, ls], e2_scr[h, lb], 0.0) * crows[h][:, ls]
                w = term if w is None else w + term
            act = _gelu_tanh(a_scr[lb, pl.ds(r0, PEER_NKEYS), :])
            wf_scr[lb, pl.ds(r0, PEER_NKEYS), :] = (w * act).astype(BF16)
        return carry

    lax.fori_loop(0, PEER_SLABS, slab_body, 0)
    rhs = jnp.concatenate([wf_scr[q] for q in range(n_lb)], axis=1)
    acc_scr[...] += _dot(vt_ref[0], rhs)

    @pl.when(j == nj - 1)
    def _():
        x2 = x_ref[...] + g2_ref[0] * acc_scr[...].T
        ms = jnp.mean(x2 * x2, axis=-1, keepdims=True)
        o_ref[...] = x2 * lax.rsqrt(ms + EPS) * fg_ref[...]


def _peer(x2, g, shift, scale, gate, final_g, wq_hi, wq_lo, k_hi, k_lo, u_bf, vt_bf, seq_len):
    t, d = x2.shape
    n_exp = u_bf.shape[0]
    tm = PEER_TM
    per_batch = seq_len // tm
    mod = lambda: pl.BlockSpec((1, 1, d), lambda i, j: (i // per_batch, 0, 0))
    vec = lambda: pl.BlockSpec((1, d), lambda i, j: (0, 0))
    nq = wq_hi.shape[0]
    tab = lambda: pltpu.VMEM((PEER_HEADS, PEER_NKEYS, tm), F32)
    nblk = n_exp // PEER_EBLK
    return pl.pallas_call(
        _peer_body,
        grid=(t // tm, nblk),
        in_specs=[pl.BlockSpec((tm, d), lambda i, j: (i, 0)),
                  vec(), mod(), mod(), mod(), vec(),
                  pl.BlockSpec((nq, d), lambda i, j: (0, 0)),
                  pl.BlockSpec((nq, d), lambda i, j: (0, 0)),
                  pl.BlockSpec((2 * PEER_HEADS, PEER_NKEYS, PEER_DHALF), lambda i, j: (0, 0, 0)),
                  pl.BlockSpec((2 * PEER_HEADS, PEER_NKEYS, PEER_DHALF), lambda i, j: (0, 0, 0)),
                  pl.BlockSpec((PEER_EBLK, d), lambda i, j: (j, 0)),
                  pl.BlockSpec((1, d, PEER_EBLK), lambda i, j: (j, 0, 0))],
        out_specs=pl.BlockSpec((tm, d), lambda i, j: (i, 0)),
        out_shape=jax.ShapeDtypeStruct((t, d), F32),
        scratch_shapes=[pltpu.VMEM((d, tm), BF16),
                        pltpu.VMEM((2 * PEER_HEADS, PEER_NKEYS, tm), F32),
                        pltpu.VMEM((PEER_HEADS, tm // LANES, PEER_NKEYS, LANES), F32),
                        pltpu.VMEM((PEER_HEADS, tm // LANES, PEER_NKEYS, LANES), F32),
                        tab(), tab(),
                        pltpu.VMEM((tm // LANES, PEER_EBLK, LANES), F32),
                        pltpu.VMEM((tm // LANES, PEER_EBLK, LANES), BF16),
                        pltpu.VMEM((d, tm), F32)],
        compiler_params=pltpu.CompilerParams(
            dimension_semantics=("parallel", "arbitrary"), vmem_limit_bytes=_vmem_limit(56 << 20)),
        name="peer",
    )(x2, g, shift, scale, gate, final_g, wq_hi, wq_lo, k_hi, k_lo, u_bf, vt_bf)


def kernel(x, c, ctx, c_ctx, mod_w, mod_b, norm1_g, norm2_g, w_in, ret_decay, conv_w, conv_b, lru_wa, lru_ba,
           lru_wx, lru_bx, lru_lambda, w_ret_out, w_lru_out, w_out, peer_wq, peer_keys, peer_u, peer_v, final_g):
    bsz, seq, d = x.shape
    ctx_len = ctx.shape[1]
    l = 0

    pad = (-(bsz + 1)) % SUBLANES
    c_all = jnp.concatenate([c, c_ctx[None, :], jnp.zeros((pad, d), F32)], axis=0)
    mod = _modulation(c_all, mod_w[l], mod_b[l][None, :])
    mod_x = mod[:bsz].reshape(bsz, 1, 6, d)
    sh1x, sc1x, g1x, sh2x, sc2x, g2x = [mod_x[:, :, i, :] for i in range(6)]
    mod_c = mod[bsz:bsz + 1].reshape(1, 1, 6, d)
    sh1c, sc1c = mod_c[:, :, 0, :], mod_c[:, :, 1, :]

    w_in_bf = w_in[l].astype(BF16)
    g1 = norm1_g[l][None, :]
    x2 = x.reshape(bsz * seq, d)
    ctx2 = ctx.reshape(bsz * ctx_len, d)
    px = _in_projection(x2, g1, sh1x, sc1x, w_in_bf, seq)
    pc = _in_projection(ctx2, g1, sh1c, sc1c, w_in_bf, ctx_len)
    px3 = px.reshape(px.shape[0], bsz, seq, d)
    pc3 = pc.reshape(pc.shape[0], bsz, ctx_len, d)

    lg = jax.nn.log_sigmoid(ret_decay[l].astype(F32))
    cos_t, sin_t = _rope_tables(seq)
    zeros_s = jnp.zeros((bsz, RET_HEADS, RET_DK, RET_DV), F32)
    _, _, sf, sb = _retention(pc3, lg, zeros_s, zeros_s, cos_t[:ctx_len], sin_t[:ctx_len], use_rope=False)
    o_f, o_b, _, _ = _retention(px3, lg, sf, sb, cos_t, sin_t, use_rope=True)

    cw = conv_w[l]
    cb = conv_b[l][None, :]
    h_dirs = []
    for dr in range(2):
        wa_bf = lru_wa[l, dr].astype(BF16)
        wx_bf = lru_wx[l, dr].astype(BF16)
        ba = lru_ba[l, dr][None, :]
        bx = lru_bx[l, dr][None, :]
        lam = lru_lambda[l, dr][None, :]
        h0 = jnp.zeros((bsz, 1, cw.shape[1]), F32)
        _, fin = _lru_direction(pc3, cw, cb, wa_bf, ba, wx_bf, bx, lam, h0, reverse=bool(dr))
        hx, _ = _lru_direction(px3, cw, cb, wa_bf, ba, wx_bf, bx, lam, fin, reverse=bool(dr))
        h_dirs.append(hx.reshape(bsz * seq, -1))

    x1 = _merge(x2, o_f.reshape(bsz * seq, -1), o_b.reshape(bsz * seq, -1), h_dirs[0], h_dirs[1], px, g1x,
                w_ret_out[l].astype(BF16), w_lru_out[l].astype(BF16), w_out[l].astype(BF16), seq)

    wq_t = peer_wq[l].T
    wq_hi = wq_t.astype(BF16)
    wq_lo = (wq_t - wq_hi.astype(F32)).astype(BF16)
    keys = peer_keys[l].reshape(2 * PEER_HEADS, PEER_NKEYS, PEER_DHALF)
    k_hi = keys.astype(BF16)
    k_lo = (keys - k_hi.astype(F32)).astype(BF16)
    u_bf = peer_u[l].astype(BF16)
    vt_bf = peer_v[l].astype(BF16).reshape(-1, PEER_EBLK, d).transpose(0, 2, 1)
    out = _peer(x1, norm2_g[l][None, :], sh2x, sc2x, g2x, final_g[None, :], wq_hi, wq_lo, k_hi, k_lo,
                u_bf, vt_bf, seq)
    return out.reshape(bsz, seq, d)
```

```python
import functools
import math

import jax
import jax.numpy as jnp
import numpy as np
from jax import lax
from jax.experimental import pallas as pl
from jax.experimental.pallas import tpu as pltpu

F32 = jnp.float32
BF16 = jnp.bfloat16

EPS = 1e-6
GRID_W = 64
ROPE_BASE = 10000.0
RET_HEADS = 4
RET_DK = 128
RET_DV = 256
RET_CHUNK = 128
LRU_BLOCKS = 8
LRU_BS = 128
LRU_C = 8.0
CONV_W = 4
PEER_HEADS = 8
PEER_DHALF = 64
PEER_NKEYS = 128
PEER_TOPK = 16
SLAB_QK, SLAB_V, SLAB_RGATE, SLAB_LRUX, SLAB_LRUG, SLAB_BGR, SLAB_BGL = range(7)

V7X_VMEM_BYTES = 64 * 1024 * 1024
LANES = 128
SUBLANES = 8

NEG_INF = float("-inf")


def _vmem_limit(nbytes):
    return int(min(max(nbytes, 16 * 1024 * 1024), V7X_VMEM_BYTES - 6 * 1024 * 1024))


def _dot(a, b):
    return jnp.dot(a, b, preferred_element_type=F32)


def _split_bf16(a):
    hi = a.astype(BF16)
    lo = (a - hi.astype(F32)).astype(BF16)
    return hi, lo


def _dot3(a_hi, a_lo, b_hi, b_lo):
    return _dot(a_hi, b_hi) + (_dot(a_hi, b_lo) + _dot(a_lo, b_hi))


def _mod_body(c_ref, w_ref, b_ref, o_ref):
    c = c_ref[...]
    sc = c * jax.nn.sigmoid(c)
    c_hi, c_lo = _split_bf16(sc)
    w_hi, w_lo = _split_bf16(w_ref[...])
    o_ref[...] = _dot3(c_hi, c_lo, w_hi, w_lo) + b_ref[...]


def _modulation(c_all, mod_w, mod_b):
    rows, d = c_all.shape
    n = mod_w.shape[1]
    tn = 1536
    return pl.pallas_call(
        _mod_body,
        grid=(n // tn,),
        in_specs=[pl.BlockSpec((rows, d), lambda j: (0, 0)),
                  pl.BlockSpec((d, tn), lambda j: (0, j)),
                  pl.BlockSpec((1, tn), lambda j: (0, j))],
        out_specs=pl.BlockSpec((rows, tn), lambda j: (0, j)),
        out_shape=jax.ShapeDtypeStruct((rows, n), F32),
        compiler_params=pltpu.CompilerParams(
            dimension_semantics=("arbitrary",), vmem_limit_bytes=_vmem_limit(40 << 20)),
        name="modulation",
    )(c_all, mod_w, mod_b)


INPROJ_TM = 256

def _rms_mod(x, g, shift, scale):
    ms = jnp.mean(x * x, axis=-1, keepdims=True)
    h = x * lax.rsqrt(ms + EPS) * g
    return h * (1.0 + scale) + shift


def _inproj_body(x_ref, g_ref, sh_ref, sc_ref, w_ref, o_ref):
    h = _rms_mod(x_ref[...], g_ref[...], sh_ref[0], sc_ref[0]).astype(BF16)
    d = h.shape[1]
    for k in range(o_ref.shape[0]):
        o_ref[k] = _dot(h, w_ref[:, k * d:(k + 1) * d])


def _in_projection(x2, g, shift, scale, w_bf, seq_len):
    t, d = x2.shape
    n = w_bf.shape[1]
    tm = min(INPROJ_TM, seq_len)
    per_batch = seq_len // tm
    if shift.shape[0] == 1:
        mod_map = lambda i: (0, 0, 0)
    else:
        mod_map = lambda i: (i // per_batch, 0, 0)
    vmem = 2 * (d * n * 2 + (n // d) * tm * d * 4 + tm * d * 4) + (4 << 20)
    return pl.pallas_call(
        _inproj_body,
        grid=(t // tm,),
        in_specs=[pl.BlockSpec((tm, d), lambda i: (i, 0)),
                  pl.BlockSpec((1, d), lambda i: (0, 0)),
                  pl.BlockSpec((1, 1, d), mod_map),
                  pl.BlockSpec((1, 1, d), mod_map),
                  pl.BlockSpec((d, n), lambda i: (0, 0))],
        out_specs=pl.BlockSpec((n // d, tm, d), lambda i: (0, i, 0)),
        out_shape=jax.ShapeDtypeStruct((n // d, t, d), F32),
        compiler_params=pltpu.CompilerParams(
            dimension_semantics=("parallel",), vmem_limit_bytes=_vmem_limit(vmem)),
        name="in_projection",
    )(x2, g, shift, scale, w_bf)


def _rope_tables(n_tok):
    quarter = RET_DK // 4
    pos = jnp.arange(n_tok)
    row = (pos // GRID_W).astype(F32)
    col = (pos % GRID_W).astype(F32)
    inv = ROPE_BASE ** (-jnp.arange(quarter, dtype=F32) / quarter)
    ar = row[:, None] * inv[None, :]
    ac = col[:, None] * inv[None, :]
    cos_t = jnp.concatenate([jnp.cos(ar), jnp.cos(ar), jnp.cos(ac), jnp.cos(ac)], axis=-1)
    sin_t = jnp.concatenate([-jnp.sin(ar), jnp.sin(ar), -jnp.sin(ac), jnp.sin(ac)], axis=-1)
    return cos_t, sin_t


def _rope(t, cos_t, sin_t, lane_low):
    swapped = jnp.where(lane_low, pltpu.roll(t, 96, 1), pltpu.roll(t, 32, 1))
    return t * cos_t + swapped * sin_t


def _ret_body(lg_ref, qkf_ref, vf_ref, qkb_ref, vb_ref,
              cosf_ref, sinf_ref, cosb_ref, sinb_ref, s0f_ref, s0b_ref,
              of_ref, ob_ref, sff_ref, sfb_ref, sf_scr, sb_scr, *, use_rope, n_batch):
    c = pl.program_id(0)
    n = pl.num_programs(0)
    C = RET_CHUNK

    @pl.when(c == 0)
    def _():
        sf_scr[...] = s0f_ref[...]
        sb_scr[...] = s0b_ref[...]

    ii = lax.broadcasted_iota(jnp.int32, (C, C), 0)
    jj = lax.broadcasted_iota(jnp.int32, (C, C), 1)
    col_i = ii.astype(F32)
    lane = lax.broadcasted_iota(jnp.int32, (C, RET_DK), 1)
    lane_low = (lane % 64) < 32
    kscale = RET_DK ** -0.5

    tabs = []
    for d in range(2):
        per_head = []
        for h in range(RET_HEADS):
            lg = lg_ref[d, h]
            if d == 0:
                diff = (ii - jj).astype(F32)
                keep = ii >= jj
                xi = jnp.exp(lg * (col_i + 1.0))
                zeta = jnp.exp(lg * (C - 1.0 - col_i))
            else:
                diff = (jj - ii).astype(F32)
                keep = jj > ii
                xi = jnp.exp(lg * (C - col_i))
                zeta = jnp.exp(lg * col_i)
            dmat = jnp.where(keep, jnp.exp(lg * jnp.maximum(diff, 0.0)), 0.0)
            cdec = jnp.exp(jnp.full((1, RET_DV), lg * C, F32))
            per_head.append((dmat, xi, zeta, cdec))
        tabs.append(per_head)

    refs = ((qkf_ref, vf_ref, cosf_ref, sinf_ref, sf_scr, of_ref),
            (qkb_ref, vb_ref, cosb_ref, sinb_ref, sb_scr, ob_ref))
    k_off = RET_HEADS * RET_DK

    def batch_body(b, carry):
        for d in range(2):
            qk_ref, v_ref, cos_ref, sin_ref, s_scr, o_ref = refs[d]
            for h in range(RET_HEADS):
                dmat, xi, zeta, cdec = tabs[d][h]
                q = qk_ref[0, b, :, h * RET_DK:(h + 1) * RET_DK]
                k = qk_ref[0, b, :, k_off + h * RET_DK:k_off + (h + 1) * RET_DK]
                v = v_ref[0, b, :, h * RET_DV:(h + 1) * RET_DV].astype(BF16)
                if use_rope:
                    q = _rope(q, cos_ref[...], sin_ref[...], lane_low)
                    k = _rope(k, cos_ref[...], sin_ref[...], lane_low)
                k = k * kscale
                s_old = s_scr[b, h]
                scores = lax.dot_general(q.astype(BF16), k.astype(BF16), (((1,), (1,)), ((), ())),
                                         preferred_element_type=F32) * dmat
                o = _dot(scores.astype(BF16), v) + _dot((q * xi).astype(BF16), s_old.astype(BF16))
                kz_t = (k * zeta).T.astype(BF16)
                s_scr[b, h] = cdec * s_old + _dot(kz_t, v)
                o_ref[b, :, h * RET_DV:(h + 1) * RET_DV] = o
        return carry

    lax.fori_loop(0, n_batch, batch_body, 0)

    @pl.when(c == n - 1)
    def _():
        sff_ref[...] = sf_scr[...]
        sfb_ref[...] = sb_scr[...]


def _retention(p4, lg, s0f, s0b, cos_t, sin_t, use_rope):
    _, bsz, seq, d = p4.shape
    n = seq // RET_CHUNK
    C = RET_CHUNK
    v_w = RET_HEADS * RET_DV
    fwd = lambda c: c
    bwd = lambda c: n - 1 - c

    def slab_spec(slab, order):
        return pl.BlockSpec((1, bsz, C, d), lambda c: (slab, 0, order(c), 0))

    tab_spec = lambda order: pl.BlockSpec((C, RET_DK), lambda c: (order(c), 0))
    st_spec = pl.BlockSpec((bsz, RET_HEADS, RET_DK, RET_DV), lambda c: (0, 0, 0, 0))
    st_shape = jax.ShapeDtypeStruct((bsz, RET_HEADS, RET_DK, RET_DV), F32)
    o_shape = jax.ShapeDtypeStruct((bsz, seq, v_w), F32)
    body = functools.partial(_ret_body, use_rope=use_rope, n_batch=bsz)
    return pl.pallas_call(
        body,
        grid=(n,),
        in_specs=[pl.BlockSpec(memory_space=pltpu.SMEM),
                  slab_spec(SLAB_QK, fwd), slab_spec(SLAB_V, fwd),
                  slab_spec(SLAB_QK, bwd), slab_spec(SLAB_V, bwd),
                  tab_spec(fwd), tab_spec(fwd), tab_spec(bwd), tab_spec(bwd),
                  st_spec, st_spec],
        out_specs=[pl.BlockSpec((bsz, C, v_w), lambda c: (0, c, 0)),
                   pl.BlockSpec((bsz, C, v_w), lambda c: (0, n - 1 - c, 0)),
                   st_spec, st_spec],
        out_shape=[o_shape, o_shape, st_shape, st_shape],
        scratch_shapes=[pltpu.VMEM((bsz, RET_HEADS, RET_DK, RET_DV), F32),
                        pltpu.VMEM((bsz, RET_HEADS, RET_DK, RET_DV), F32)],
        compiler_params=pltpu.CompilerParams(
            dimension_semantics=("arbitrary",), vmem_limit_bytes=_vmem_limit(52 << 20)),
        name="retention",
    )(lg, p4, p4, p4, p4, cos_t, sin_t, cos_t, sin_t, s0f, s0b)


def _lru_body(uc_ref, up_ref, un_ref, cw_ref, cb_ref, wa_ref, ba_ref, wx_ref, bx_ref, lam_ref, h0_ref,
              h_ref, fin_ref, ext_scr, a_scr, b_scr, carry_scr, *, reverse, t_blk):
    tb = pl.program_id(1)
    nt = pl.num_programs(1)
    pos = (nt - 1 - tb) if reverse else tb
    T = t_blk

    @pl.when(tb == 0)
    def _():
        carry_scr[...] = h0_ref[0]

    has_prev = (pos > 0).astype(F32)
    has_next = (pos < nt - 1).astype(F32)
    ext_scr[0:8, :] = up_ref[0, 0] * has_prev
    ext_scr[8:8 + T, :] = uc_ref[0, 0]
    ext_scr[8 + T:16 + T, :] = un_ref[0, 0] * has_next
    u = (cb_ref[...]
         + ext_scr[6:6 + T, :] * cw_ref[0:1, :]
         + ext_scr[7:7 + T, :] * cw_ref[1:2, :]
         + ext_scr[8:8 + T, :] * cw_ref[2:3, :]
         + ext_scr[9:9 + T, :] * cw_ref[3:4, :])
    ub = u.astype(BF16)
    rr = []
    xx = []
    for nb in range(LRU_BLOCKS):
        blk = ub[:, nb * LRU_BS:(nb + 1) * LRU_BS]
        rr.append(_dot(blk, wa_ref[nb]))
        xx.append(_dot(blk, wx_ref[nb]))
    r = jax.nn.sigmoid(jnp.concatenate(rr, axis=-1) + ba_ref[...])
    gate_i = jax.nn.sigmoid(jnp.concatenate(xx, axis=-1) + bx_ref[...])
    z = -lam_ref[...]
    softplus = jnp.maximum(z, 0.0) + jnp.log1p(jnp.exp(-jnp.abs(z)))
    log_a = (-LRU_C) * r * softplus
    a_scr[...] = jnp.exp(log_a)
    th = jnp.tanh(log_a)
    b_scr[...] = jnp.sqrt(-2.0 * th / (1.0 - th)) * (gate_i * u)

    row = lax.broadcasted_iota(jnp.int32, (SUBLANES, a_scr.shape[1]), 0)
    n_grp = T // SUBLANES

    def step(i, hprev):
        g = (n_grp - 1 - i) if reverse else i
        r0 = pl.multiple_of(g * SUBLANES, SUBLANES)
        a = a_scr[pl.ds(r0, SUBLANES), :]
        b = b_scr[pl.ds(r0, SUBLANES), :]
        for s in (1, 2, 4):
            if reverse:
                valid = row < SUBLANES - s
                shift = SUBLANES - s
            else:
                valid = row >= s
                shift = s
            a_s = jnp.where(valid, pltpu.roll(a, shift, 0), 1.0)
            b_s = jnp.where(valid, pltpu.roll(b, shift, 0), 0.0)
            b = a * b_s + b
            a = a * a_s
        h = a * hprev + b
        h_ref[0, pl.ds(r0, SUBLANES), :] = h
        return h[0:1, :] if reverse else h[SUBLANES - 1:SUBLANES, :]

    hfin = lax.fori_loop(0, n_grp, step, carry_scr[...])
    carry_scr[...] = hfin
    fin_ref[0] = hfin


def _lru_direction(p4, cw, cb, wa_bf, ba, wx_bf, bx, lam, h0, reverse):
    _, bsz, seq, d = p4.shape
    T = min(512, seq)
    nt = seq // T
    sl = SLAB_LRUX
    hb = T // SUBLANES
    n_halo = seq // SUBLANES
    order = (lambda tb: nt - 1 - tb) if reverse else (lambda tb: tb)
    body = functools.partial(_lru_body, reverse=reverse, t_blk=T)
    vec = lambda: pl.BlockSpec((1, d), lambda b, tb: (0, 0))
    return pl.pallas_call(
        body,
        grid=(bsz, nt),
        in_specs=[pl.BlockSpec((1, 1, T, d), lambda b, tb: (sl, b, order(tb), 0)),
                  pl.BlockSpec((1, 1, SUBLANES, d), lambda b, tb: (sl, b, jnp.maximum(order(tb) * hb - 1, 0), 0)),
                  pl.BlockSpec((1, 1, SUBLANES, d),
                               lambda b, tb: (sl, b, jnp.minimum((order(tb) + 1) * hb, n_halo - 1), 0)),
                  pl.BlockSpec((CONV_W, d), lambda b, tb: (0, 0)),
                  vec(),
                  pl.BlockSpec((LRU_BLOCKS, LRU_BS, LRU_BS), lambda b, tb: (0, 0, 0)),
                  vec(),
                  pl.BlockSpec((LRU_BLOCKS, LRU_BS, LRU_BS), lambda b, tb: (0, 0, 0)),
                  vec(), vec(),
                  pl.BlockSpec((1, 1, d), lambda b, tb: (b, 0, 0))],
        out_specs=[pl.BlockSpec((1, T, d), lambda b, tb: (b, order(tb), 0)),
                   pl.BlockSpec((1, 1, d), lambda b, tb: (b, 0, 0))],
        out_shape=[jax.ShapeDtypeStruct((bsz, seq, d), F32), jax.ShapeDtypeStruct((bsz, 1, d), F32)],
        scratch_shapes=[pltpu.VMEM((T + 2 * SUBLANES, d), F32), pltpu.VMEM((T, d), F32),
                        pltpu.VMEM((T, d), F32), pltpu.VMEM((1, d), F32)],
        compiler_params=pltpu.CompilerParams(
            dimension_semantics=("parallel", "arbitrary"), vmem_limit_bytes=_vmem_limit(40 << 20)),
        name="rglru_rev" if reverse else "rglru_fwd",
    )(p4, p4, p4, cw, cb, wa_bf, ba, wx_bf, bx, lam, h0)


def _merge_body(x_ref, of_ref, ob_ref, hf_ref, hb_ref, rg_ref, lg_ref, bgr_ref, bgl_ref, g1_ref,
                wr_ref, wl_ref, wo_ref, o_ref):
    o = of_ref[...] + ob_ref[...]
    parts = []
    for h in range(RET_HEADS):
        oh = o[:, h * RET_DV:(h + 1) * RET_DV]
        mu = jnp.mean(oh, axis=-1, keepdims=True)
        cen = oh - mu
        var = jnp.mean(cen * cen, axis=-1, keepdims=True)
        parts.append(cen * lax.rsqrt(var + EPS))
    on = jnp.concatenate(parts, axis=-1)
    rg = rg_ref[0]
    ret = _dot((on * (rg * jax.nn.sigmoid(rg))).astype(BF16), wr_ref[...])
    lru = _dot(((hf_ref[...] + hb_ref[...]) * jax.nn.gelu(lg_ref[0])).astype(BF16), wl_ref[...])
    y = jax.nn.sigmoid(bgr_ref[0]) * ret + jax.nn.sigmoid(bgl_ref[0]) * lru
    o_ref[...] = x_ref[...] + g1_ref[0] * _dot(y.astype(BF16), wo_ref[...])


def _merge(x2, o_f, o_b, h_f, h_b, p2, g1, wr_bf, wl_bf, wo_bf, seq_len):
    t, d = x2.shape
    tm = 512
    per_batch = seq_len // tm
    tok = lambda: pl.BlockSpec((tm, d), lambda i: (i, 0))
    pcol = lambda slab: pl.BlockSpec((1, tm, d), lambda i: (slab, i, 0))
    wspec = lambda: pl.BlockSpec((d, d), lambda i: (0, 0))
    return pl.pallas_call(
        _merge_body,
        grid=(t // tm,),
        in_specs=[tok(), tok(), tok(), tok(), tok(),
                  pcol(SLAB_RGATE), pcol(SLAB_LRUG), pcol(SLAB_BGR), pcol(SLAB_BGL),
                  pl.BlockSpec((1, 1, d), lambda i: (i // per_batch, 0, 0)),
                  wspec(), wspec(), wspec()],
        out_specs=tok(),
        out_shape=jax.ShapeDtypeStruct((t, d), F32),
        compiler_params=pltpu.CompilerParams(
            dimension_semantics=("parallel",), vmem_limit_bytes=_vmem_limit(52 << 20)),
        name="merge",
    )(x2, o_f, o_b, h_f, h_b, p2, p2, p2, p2, g1, wr_bf, wl_bf, wo_bf)


PEER_TM = 512
PEER_EBLK = 1024
PEER_SLABS = PEER_EBLK // PEER_NKEYS
PEER_RB = 256


_GELU_K1 = 2.0 * math.sqrt(2.0 / math.pi)
_GELU_K2 = _GELU_K1 * 0.044715


def _gelu_tanh(x):
    return x / (1.0 + jnp.exp(x * (-_GELU_K1 - _GELU_K2 * (x * x))))


def _max_first(v, order_groups):
    groups = [(v[k * SUBLANES:(k + 1) * SUBLANES], o) for k, o in enumerate(order_groups)]
    while len(groups) > 1:
        nxt = []
        for a in range(0, len(groups) - 1, 2):
            (va, oa), (vb, ob) = groups[a], groups[a + 1]
            later = vb > va
            nxt.append((jnp.where(later, vb, va), jnp.where(later, ob, oa)))
        if len(groups) % 2:
            nxt.append(groups[-1])
        groups = nxt
    v8, o8 = groups[0]
    m = jnp.max(v8, axis=0, keepdims=True)
    first = jnp.min(jnp.where(v8 == m, o8, 1e9), axis=0, keepdims=True)
    return m, first


def _extract_top(vals, order_groups, n_take, on_take):
    order = jnp.concatenate(order_groups, axis=0)

    def body(a, state):
        v, carry = state
        m, first = _max_first(v, order_groups)
        hit = order == first
        carry = on_take(jnp.asarray(a, F32), m, hit, carry)
        return jnp.where(hit, NEG_INF, v), carry
    return body, n_take


def _topk_keys(scores):
    nk, n = scores[0].shape
    order = lax.broadcasted_iota(jnp.int32, (nk, n), 0).astype(F32)
    row8 = lax.broadcasted_iota(jnp.int32, (SUBLANES, n), 0).astype(F32)
    order_groups = [row8 + float(k * SUBLANES) for k in range(nk // SUBLANES)]
    row16 = lax.broadcasted_iota(jnp.int32, (PEER_TOPK, n), 0).astype(F32)

    def body(a, state):
        take = row16 == jnp.asarray(a, F32)
        out = []
        for v, top, idx in state:
            m, first = _max_first(v, order_groups)
            out.append((jnp.where(order == first, NEG_INF, v), jnp.where(take, m, top),
                        jnp.where(take, first, idx)))
        return tuple(out)

    zeros = jnp.zeros((PEER_TOPK, n), F32)
    res = lax.fori_loop(0, PEER_TOPK, body, tuple((s, zeros, zeros) for s in scores))
    return [(idx, top) for _, top, idx in res]


def _by_rank(idx, values, fill, shape):
    order = lax.broadcasted_iota(jnp.int32, shape, 0).astype(F32)
    out = jnp.full(shape, fill, F32)
    for a in range(PEER_TOPK):
        out = jnp.where(order == idx[a:a + 1], values[a], out)
    return out


def _route_head(s1, s2):
    n = s1.shape[1]
    (i1, t1), = _topk_keys([s1])
    (i2, t2), = _topk_keys([s2])
    row8 = lax.broadcasted_iota(jnp.int32, (SUBLANES, n), 0).astype(F32)
    row16 = lax.broadcasted_iota(jnp.int32, (PEER_TOPK, n), 0).astype(F32)
    cands = [t2 + t1[0:1]]
    flats = [row8, row8 + float(SUBLANES)]
    for a in range(1, 8):
        n_a = PEER_TOPK // (a + 1)
        cands.append(jnp.where(row8 < n_a, t2[0:8] + t1[a:a + 1], NEG_INF))
        flats.append(row8 + float(a * PEER_TOPK))
    cands.append(t1[8:16] + t2[0:1])
    flats.append((row8 + 8.0) * float(PEER_TOPK))
    cand = jnp.concatenate(cands, axis=0)

    def on_take(a, m, hit, sel):
        return jnp.where(hit, 1.0, sel)

    body, n_take = _extract_top(cand, flats, PEER_TOPK, on_take)
    _, sel = lax.fori_loop(0, n_take, body, (cand, jnp.zeros_like(cand)))

    cmax = cand[0:1]
    z = jnp.sum(sel * jnp.exp(cand - cmax), axis=0, keepdims=True)
    l_top = jnp.where(row8 == 0.0, jnp.sum(sel[0:16], axis=0, keepdims=True), 0.0)
    for a in range(1, 8):
        l_a = jnp.sum(sel[8 + 8 * a:16 + 8 * a], axis=0, keepdims=True)
        l_top = jnp.where(row8 == float(a), l_a, l_top)
    l_mat = jnp.concatenate([l_top, sel[72:80]], axis=0)
    l1 = _by_rank(i1, [l_mat[a:a + 1] for a in range(PEER_TOPK)], 0.0, s1.shape)
    r2 = _by_rank(i2, [float(a) for a in range(PEER_TOPK)], float(PEER_TOPK), s2.shape)
    c1 = jnp.exp(s1 - t1[0:1]) * (1.0 / z)
    e2 = jnp.exp(s2 - t2[0:1])
    return r2, e2, l1, c1


def _peer_body(x_ref, g_ref, sh_ref, sc_ref, g2_ref, fg_ref, wqh_ref, wql_ref, kh_ref, kl_ref, u_ref, vt_ref,
               o_ref, h2t_scr, s_scr, r2_scr, e2_scr, l1_scr, c1_scr, a_scr, wf_scr, acc_scr):
    j = pl.program_id(1)
    nj = pl.num_programs(1)
    tm = x_ref.shape[0]

    @pl.when(j == 0)
    def _():
        h2 = _rms_mod(x_ref[...], g_ref[...], sh_ref[0], sc_ref[0])
        h2t = h2.T
        h_hi, h_lo = _split_bf16(h2t)
        h2t_scr[...] = h_hi
        qt = _dot3(wqh_ref[...], wql_ref[...], h_hi, h_lo)
        q_hi, q_lo = _split_bf16(qt)
        for hp in range(2 * PEER_HEADS):
            sl = slice(hp * PEER_DHALF, (hp + 1) * PEER_DHALF)
            s_scr[hp] = _dot3(kh_ref[hp], kl_ref[hp], q_hi[sl], q_lo[sl])
        acc_scr[...] = jnp.zeros_like(acc_scr)

        def head_body(h, carry):
            for pb in range(tm // PEER_RB):
                ps = slice(pb * PEER_RB, (pb + 1) * PEER_RB)
                r2, e2, l1, c1 = _route_head(s_scr[2 * h, :, ps], s_scr[2 * h + 1, :, ps])
                for q in range(PEER_RB // LANES):
                    qs = slice(q * LANES, (q + 1) * LANES)
                    r2_scr[h, pb * (PEER_RB // LANES) + q] = r2[:, qs]
                    e2_scr[h, pb * (PEER_RB // LANES) + q] = e2[:, qs]
                l1_scr[h, :, ps] = l1
                c1_scr[h, :, ps] = c1
            return carry

        lax.fori_loop(0, PEER_HEADS, head_body, 0)

    n_lb = tm // LANES

    res = _dot(u_ref[...], h2t_scr[...])
    for q in range(n_lb):
        a_scr[q] = res[:, q * LANES:(q + 1) * LANES]

    def slab_body(s, carry):
        i1 = j * PEER_SLABS + s
        r0 = pl.multiple_of(s * PEER_NKEYS, PEER_NKEYS)
        lrows = [l1_scr[h, pl.ds(i1, 1), :] for h in range(PEER_HEADS)]
        crows = [c1_scr[h, pl.ds(i1, 1), :] for h in range(PEER_HEADS)]
        for lb in range(n_lb):
            ls = slice(lb * LANES, (lb + 1) * LANES)
            w = None
            for h in range(PEER_HEADS):
                term = jnp.where(r2_scr[h, lb] < lrows[h][:, ls], e2_scr[h, lb], 0.0) * crows[h][:, ls]
                w = term if w is None else w + term
            act = _gelu_tanh(a_scr[lb, pl.ds(r0, PEER_NKEYS), :])
            wf_scr[lb, pl.ds(r0, PEER_NKEYS), :] = (w * act).astype(BF16)
        return carry

    lax.fori_loop(0, PEER_SLABS, slab_body, 0)
    rhs = jnp.concatenate([wf_scr[q] for q in range(n_lb)], axis=1)
    acc_scr[...] += _dot(vt_ref[0], rhs)

    @pl.when(j == nj - 1)
    def _():
        x2 = x_ref[...] + g2_ref[0] * acc_scr[...].T
        ms = jnp.mean(x2 * x2, axis=-1, keepdims=True)
        o_ref[...] = x2 * lax.rsqrt(ms + EPS) * fg_ref[...]


def _peer(x2, g, shift, scale, gate, final_g, wq_hi, wq_lo, k_hi, k_lo, u_bf, vt_bf, seq_len):
    t, d = x2.shape
    n_exp = u_bf.shape[0]
    tm = PEER_TM
    per_batch = seq_len // tm
    mod = lambda: pl.BlockSpec((1, 1, d), lambda i, j: (i // per_batch, 0, 0))
    vec = lambda: pl.BlockSpec((1, d), lambda i, j: (0, 0))
    nq = wq_hi.shape[0]
    tab = lambda: pltpu.VMEM((PEER_HEADS, PEER_NKEYS, tm), F32)
    nblk = n_exp // PEER_EBLK
    return pl.pallas_call(
        _peer_body,
        grid=(t // tm, nblk),
        in_specs=[pl.BlockSpec((tm, d), lambda i, j: (i, 0)),
                  vec(), mod(), mod(), mod(), vec(),
                  pl.BlockSpec((nq, d), lambda i, j: (0, 0)),
                  pl.BlockSpec((nq, d), lambda i, j: (0, 0)),
                  pl.BlockSpec((2 * PEER_HEADS, PEER_NKEYS, PEER_DHALF), lambda i, j: (0, 0, 0)),
                  pl.BlockSpec((2 * PEER_HEADS, PEER_NKEYS, PEER_DHALF), lambda i, j: (0, 0, 0)),
                  pl.BlockSpec((PEER_EBLK, d), lambda i, j: (j, 0)),
                  pl.BlockSpec((1, d, PEER_EBLK), lambda i, j: (j, 0, 0))],
        out_specs=pl.BlockSpec((tm, d), lambda i, j: (i, 0)),
        out_shape=jax.ShapeDtypeStruct((t, d), F32),
        scratch_shapes=[pltpu.VMEM((d, tm), BF16),
                        pltpu.VMEM((2 * PEER_HEADS, PEER_NKEYS, tm), F32),
                        pltpu.VMEM((PEER_HEADS, tm // LANES, PEER_NKEYS, LANES), F32),
                        pltpu.VMEM((PEER_HEADS, tm // LANES, PEER_NKEYS, LANES), F32),
                        tab(), tab(),
                        pltpu.VMEM((tm // LANES, PEER_EBLK, LANES), F32),
                        pltpu.VMEM((tm // LANES, PEER_EBLK, LANES), BF16),
                        pltpu.VMEM((d, tm), F32)],
        compiler_params=pltpu.CompilerParams(
            dimension_semantics=("parallel", "arbitrary"), vmem_limit_bytes=_vmem_limit(56 << 20)),
        name="peer",
    )(x2, g, shift, scale, gate, final_g, wq_hi, wq_lo, k_hi, k_lo, u_bf, vt_bf)


def kernel(x, c, ctx, c_ctx, mod_w, mod_b, norm1_g, norm2_g, w_in, ret_decay, conv_w, conv_b, lru_wa, lru_ba,
           lru_wx, lru_bx, lru_lambda, w_ret_out, w_lru_out, w_out, peer_wq, peer_keys, peer_u, peer_v, final_g):
    bsz, seq, d = x.shape
    ctx_len = ctx.shape[1]
    l = 0

    pad = (-(bsz + 1)) % SUBLANES
    c_all = jnp.concatenate([c, c_ctx[None, :], jnp.zeros((pad, d), F32)], axis=0)
    mod = _modulation(c_all, mod_w[l], mod_b[l][None, :])
    mod_x = mod[:bsz].reshape(bsz, 1, 6, d)
    sh1x, sc1x, g1x, sh2x, sc2x, g2x = [mod_x[:, :, i, :] for i in range(6)]
    mod_c = mod[bsz:bsz + 1].reshape(1, 1, 6, d)
    sh1c, sc1c = mod_c[:, :, 0, :], mod_c[:, :, 1, :]

    w_in_bf = w_in[l].astype(BF16)
    g1 = norm1_g[l][None, :]
    x2 = x.reshape(bsz * seq, d)
    ctx2 = ctx.reshape(bsz * ctx_len, d)
    px = _in_projection(x2, g1, sh1x, sc1x, w_in_bf, seq)
    pc = _in_projection(ctx2, g1, sh1c, sc1c, w_in_bf, ctx_len)
    px3 = px.reshape(px.shape[0], bsz, seq, d)
    pc3 = pc.reshape(pc.shape[0], bsz, ctx_len, d)

    lg = jax.nn.log_sigmoid(ret_decay[l].astype(F32))
    cos_t, sin_t = _rope_tables(seq)
    zeros_s = jnp.zeros((bsz, RET_HEADS, RET_DK, RET_DV), F32)
    _, _, sf, sb = _retention(pc3, lg, zeros_s, zeros_s, cos_t[:ctx_len], sin_t[:ctx_len], use_rope=False)
    o_f, o_b, _, _ = _retention(px3, lg, sf, sb, cos_t, sin_t, use_rope=True)

    cw = conv_w[l]
    cb = conv_b[l][None, :]
    h_dirs = []
    for dr in range(2):
        wa_bf = lru_wa[l, dr].astype(BF16)
        wx_bf = lru_wx[l, dr].astype(BF16)
        ba = lru_ba[l, dr][None, :]
        bx = lru_bx[l, dr][None, :]
        lam = lru_lambda[l, dr][None, :]
        h0 = jnp.zeros((bsz, 1, cw.shape[1]), F32)
        _, fin = _lru_direction(pc3, cw, cb, wa_bf, ba, wx_bf, bx, lam, h0, reverse=bool(dr))
        hx, _ = _lru_direction(px3, cw, cb, wa_bf, ba, wx_bf, bx, lam, fin, reverse=bool(dr))
        h_dirs.append(hx.reshape(bsz * seq, -1))

    x1 = _merge(x2, o_f.reshape(bsz * seq, -1), o_b.reshape(bsz * seq, -1), h_dirs[0], h_dirs[1], px, g1x,
                w_ret_out[l].astype(BF16), w_lru_out[l].astype(BF16), w_out[l].astype(BF16), seq)

    wq_t = peer_wq[l].T
    wq_hi = wq_t.astype(BF16)
    wq_lo = (wq_t - wq_hi.astype(F32)).astype(BF16)
    keys = peer_keys[l].reshape(2 * PEER_HEADS, PEER_NKEYS, PEER_DHALF)
    k_hi = keys.astype(BF16)
    k_lo = (keys - k_hi.astype(F32)).astype(BF16)
    u_bf = peer_u[l].astype(BF16)
    vt_bf = peer_v[l].astype(BF16).reshape(-1, PEER_EBLK, d).transpose(0, 2, 1)
    out = _peer(x1, norm2_g[l][None, :], sh2x, sc2x, g2x, final_g[None, :], wq_hi, wq_lo, k_hi, k_lo,
                u_bf, vt_bf, seq)
    return out.reshape(bsz, seq, d)
```

```python
import functools
import math

import jax
import jax.numpy as jnp
import numpy as np
from jax import lax
from jax.experimental import pallas as pl
from jax.experimental.pallas import tpu as pltpu

F32 = jnp.float32
BF16 = jnp.bfloat16

EPS = 1e-6
GRID_W = 64
ROPE_BASE = 10000.0
RET_HEADS = 4
RET_DK = 128
RET_DV = 256
RET_CHUNK = 128
LRU_BLOCKS = 8
LRU_BS = 128
LRU_C = 8.0
CONV_W = 4
PEER_HEADS = 8
PEER_DHALF = 64
PEER_NKEYS = 128
PEER_TOPK = 16
SLAB_QK, SLAB_V, SLAB_RGATE, SLAB_LRUX, SLAB_LRUG, SLAB_BGR, SLAB_BGL = range(7)

V7X_VMEM_BYTES = 64 * 1024 * 1024
LANES = 128
SUBLANES = 8

NEG_INF = float("-inf")


def _vmem_limit(nbytes):
    return int(min(max(nbytes, 16 * 1024 * 1024), V7X_VMEM_BYTES - 6 * 1024 * 1024))


def _dot(a, b):
    return jnp.dot(a, b, preferred_element_type=F32)


def _split_bf16(a):
    hi = a.astype(BF16)
    lo = (a - hi.astype(F32)).astype(BF16)
    return hi, lo


def _dot3(a_hi, a_lo, b_hi, b_lo):
    return _dot(a_hi, b_hi) + (_dot(a_hi, b_lo) + _dot(a_lo, b_hi))


def _mod_body(c_ref, w_ref, b_ref, o_ref):
    c = c_ref[...]
    sc = c * jax.nn.sigmoid(c)
    c_hi, c_lo = _split_bf16(sc)
    w_hi, w_lo = _split_bf16(w_ref[...])
    o_ref[...] = _dot3(c_hi, c_lo, w_hi, w_lo) + b_ref[...]


def _modulation(c_all, mod_w, mod_b):
    rows, d = c_all.shape
    n = mod_w.shape[1]
    tn = 1536
    return pl.pallas_call(
        _mod_body,
        grid=(n // tn,),
        in_specs=[pl.BlockSpec((rows, d), lambda j: (0, 0)),
                  pl.BlockSpec((d, tn), lambda j: (0, j)),
                  pl.BlockSpec((1, tn), lambda j: (0, j))],
        out_specs=pl.BlockSpec((rows, tn), lambda j: (0, j)),
        out_shape=jax.ShapeDtypeStruct((rows, n), F32),
        compiler_params=pltpu.CompilerParams(
            dimension_semantics=("arbitrary",), vmem_limit_bytes=_vmem_limit(40 << 20)),
        name="modulation",
    )(c_all, mod_w, mod_b)


INPROJ_TM = 256

def _rms_mod(x, g, shift, scale):
    ms = jnp.mean(x * x, axis=-1, keepdims=True)
    h = x * lax.rsqrt(ms + EPS) * g
    return h * (1.0 + scale) + shift


def _inproj_body(x_ref, g_ref, sh_ref, sc_ref, w_ref, o_ref):
    h = _rms_mod(x_ref[...], g_ref[...], sh_ref[0], sc_ref[0]).astype(BF16)
    d = h.shape[1]
    for k in range(o_ref.shape[0]):
        o_ref[k] = _dot(h, w_ref[:, k * d:(k + 1) * d])


def _in_projection(x2, g, shift, scale, w_bf, seq_len):
    t, d = x2.shape
    n = w_bf.shape[1]
    tm = min(INPROJ_TM, seq_len)
    per_batch = seq_len // tm
    if shift.shape[0] == 1:
        mod_map = lambda i: (0, 0, 0)
    else:
        mod_map = lambda i: (i // per_batch, 0, 0)
    vmem = 2 * (d * n * 2 + (n // d) * tm * d * 4 + tm * d * 4) + (4 << 20)
    return pl.pallas_call(
        _inproj_body,
        grid=(t // tm,),
        in_specs=[pl.BlockSpec((tm, d), lambda i: (i, 0)),
                  pl.BlockSpec((1, d), lambda i: (0, 0)),
                  pl.BlockSpec((1, 1, d), mod_map),
                  pl.BlockSpec((1, 1, d), mod_map),
                  pl.BlockSpec((d, n), lambda i: (0, 0))],
        out_specs=pl.BlockSpec((n // d, tm, d), lambda i: (0, i, 0)),
        out_shape=jax.ShapeDtypeStruct((n // d, t, d), F32),
        compiler_params=pltpu.CompilerParams(
            dimension_semantics=("parallel",), vmem_limit_bytes=_vmem_limit(vmem)),
        name="in_projection",
    )(x2, g, shift, scale, w_bf)


def _rope_tables(n_tok):
    quarter = RET_DK // 4
    pos = jnp.arange(n_tok)
    row = (pos // GRID_W).astype(F32)
    col = (pos % GRID_W).astype(F32)
    inv = ROPE_BASE ** (-jnp.arange(quarter, dtype=F32) / quarter)
    ar = row[:, None] * inv[None, :]
    ac = col[:, None] * inv[None, :]
    cos_t = jnp.concatenate([jnp.cos(ar), jnp.cos(ar), jnp.cos(ac), jnp.cos(ac)], axis=-1)
    sin_t = jnp.concatenate([-jnp.sin(ar), jnp.sin(ar), -jnp.sin(ac), jnp.sin(ac)], axis=-1)
    return cos_t, sin_t


def _rope(t, cos_t, sin_t, lane_low):
    swapped = jnp.where(lane_low, pltpu.roll(t, 96, 1), pltpu.roll(t, 32, 1))
    return t * cos_t + swapped * sin_t


def _ret_body(lg_ref, qkf_ref, vf_ref, qkb_ref, vb_ref,
              cosf_ref, sinf_ref, cosb_ref, sinb_ref, s0f_ref, s0b_ref,
              of_ref, ob_ref, sff_ref, sfb_ref, sf_scr, sb_scr, *, use_rope, n_batch):
    c = pl.program_id(0)
    n = pl.num_programs(0)
    C = RET_CHUNK

    @pl.when(c == 0)
    def _():
        sf_scr[...] = s0f_ref[...]
        sb_scr[...] = s0b_ref[...]

    ii = lax.broadcasted_iota(jnp.int32, (C, C), 0)
    jj = lax.broadcasted_iota(jnp.int32, (C, C), 1)
    col_i = ii.astype(F32)
    lane = lax.broadcasted_iota(jnp.int32, (C, RET_DK), 1)
    lane_low = (lane % 64) < 32
    kscale = RET_DK ** -0.5

    tabs = []
    for d in range(2):
        per_head = []
        for h in range(RET_HEADS):
            lg = lg_ref[d, h]
            if d == 0:
                diff = (ii - jj).astype(F32)
                keep = ii >= jj
                xi = jnp.exp(lg * (col_i + 1.0))
                zeta = jnp.exp(lg * (C - 1.0 - col_i))
            else:
                diff = (jj - ii).astype(F32)
                keep = jj > ii
                xi = jnp.exp(lg * (C - col_i))
                zeta = jnp.exp(lg * col_i)
            dmat = jnp.where(keep, jnp.exp(lg * jnp.maximum(diff, 0.0)), 0.0)
            cdec = jnp.exp(jnp.full((1, RET_DV), lg * C, F32))
            per_head.append((dmat, xi, zeta, cdec))
        tabs.append(per_head)

    refs = ((qkf_ref, vf_ref, cosf_ref, sinf_ref, sf_scr, of_ref),
            (qkb_ref, vb_ref, cosb_ref, sinb_ref, sb_scr, ob_ref))
    k_off = RET_HEADS * RET_DK

    def batch_body(b, carry):
        for d in range(2):
            qk_ref, v_ref, cos_ref, sin_ref, s_scr, o_ref = refs[d]
            for h in range(RET_HEADS):
                dmat, xi, zeta, cdec = tabs[d][h]
                q = qk_ref[0, b, :, h * RET_DK:(h + 1) * RET_DK]
                k = qk_ref[0, b, :, k_off + h * RET_DK:k_off + (h + 1) * RET_DK]
                v = v_ref[0, b, :, h * RET_DV:(h + 1) * RET_DV].astype(BF16)
                if use_rope:
                    q = _rope(q, cos_ref[...], sin_ref[...], lane_low)
                    k = _rope(k, cos_ref[...], sin_ref[...], lane_low)
                k = k * kscale
                s_old = s_scr[b, h]
                scores = lax.dot_general(q.astype(BF16), k.astype(BF16), (((1,), (1,)), ((), ())),
                                         preferred_element_type=F32) * dmat
                o = _dot(scores.astype(BF16), v) + _dot((q * xi).astype(BF16), s_old.astype(BF16))
                kz_t = (k * zeta).T.astype(BF16)
                s_scr[b, h] = cdec * s_old + _dot(kz_t, v)
                o_ref[b, :, h * RET_DV:(h + 1) * RET_DV] = o
        return carry

    lax.fori_loop(0, n_batch, batch_body, 0)

    @pl.when(c == n - 1)
    def _():
        sff_ref[...] = sf_scr[...]
        sfb_ref[...] = sb_scr[...]


def _retention(p4, lg, s0f, s0b, cos_t, sin_t, use_rope):
    _, bsz, seq, d = p4.shape
    n = seq // RET_CHUNK
    C = RET_CHUNK
    v_w = RET_HEADS * RET_DV
    fwd = lambda c: c
    bwd = lambda c: n - 1 - c

    def slab_spec(slab, order):
        return pl.BlockSpec((1, bsz, C, d), lambda c: (slab, 0, order(c), 0))

    tab_spec = lambda order: pl.BlockSpec((C, RET_DK), lambda c: (order(c), 0))
    st_spec = pl.BlockSpec((bsz, RET_HEADS, RET_DK, RET_DV), lambda c: (0, 0, 0, 0))
    st_shape = jax.ShapeDtypeStruct((bsz, RET_HEADS, RET_DK, RET_DV), F32)
    o_shape = jax.ShapeDtypeStruct((bsz, seq, v_w), F32)
    body = functools.partial(_ret_body, use_rope=use_rope, n_batch=bsz)
    return pl.pallas_call(
        body,
        grid=(n,),
        in_specs=[pl.BlockSpec(memory_space=pltpu.SMEM),
                  slab_spec(SLAB_QK, fwd), slab_spec(SLAB_V, fwd),
                  slab_spec(SLAB_QK, bwd), slab_spec(SLAB_V, bwd),
                  tab_spec(fwd), tab_spec(fwd), tab_spec(bwd), tab_spec(bwd),
                  st_spec, st_spec],
        out_specs=[pl.BlockSpec((bsz, C, v_w), lambda c: (0, c, 0)),
                   pl.BlockSpec((bsz, C, v_w), lambda c: (0, n - 1 - c, 0)),
                   st_spec, st_spec],
        out_shape=[o_shape, o_shape, st_shape, st_shape],
        scratch_shapes=[pltpu.VMEM((bsz, RET_HEADS, RET_DK, RET_DV), F32),
                        pltpu.VMEM((bsz, RET_HEADS, RET_DK, RET_DV), F32)],
        compiler_params=pltpu.CompilerParams(
            dimension_semantics=("arbitrary",), vmem_limit_bytes=_vmem_limit(52 << 20)),
        name="retention",
    )(lg, p4, p4, p4, p4, cos_t, sin_t, cos_t, sin_t, s0f, s0b)


def _lru_body(uc_ref, up_ref, un_ref, cw_ref, cb_ref, wa_ref, ba_ref, wx_ref, bx_ref, lam_ref, h0_ref,
              h_ref, fin_ref, ext_scr, a_scr, b_scr, carry_scr, *, reverse, t_blk):
    tb = pl.program_id(1)
    nt = pl.num_programs(1)
    pos = (nt - 1 - tb) if reverse else tb
    T = t_blk

    @pl.when(tb == 0)
    def _():
        carry_scr[...] = h0_ref[0]

    has_prev = (pos > 0).astype(F32)
    has_next = (pos < nt - 1).astype(F32)
    ext_scr[0:8, :] = up_ref[0, 0] * has_prev
    ext_scr[8:8 + T, :] = uc_ref[0, 0]
    ext_scr[8 + T:16 + T, :] = un_ref[0, 0] * has_next
    full = ext_scr[...]
    rows = full.shape[0]
    u = (cb_ref[...]
         + pltpu.roll(full, 2, 0)[8:8 + T] * cw_ref[0:1, :]
         + pltpu.roll(full, 1, 0)[8:8 + T] * cw_ref[1:2, :]
         + full[8:8 + T] * cw_ref[2:3, :]
         + pltpu.roll(full, rows - 1, 0)[8:8 + T] * cw_ref[3:4, :])
    ub = u.astype(BF16)
    rr = []
    xx = []
    for nb in range(LRU_BLOCKS):
        blk = ub[:, nb * LRU_BS:(nb + 1) * LRU_BS]
        rr.append(_dot(blk, wa_ref[nb]))
        xx.append(_dot(blk, wx_ref[nb]))
    r = jax.nn.sigmoid(jnp.concatenate(rr, axis=-1) + ba_ref[...])
    gate_i = jax.nn.sigmoid(jnp.concatenate(xx, axis=-1) + bx_ref[...])
    z = -lam_ref[...]
    softplus = jnp.maximum(z, 0.0) + jnp.log1p(jnp.exp(-jnp.abs(z)))
    log_a = (-LRU_C) * r * softplus
    a = jnp.exp(log_a)
    a_scr[...] = a
    b_scr[...] = jnp.sqrt(1.0 - a * a) * (gate_i * u)

    row = lax.broadcasted_iota(jnp.int32, (SUBLANES, a_scr.shape[1]), 0)
    n_grp = T // SUBLANES

    def step(i, hprev):
        g = (n_grp - 1 - i) if reverse else i
        r0 = pl.multiple_of(g * SUBLANES, SUBLANES)
        a = a_scr[pl.ds(r0, SUBLANES), :]
        b = b_scr[pl.ds(r0, SUBLANES), :]
        for s in (1, 2, 4):
            if reverse:
                valid = row < SUBLANES - s
                shift = SUBLANES - s
            else:
                valid = row >= s
                shift = s
            a_s = jnp.where(valid, pltpu.roll(a, shift, 0), 1.0)
            b_s = jnp.where(valid, pltpu.roll(b, shift, 0), 0.0)
            b = a * b_s + b
            a = a * a_s
        h = a * hprev + b
        h_ref[0, pl.ds(r0, SUBLANES), :] = h
        return h[0:1, :] if reverse else h[SUBLANES - 1:SUBLANES, :]

    hfin = lax.fori_loop(0, n_grp, step, carry_scr[...], unroll=2)
    carry_scr[...] = hfin
    fin_ref[0] = hfin


def _lru_direction(p4, cw, cb, wa_bf, ba, wx_bf, bx, lam, h0, reverse):
    _, bsz, seq, d = p4.shape
    T = min(512, seq)
    nt = seq // T
    sl = SLAB_LRUX
    hb = T // SUBLANES
    n_halo = seq // SUBLANES
    order = (lambda tb: nt - 1 - tb) if reverse else (lambda tb: tb)
    body = functools.partial(_lru_body, reverse=reverse, t_blk=T)
    vec = lambda: pl.BlockSpec((1, d), lambda b, tb: (0, 0))
    return pl.pallas_call(
        body,
        grid=(bsz, nt),
        in_specs=[pl.BlockSpec((1, 1, T, d), lambda b, tb: (sl, b, order(tb), 0)),
                  pl.BlockSpec((1, 1, SUBLANES, d), lambda b, tb: (sl, b, jnp.maximum(order(tb) * hb - 1, 0), 0)),
                  pl.BlockSpec((1, 1, SUBLANES, d),
                               lambda b, tb: (sl, b, jnp.minimum((order(tb) + 1) * hb, n_halo - 1), 0)),
                  pl.BlockSpec((CONV_W, d), lambda b, tb: (0, 0)),
                  vec(),
                  pl.BlockSpec((LRU_BLOCKS, LRU_BS, LRU_BS), lambda b, tb: (0, 0, 0)),
                  vec(),
                  pl.BlockSpec((LRU_BLOCKS, LRU_BS, LRU_BS), lambda b, tb: (0, 0, 0)),
                  vec(), vec(),
                  pl.BlockSpec((1, 1, d), lambda b, tb: (b, 0, 0))],
        out_specs=[pl.BlockSpec((1, T, d), lambda b, tb: (b, order(tb), 0)),
                   pl.BlockSpec((1, 1, d), lambda b, tb: (b, 0, 0))],
        out_shape=[jax.ShapeDtypeStruct((bsz, seq, d), F32), jax.ShapeDtypeStruct((bsz, 1, d), F32)],
        scratch_shapes=[pltpu.VMEM((T + 2 * SUBLANES, d), F32), pltpu.VMEM((T, d), F32),
                        pltpu.VMEM((T, d), F32), pltpu.VMEM((1, d), F32)],
        compiler_params=pltpu.CompilerParams(
            dimension_semantics=("parallel", "arbitrary"), vmem_limit_bytes=_vmem_limit(40 << 20)),
        name="rglru_rev" if reverse else "rglru_fwd",
    )(p4, p4, p4, cw, cb, wa_bf, ba, wx_bf, bx, lam, h0)


def _merge_body(x_ref, of_ref, ob_ref, hf_ref, hb_ref, rg_ref, lg_ref, bgr_ref, bgl_ref, g1_ref,
                wr_ref, wl_ref, wo_ref, o_ref):
    o = of_ref[...] + ob_ref[...]
    parts = []
    for h in range(RET_HEADS):
        oh = o[:, h * RET_DV:(h + 1) * RET_DV]
        mu = jnp.mean(oh, axis=-1, keepdims=True)
        cen = oh - mu
        var = jnp.mean(cen * cen, axis=-1, keepdims=True)
        parts.append(cen * lax.rsqrt(var + EPS))
    on = jnp.concatenate(parts, axis=-1)
    rg = rg_ref[0]
    ret = _dot((on * (rg * jax.nn.sigmoid(rg))).astype(BF16), wr_ref[...])
    lru = _dot(((hf_ref[...] + hb_ref[...]) * jax.nn.gelu(lg_ref[0])).astype(BF16), wl_ref[...])
    y = jax.nn.sigmoid(bgr_ref[0]) * ret + jax.nn.sigmoid(bgl_ref[0]) * lru
    o_ref[...] = x_ref[...] + g1_ref[0] * _dot(y.astype(BF16), wo_ref[...])


def _merge(x2, o_f, o_b, h_f, h_b, p2, g1, wr_bf, wl_bf, wo_bf, seq_len):
    t, d = x2.shape
    tm = 512
    per_batch = seq_len // tm
    tok = lambda: pl.BlockSpec((tm, d), lambda i: (i, 0))
    pcol = lambda slab: pl.BlockSpec((1, tm, d), lambda i: (slab, i, 0))
    wspec = lambda: pl.BlockSpec((d, d), lambda i: (0, 0))
    return pl.pallas_call(
        _merge_body,
        grid=(t // tm,),
        in_specs=[tok(), tok(), tok(), tok(), tok(),
                  pcol(SLAB_RGATE), pcol(SLAB_LRUG), pcol(SLAB_BGR), pcol(SLAB_BGL),
                  pl.BlockSpec((1, 1, d), lambda i: (i // per_batch, 0, 0)),
                  wspec(), wspec(), wspec()],
        out_specs=tok(),
        out_shape=jax.ShapeDtypeStruct((t, d), F32),
        compiler_params=pltpu.CompilerParams(
            dimension_semantics=("parallel",), vmem_limit_bytes=_vmem_limit(52 << 20)),
        name="merge",
    )(x2, o_f, o_b, h_f, h_b, p2, p2, p2, p2, g1, wr_bf, wl_bf, wo_bf)


PEER_TM = 512
PEER_EBLK = 1024
PEER_SLABS = PEER_EBLK // PEER_NKEYS
PEER_RB = 256


_GELU_K1 = 2.0 * math.sqrt(2.0 / math.pi) / math.log(2.0)
_GELU_K2 = _GELU_K1 * 0.044715


def _gelu_tanh(x):
    return x / (1.0 + jnp.exp2(x * (-_GELU_K1 - _GELU_K2 * (x * x))))


def _max_first(v, order_groups):
    groups = [(v[k * SUBLANES:(k + 1) * SUBLANES], o) for k, o in enumerate(order_groups)]
    while len(groups) > 1:
        nxt = []
        for a in range(0, len(groups) - 1, 2):
            (va, oa), (vb, ob) = groups[a], groups[a + 1]
            later = vb > va
            nxt.append((jnp.where(later, vb, va), jnp.where(later, ob, oa)))
        if len(groups) % 2:
            nxt.append(groups[-1])
        groups = nxt
    v8, o8 = groups[0]
    m = jnp.max(v8, axis=0, keepdims=True)
    first = jnp.min(jnp.where(v8 == m, o8, 1e9), axis=0, keepdims=True)
    return m, first


def _extract_top(vals, order_groups, n_take, on_take):
    order = jnp.concatenate(order_groups, axis=0)

    def body(a, state):
        v, carry = state
        m, first = _max_first(v, order_groups)
        hit = order == first
        carry = on_take(jnp.asarray(a, F32), m, hit, carry)
        return jnp.where(hit, NEG_INF, v), carry
    return body, n_take


def _topk_keys(scores):
    nk, n = scores[0].shape
    order = lax.broadcasted_iota(jnp.int32, (nk, n), 0).astype(F32)
    row8 = lax.broadcasted_iota(jnp.int32, (SUBLANES, n), 0).astype(F32)
    order_groups = [row8 + float(k * SUBLANES) for k in range(nk // SUBLANES)]
    row16 = lax.broadcasted_iota(jnp.int32, (PEER_TOPK, n), 0).astype(F32)

    def body(a, state):
        take = row16 == jnp.asarray(a, F32)
        out = []
        for v, top, idx in state:
            m, first = _max_first(v, order_groups)
            out.append((jnp.where(order == first, NEG_INF, v), jnp.where(take, m, top),
                        jnp.where(take, first, idx)))
        return tuple(out)

    zeros = jnp.zeros((PEER_TOPK, n), F32)
    res = lax.fori_loop(0, PEER_TOPK, body, tuple((s, zeros, zeros) for s in scores))
    return [(idx, top) for _, top, idx in res]


def _by_rank(idx, values, fill, shape):
    order = lax.broadcasted_iota(jnp.int32, shape, 0).astype(F32)
    out = jnp.full(shape, fill, F32)
    for a in range(PEER_TOPK):
        out = jnp.where(order == idx[a:a + 1], values[a], out)
    return out


def _route_head(s1, s2):
    n = s1.shape[1]
    (i1, t1), = _topk_keys([s1])
    (i2, t2), = _topk_keys([s2])
    row8 = lax.broadcasted_iota(jnp.int32, (SUBLANES, n), 0).astype(F32)
    row16 = lax.broadcasted_iota(jnp.int32, (PEER_TOPK, n), 0).astype(F32)
    cands = [t2 + t1[0:1]]
    flats = [row8, row8 + float(SUBLANES)]
    for a in range(1, 8):
        n_a = PEER_TOPK // (a + 1)
        cands.append(jnp.where(row8 < n_a, t2[0:8] + t1[a:a + 1], NEG_INF))
        flats.append(row8 + float(a * PEER_TOPK))
    cands.append(t1[8:16] + t2[0:1])
    flats.append((row8 + 8.0) * float(PEER_TOPK))
    cand = jnp.concatenate(cands, axis=0)

    def on_take(a, m, hit, sel):
        return jnp.where(hit, 1.0, sel)

    body, n_take = _extract_top(cand, flats, PEER_TOPK, on_take)
    _, sel = lax.fori_loop(0, n_take, body, (cand, jnp.zeros_like(cand)))

    cmax = cand[0:1]
    z = jnp.sum(sel * jnp.exp(cand - cmax), axis=0, keepdims=True)
    l_top = jnp.where(row8 == 0.0, jnp.sum(sel[0:16], axis=0, keepdims=True), 0.0)
    for a in range(1, 8):
        l_a = jnp.sum(sel[8 + 8 * a:16 + 8 * a], axis=0, keepdims=True)
        l_top = jnp.where(row8 == float(a), l_a, l_top)
    l_mat = jnp.concatenate([l_top, sel[72:80]], axis=0)
    l1 = _by_rank(i1, [l_mat[a:a + 1] for a in range(PEER_TOPK)], 0.0, s1.shape)
    r2 = _by_rank(i2, [float(a) for a in range(PEER_TOPK)], float(PEER_TOPK), s2.shape)
    c1 = jnp.exp(s1 - t1[0:1]) * (1.0 / z)
    e2 = jnp.exp(s2 - t2[0:1])
    return r2, e2, l1, c1


def _peer_body(x_ref, g_ref, sh_ref, sc_ref, g2_ref, fg_ref, wqh_ref, wql_ref, kh_ref, kl_ref, u_ref, vt_ref,
               o_ref, h2t_scr, s_scr, r2_scr, e2_scr, l1_scr, c1_scr, a_scr, wf_scr, acc_scr):
    j = pl.program_id(1)
    nj = pl.num_programs(1)
    tm = x_ref.shape[0]

    @pl.when(j == 0)
    def _():
        h2 = _rms_mod(x_ref[...], g_ref[...], sh_ref[0], sc_ref[0])
        h2t = h2.T
        h_hi, h_lo = _split_bf16(h2t)
        h2t_scr[...] = h_hi
        qt = _dot3(wqh_ref[...], wql_ref[...], h_hi, h_lo)
        q_hi, q_lo = _split_bf16(qt)
        for hp in range(2 * PEER_HEADS):
            sl = slice(hp * PEER_DHALF, (hp + 1) * PEER_DHALF)
            s_scr[hp] = _dot3(kh_ref[hp], kl_ref[hp], q_hi[sl], q_lo[sl])
        acc_scr[...] = jnp.zeros_like(acc_scr)

        def head_body(h, carry):
            for pb in range(tm // PEER_RB):
                ps = slice(pb * PEER_RB, (pb + 1) * PEER_RB)
                r2, e2, l1, c1 = _route_head(s_scr[2 * h, :, ps], s_scr[2 * h + 1, :, ps])
                for q in range(PEER_RB // LANES):
                    qs = slice(q * LANES, (q + 1) * LANES)
                    r2_scr[h, pb * (PEER_RB // LANES) + q] = r2[:, qs]
                    e2_scr[h, pb * (PEER_RB // LANES) + q] = e2[:, qs]
                l1_scr[h, :, ps] = l1
                c1_scr[h, :, ps] = c1
            return carry

        lax.fori_loop(0, PEER_HEADS, head_body, 0)

    n_lb = tm // LANES

    res = _dot(u_ref[...], h2t_scr[...])
    for q in range(n_lb):
        a_scr[q] = res[:, q * LANES:(q + 1) * LANES]

    def slab_body(s, carry):
        i1 = j * PEER_SLABS + s
        r0 = pl.multiple_of(s * PEER_NKEYS, PEER_NKEYS)
        lrows = [l1_scr[h, pl.ds(i1, 1), :] for h in range(PEER_HEADS)]
        crows = [c1_scr[h, pl.ds(i1, 1), :] for h in range(PEER_HEADS)]
        for lb in range(n_lb):
            ls = slice(lb * LANES, (lb + 1) * LANES)
            w = None
            for h in range(PEER_HEADS):
                term = jnp.where(r2_scr[h, lb] < lrows[h][:, ls], e2_scr[h, lb], 0.0) * crows[h][:, ls]
                w = term if w is None else w + term
            act = _gelu_tanh(a_scr[lb, pl.ds(r0, PEER_NKEYS), :])
            wf_scr[lb, pl.ds(r0, PEER_NKEYS), :] = (w * act).astype(BF16)
        return carry

    lax.fori_loop(0, PEER_SLABS, slab_body, 0)
    rhs = jnp.concatenate([wf_scr[q] for q in range(n_lb)], axis=1)
    acc_scr[...] += _dot(vt_ref[0], rhs)

    @pl.when(j == nj - 1)
    def _():
        x2 = x_ref[...] + g2_ref[0] * acc_scr[...].T
        ms = jnp.mean(x2 * x2, axis=-1, keepdims=True)
        o_ref[...] = x2 * lax.rsqrt(ms + EPS) * fg_ref[...]


def _peer(x2, g, shift, scale, gate, final_g, wq_hi, wq_lo, k_hi, k_lo, u_bf, vt_bf, seq_len):
    t, d = x2.shape
    n_exp = u_bf.shape[0]
    tm = PEER_TM
    per_batch = seq_len // tm
    mod = lambda: pl.BlockSpec((1, 1, d), lambda i, j: (i // per_batch, 0, 0))
    vec = lambda: pl.BlockSpec((1, d), lambda i, j: (0, 0))
    nq = wq_hi.shape[0]
    tab = lambda: pltpu.VMEM((PEER_HEADS, PEER_NKEYS, tm), F32)
    nblk = n_exp // PEER_EBLK
    return pl.pallas_call(
        _peer_body,
        grid=(t // tm, nblk),
        in_specs=[pl.BlockSpec((tm, d), lambda i, j: (i, 0)),
                  vec(), mod(), mod(), mod(), vec(),
                  pl.BlockSpec((nq, d), lambda i, j: (0, 0)),
                  pl.BlockSpec((nq, d), lambda i, j: (0, 0)),
                  pl.BlockSpec((2 * PEER_HEADS, PEER_NKEYS, PEER_DHALF), lambda i, j: (0, 0, 0)),
                  pl.BlockSpec((2 * PEER_HEADS, PEER_NKEYS, PEER_DHALF), lambda i, j: (0, 0, 0)),
                  pl.BlockSpec((PEER_EBLK, d), lambda i, j: (j, 0)),
                  pl.BlockSpec((1, d, PEER_EBLK), lambda i, j: (j, 0, 0))],
        out_specs=pl.BlockSpec((tm, d), lambda i, j: (i, 0)),
        out_shape=jax.ShapeDtypeStruct((t, d), F32),
        scratch_shapes=[pltpu.VMEM((d, tm), BF16),
                        pltpu.VMEM((2 * PEER_HEADS, PEER_NKEYS, tm), F32),
                        pltpu.VMEM((PEER_HEADS, tm // LANES, PEER_NKEYS, LANES), F32),
                        pltpu.VMEM((PEER_HEADS, tm // LANES, PEER_NKEYS, LANES), F32),
                        tab(), tab(),
                        pltpu.VMEM((tm // LANES, PEER_EBLK, LANES), F32),
                        pltpu.VMEM((tm // LANES, PEER_EBLK, LANES), BF16),
                        pltpu.VMEM((d, tm), F32)],
        compiler_params=pltpu.CompilerParams(
            dimension_semantics=("parallel", "arbitrary"), vmem_limit_bytes=_vmem_limit(56 << 20)),
        name="peer",
    )(x2, g, shift, scale, gate, final_g, wq_hi, wq_lo, k_hi, k_lo, u_bf, vt_bf)


def kernel(x, c, ctx, c_ctx, mod_w, mod_b, norm1_g, norm2_g, w_in, ret_decay, conv_w, conv_b, lru_wa, lru_ba,
           lru_wx, lru_bx, lru_lambda, w_ret_out, w_lru_out, w_out, peer_wq, peer_keys, peer_u, peer_v, final_g):
    bsz, seq, d = x.shape
    ctx_len = ctx.shape[1]
    l = 0

    pad = (-(bsz + 1)) % SUBLANES
    c_all = jnp.concatenate([c, c_ctx[None, :], jnp.zeros((pad, d), F32)], axis=0)
    mod = _modulation(c_all, mod_w[l], mod_b[l][None, :])
    mod_x = mod[:bsz].reshape(bsz, 1, 6, d)
    sh1x, sc1x, g1x, sh2x, sc2x, g2x = [mod_x[:, :, i, :] for i in range(6)]
    mod_c = mod[bsz:bsz + 1].reshape(1, 1, 6, d)
    sh1c, sc1c = mod_c[:, :, 0, :], mod_c[:, :, 1, :]

    w_in_bf = w_in[l].astype(BF16)
    g1 = norm1_g[l][None, :]
    x2 = x.reshape(bsz * seq, d)
    ctx2 = ctx.reshape(bsz * ctx_len, d)
    px = _in_projection(x2, g1, sh1x, sc1x, w_in_bf, seq)
    pc = _in_projection(ctx2, g1, sh1c, sc1c, w_in_bf, ctx_len)
    px3 = px.reshape(px.shape[0], bsz, seq, d)
    pc3 = pc.reshape(pc.shape[0], bsz, ctx_len, d)

    lg = jax.nn.log_sigmoid(ret_decay[l].astype(F32))
    cos_t, sin_t = _rope_tables(seq)
    zeros_s = jnp.zeros((bsz, RET_HEADS, RET_DK, RET_DV), F32)
    _, _, sf, sb = _retention(pc3, lg, zeros_s, zeros_s, cos_t[:ctx_len], sin_t[:ctx_len], use_rope=False)
    o_f, o_b, _, _ = _retention(px3, lg, sf, sb, cos_t, sin_t, use_rope=True)

    cw = conv_w[l]
    cb = conv_b[l][None, :]
    h_dirs = []
    for dr in range(2):
        wa_bf = lru_wa[l, dr].astype(BF16)
        wx_bf = lru_wx[l, dr].astype(BF16)
        ba = lru_ba[l, dr][None, :]
        bx = lru_bx[l, dr][None, :]
        lam = lru_lambda[l, dr][None, :]
        h0 = jnp.zeros((bsz, 1, cw.shape[1]), F32)
        _, fin = _lru_direction(pc3, cw, cb, wa_bf, ba, wx_bf, bx, lam, h0, reverse=bool(dr))
        hx, _ = _lru_direction(px3, cw, cb, wa_bf, ba, wx_bf, bx, lam, fin, reverse=bool(dr))
        h_dirs.append(hx.reshape(bsz * seq, -1))

    x1 = _merge(x2, o_f.reshape(bsz * seq, -1), o_b.reshape(bsz * seq, -1), h_dirs[0], h_dirs[1], px, g1x,
                w_ret_out[l].astype(BF16), w_lru_out[l].astype(BF16), w_out[l].astype(BF16), seq)

    wq_t = peer_wq[l].T
    wq_hi = wq_t.astype(BF16)
    wq_lo = (wq_t - wq_hi.astype(F32)).astype(BF16)
    keys = peer_keys[l].reshape(2 * PEER_HEADS, PEER_NKEYS, PEER_DHALF)
    k_hi = keys.astype(BF16)
    k_lo = (keys - k_hi.astype(F32)).astype(BF16)
    u_bf = peer_u[l].astype(BF16)
    vt_bf = peer_v[l].astype(BF16).reshape(-1, PEER_EBLK, d).transpose(0, 2, 1)
    out = _peer(x1, norm2_g[l][None, :], sh2x, sc2x, g2x, final_g[None, :], wq_hi, wq_lo, k_hi, k_lo,
                u_bf, vt_bf, seq)
    return out.reshape(bsz, seq, d)
```

```python
import functools
import math

import jax
import jax.numpy as jnp
import numpy as np
from jax import lax
from jax.experimental import pallas as pl
from jax.experimental.pallas import tpu as pltpu

F32 = jnp.float32
BF16 = jnp.bfloat16

EPS = 1e-6
GRID_W = 64
ROPE_BASE = 10000.0
RET_HEADS = 4
RET_DK = 128
RET_DV = 256
RET_CHUNK = 128
LRU_BLOCKS = 8
LRU_BS = 128
LRU_C = 8.0
CONV_W = 4
PEER_HEADS = 8
PEER_DHALF = 64
PEER_NKEYS = 128
PEER_TOPK = 16
SLAB_QK, SLAB_V, SLAB_RGATE, SLAB_LRUX, SLAB_LRUG, SLAB_BGR, SLAB_BGL = range(7)

V7X_VMEM_BYTES = 64 * 1024 * 1024
LANES = 128
SUBLANES = 8

NEG_INF = float("-inf")


def _vmem_limit(nbytes):
    return int(min(max(nbytes, 16 * 1024 * 1024), V7X_VMEM_BYTES - 6 * 1024 * 1024))


def _dot(a, b):
    return jnp.dot(a, b, preferred_element_type=F32)


def _split_bf16(a):
    hi = a.astype(BF16)
    lo = (a - hi.astype(F32)).astype(BF16)
    return hi, lo


def _dot3(a_hi, a_lo, b_hi, b_lo):
    return _dot(a_hi, b_hi) + (_dot(a_hi, b_lo) + _dot(a_lo, b_hi))


def _mod_body(c_ref, w_ref, b_ref, o_ref):
    c = c_ref[...]
    sc = c * jax.nn.sigmoid(c)
    c_hi, c_lo = _split_bf16(sc)
    w_hi, w_lo = _split_bf16(w_ref[...])
    o_ref[...] = _dot3(c_hi, c_lo, w_hi, w_lo) + b_ref[...]


def _modulation(c_all, mod_w, mod_b):
    rows, d = c_all.shape
    n = mod_w.shape[1]
    tn = 1536
    return pl.pallas_call(
        _mod_body,
        grid=(n // tn,),
        in_specs=[pl.BlockSpec((rows, d), lambda j: (0, 0)),
                  pl.BlockSpec((d, tn), lambda j: (0, j)),
                  pl.BlockSpec((1, tn), lambda j: (0, j))],
        out_specs=pl.BlockSpec((rows, tn), lambda j: (0, j)),
        out_shape=jax.ShapeDtypeStruct((rows, n), F32),
        compiler_params=pltpu.CompilerParams(
            dimension_semantics=("arbitrary",), vmem_limit_bytes=_vmem_limit(40 << 20)),
        name="modulation",
    )(c_all, mod_w, mod_b)


INPROJ_TM = 256

def _rms_mod(x, g, shift, scale):
    ms = jnp.mean(x * x, axis=-1, keepdims=True)
    h = x * lax.rsqrt(ms + EPS) * g
    return h * (1.0 + scale) + shift


def _inproj_body(x_ref, g_ref, sh_ref, sc_ref, w_ref, *rest, use_rope):
    if use_rope:
        cos_ref, sin_ref, o_ref = rest
    else:
        o_ref, = rest
    h = _rms_mod(x_ref[...], g_ref[...], sh_ref[0], sc_ref[0]).astype(BF16)
    d = h.shape[1]
    for k in range(o_ref.shape[0]):
        res = _dot(h, w_ref[:, k * d:(k + 1) * d])
        if k == SLAB_QK:
            lane = lax.broadcasted_iota(jnp.int32, (res.shape[0], RET_DK), 1)
            lane_low = (lane % 64) < 32
            parts = []
            for gi in range(d // RET_DK):
                part = res[:, gi * RET_DK:(gi + 1) * RET_DK]
                if use_rope:
                    part = _rope(part, cos_ref[...], sin_ref[...], lane_low)
                if gi >= RET_HEADS:
                    part = part * (RET_DK ** -0.5)
                parts.append(part)
            res = jnp.concatenate(parts, axis=-1)
        o_ref[k] = res


def _in_projection(x2, g, shift, scale, w_bf, seq_len, rope=None):
    t, d = x2.shape
    n = w_bf.shape[1]
    tm = min(INPROJ_TM, seq_len)
    per_batch = seq_len // tm
    if shift.shape[0] == 1:
        mod_map = lambda i: (0, 0, 0)
    else:
        mod_map = lambda i: (i // per_batch, 0, 0)
    vmem = 2 * (d * n * 2 + (n // d) * tm * d * 4 + tm * d * 4) + (4 << 20)
    in_specs = [pl.BlockSpec((tm, d), lambda i: (i, 0)),
                pl.BlockSpec((1, d), lambda i: (0, 0)),
                pl.BlockSpec((1, 1, d), mod_map),
                pl.BlockSpec((1, 1, d), mod_map),
                pl.BlockSpec((d, n), lambda i: (0, 0))]
    args = [x2, g, shift, scale, w_bf]
    if rope is not None:
        in_specs += [pl.BlockSpec((tm, RET_DK), lambda i: (i % per_batch, 0))] * 2
        args += list(rope)
    return pl.pallas_call(
        functools.partial(_inproj_body, use_rope=rope is not None),
        grid=(t // tm,),
        in_specs=in_specs,
        out_specs=pl.BlockSpec((n // d, tm, d), lambda i: (0, i, 0)),
        out_shape=jax.ShapeDtypeStruct((n // d, t, d), F32),
        compiler_params=pltpu.CompilerParams(
            dimension_semantics=("parallel",), vmem_limit_bytes=_vmem_limit(vmem)),
        name="in_projection",
    )(*args)


def _rope_tables(n_tok):
    quarter = RET_DK // 4
    pos = jnp.arange(n_tok)
    row = (pos // GRID_W).astype(F32)
    col = (pos % GRID_W).astype(F32)
    inv = ROPE_BASE ** (-jnp.arange(quarter, dtype=F32) / quarter)
    ar = row[:, None] * inv[None, :]
    ac = col[:, None] * inv[None, :]
    cos_t = jnp.concatenate([jnp.cos(ar), jnp.cos(ar), jnp.cos(ac), jnp.cos(ac)], axis=-1)
    sin_t = jnp.concatenate([-jnp.sin(ar), jnp.sin(ar), -jnp.sin(ac), jnp.sin(ac)], axis=-1)
    return cos_t, sin_t


def _rope(t, cos_t, sin_t, lane_low):
    swapped = jnp.where(lane_low, pltpu.roll(t, 96, 1), pltpu.roll(t, 32, 1))
    return t * cos_t + swapped * sin_t


def _ret_body(lg_ref, qkf_ref, vf_ref, qkb_ref, vb_ref, s0f_ref, s0b_ref,
              of_ref, ob_ref, sff_ref, sfb_ref, sf_scr, sb_scr, dmat_scr, xi_scr, zeta_scr, cdec_scr, *, n_batch):
    c = pl.program_id(0)
    n = pl.num_programs(0)
    C = RET_CHUNK

    @pl.when(c == 0)
    def _():
        sf_scr[...] = s0f_ref[...]
        sb_scr[...] = s0b_ref[...]
        ii = lax.broadcasted_iota(jnp.int32, (C, C), 0)
        jj = lax.broadcasted_iota(jnp.int32, (C, C), 1)
        col_i = ii.astype(F32)
        for d in range(2):
            for h in range(RET_HEADS):
                lg = lg_ref[d, h]
                if d == 0:
                    diff = (ii - jj).astype(F32)
                    keep = ii >= jj
                    xi = jnp.exp(lg * (col_i + 1.0))
                    zeta = jnp.exp(lg * (C - 1.0 - col_i))
                else:
                    diff = (jj - ii).astype(F32)
                    keep = jj > ii
                    xi = jnp.exp(lg * (C - col_i))
                    zeta = jnp.exp(lg * col_i)
                dmat_scr[d, h] = jnp.where(keep, jnp.exp(lg * jnp.maximum(diff, 0.0)), 0.0)
                xi_scr[d, h] = xi
                zeta_scr[d, h] = zeta
                cdec_scr[d, h] = jnp.exp(jnp.full((SUBLANES, RET_DV), lg * C, F32))

    refs = ((qkf_ref, vf_ref, sf_scr, of_ref), (qkb_ref, vb_ref, sb_scr, ob_ref))
    k_off = RET_HEADS * RET_DK

    def batch_body(b, carry):
        for d in range(2):
            qk_ref, v_ref, s_scr, o_ref = refs[d]
            for h in range(RET_HEADS):
                dmat, xi, zeta = dmat_scr[d, h], xi_scr[d, h], zeta_scr[d, h]
                cdec = cdec_scr[d, h, 0:1, :]
                q = qk_ref[0, b, :, h * RET_DK:(h + 1) * RET_DK]
                k = qk_ref[0, b, :, k_off + h * RET_DK:k_off + (h + 1) * RET_DK]
                v = v_ref[0, b, :, h * RET_DV:(h + 1) * RET_DV].astype(BF16)
                s_old = s_scr[b, h]
                scores = lax.dot_general(q.astype(BF16), k.astype(BF16), (((1,), (1,)), ((), ())),
                                         preferred_element_type=F32) * dmat
                o = _dot(scores.astype(BF16), v) + _dot((q * xi).astype(BF16), s_old.astype(BF16))
                kz_t = (k * zeta).T.astype(BF16)
                s_scr[b, h] = cdec * s_old + _dot(kz_t, v)
                o_ref[b, :, h * RET_DV:(h + 1) * RET_DV] = o
        return carry

    lax.fori_loop(0, n_batch, batch_body, 0)

    @pl.when(c == n - 1)
    def _():
        sff_ref[...] = sf_scr[...]
        sfb_ref[...] = sb_scr[...]


def _retention(p4, lg, s0f, s0b):
    _, bsz, seq, d = p4.shape
    n = seq // RET_CHUNK
    C = RET_CHUNK
    v_w = RET_HEADS * RET_DV
    fwd = lambda c: c
    bwd = lambda c: n - 1 - c

    def slab_spec(slab, order):
        return pl.BlockSpec((1, bsz, C, d), lambda c: (slab, 0, order(c), 0))

    st_spec = pl.BlockSpec((bsz, RET_HEADS, RET_DK, RET_DV), lambda c: (0, 0, 0, 0))
    st_shape = jax.ShapeDtypeStruct((bsz, RET_HEADS, RET_DK, RET_DV), F32)
    o_shape = jax.ShapeDtypeStruct((bsz, seq, v_w), F32)
    body = functools.partial(_ret_body, n_batch=bsz)
    return pl.pallas_call(
        body,
        grid=(n,),
        in_specs=[pl.BlockSpec(memory_space=pltpu.SMEM),
                  slab_spec(SLAB_QK, fwd), slab_spec(SLAB_V, fwd),
                  slab_spec(SLAB_QK, bwd), slab_spec(SLAB_V, bwd),
                  st_spec, st_spec],
        out_specs=[pl.BlockSpec((bsz, C, v_w), lambda c: (0, c, 0)),
                   pl.BlockSpec((bsz, C, v_w), lambda c: (0, n - 1 - c, 0)),
                   st_spec, st_spec],
        out_shape=[o_shape, o_shape, st_shape, st_shape],
        scratch_shapes=[pltpu.VMEM((bsz, RET_HEADS, RET_DK, RET_DV), F32),
                        pltpu.VMEM((bsz, RET_HEADS, RET_DK, RET_DV), F32),
                        pltpu.VMEM((2, RET_HEADS, C, C), F32),
                        pltpu.VMEM((2, RET_HEADS, C, RET_DK), F32),
                        pltpu.VMEM((2, RET_HEADS, C, RET_DK), F32),
                        pltpu.VMEM((2, RET_HEADS, SUBLANES, RET_DV), F32)],
        compiler_params=pltpu.CompilerParams(
            dimension_semantics=("arbitrary",), vmem_limit_bytes=_vmem_limit(52 << 20)),
        name="retention",
    )(lg, p4, p4, p4, p4, s0f, s0b)


def _lru_body(uc_ref, up_ref, un_ref, cw_ref, cb_ref, wa_ref, ba_ref, wx_ref, bx_ref, lam_ref, h0_ref,
              h_ref, fin_ref, ext_scr, a_scr, b_scr, carry_scr, *, reverse, t_blk):
    tb = pl.program_id(1)
    nt = pl.num_programs(1)
    pos = (nt - 1 - tb) if reverse else tb
    T = t_blk

    @pl.when(tb == 0)
    def _():
        carry_scr[...] = h0_ref[0]

    has_prev = (pos > 0).astype(F32)
    has_next = (pos < nt - 1).astype(F32)
    ext_scr[0:8, :] = up_ref[0, 0] * has_prev
    ext_scr[8:8 + T, :] = uc_ref[0, 0]
    ext_scr[8 + T:16 + T, :] = un_ref[0, 0] * has_next
    full = ext_scr[...]
    rows = full.shape[0]
    u = (cb_ref[...]
         + pltpu.roll(full, 2, 0)[8:8 + T] * cw_ref[0:1, :]
         + pltpu.roll(full, 1, 0)[8:8 + T] * cw_ref[1:2, :]
         + full[8:8 + T] * cw_ref[2:3, :]
         + pltpu.roll(full, rows - 1, 0)[8:8 + T] * cw_ref[3:4, :])
    ub = u.astype(BF16)
    rr = []
    xx = []
    for nb in range(LRU_BLOCKS):
        blk = ub[:, nb * LRU_BS:(nb + 1) * LRU_BS]
        rr.append(_dot(blk, wa_ref[nb]))
        xx.append(_dot(blk, wx_ref[nb]))
    r = jax.nn.sigmoid(jnp.concatenate(rr, axis=-1) + ba_ref[...])
    gate_i = jax.nn.sigmoid(jnp.concatenate(xx, axis=-1) + bx_ref[...])
    z = -lam_ref[...]
    softplus = jnp.maximum(z, 0.0) + jnp.log1p(jnp.exp(-jnp.abs(z)))
    log_a = (-LRU_C) * r * softplus
    a = jnp.exp(log_a)
    a_scr[...] = a
    b_scr[...] = jnp.sqrt(1.0 - a * a) * (gate_i * u)

    row = lax.broadcasted_iota(jnp.int32, (SUBLANES, a_scr.shape[1]), 0)
    n_grp = T // SUBLANES

    def step(i, hprev):
        g = (n_grp - 1 - i) if reverse else i
        r0 = pl.multiple_of(g * SUBLANES, SUBLANES)
        a = a_scr[pl.ds(r0, SUBLANES), :]
        b = b_scr[pl.ds(r0, SUBLANES), :]
        for s in (1, 2, 4):
            if reverse:
                valid = row < SUBLANES - s
                shift = SUBLANES - s
            else:
                valid = row >= s
                shift = s
            a_s = jnp.where(valid, pltpu.roll(a, shift, 0), 1.0)
            b_s = jnp.where(valid, pltpu.roll(b, shift, 0), 0.0)
            b = a * b_s + b
            a = a * a_s
        h = a * hprev + b
        h_ref[0, pl.ds(r0, SUBLANES), :] = h
        return h[0:1, :] if reverse else h[SUBLANES - 1:SUBLANES, :]

    hfin = lax.fori_loop(0, n_grp, step, carry_scr[...], unroll=2)
    carry_scr[...] = hfin
    fin_ref[0] = hfin


def _lru_direction(p4, cw, cb, wa_bf, ba, wx_bf, bx, lam, h0, reverse):
    _, bsz, seq, d = p4.shape
    T = min(512, seq)
    nt = seq // T
    sl = SLAB_LRUX
    hb = T // SUBLANES
    n_halo = seq // SUBLANES
    order = (lambda tb: nt - 1 - tb) if reverse else (lambda tb: tb)
    body = functools.partial(_lru_body, reverse=reverse, t_blk=T)
    vec = lambda: pl.BlockSpec((1, d), lambda b, tb: (0, 0))
    return pl.pallas_call(
        body,
        grid=(bsz, nt),
        in_specs=[pl.BlockSpec((1, 1, T, d), lambda b, tb: (sl, b, order(tb), 0)),
                  pl.BlockSpec((1, 1, SUBLANES, d), lambda b, tb: (sl, b, jnp.maximum(order(tb) * hb - 1, 0), 0)),
                  pl.BlockSpec((1, 1, SUBLANES, d),
                               lambda b, tb: (sl, b, jnp.minimum((order(tb) + 1) * hb, n_halo - 1), 0)),
                  pl.BlockSpec((CONV_W, d), lambda b, tb: (0, 0)),
                  vec(),
                  pl.BlockSpec((LRU_BLOCKS, LRU_BS, LRU_BS), lambda b, tb: (0, 0, 0)),
                  vec(),
                  pl.BlockSpec((LRU_BLOCKS, LRU_BS, LRU_BS), lambda b, tb: (0, 0, 0)),
                  vec(), vec(),
                  pl.BlockSpec((1, 1, d), lambda b, tb: (b, 0, 0))],
        out_specs=[pl.BlockSpec((1, T, d), lambda b, tb: (b, order(tb), 0)),
                   pl.BlockSpec((1, 1, d), lambda b, tb: (b, 0, 0))],
        out_shape=[jax.ShapeDtypeStruct((bsz, seq, d), F32), jax.ShapeDtypeStruct((bsz, 1, d), F32)],
        scratch_shapes=[pltpu.VMEM((T + 2 * SUBLANES, d), F32), pltpu.VMEM((T, d), F32),
                        pltpu.VMEM((T, d), F32), pltpu.VMEM((1, d), F32)],
        compiler_params=pltpu.CompilerParams(
            dimension_semantics=("parallel", "arbitrary"), vmem_limit_bytes=_vmem_limit(40 << 20)),
        name="rglru_rev" if reverse else "rglru_fwd",
    )(p4, p4, p4, cw, cb, wa_bf, ba, wx_bf, bx, lam, h0)


def _merge_body(x_ref, of_ref, ob_ref, hf_ref, hb_ref, rg_ref, lg_ref, bgr_ref, bgl_ref, g1_ref,
                wr_ref, wl_ref, wo_ref, o_ref):
    o = of_ref[...] + ob_ref[...]
    parts = []
    for h in range(RET_HEADS):
        oh = o[:, h * RET_DV:(h + 1) * RET_DV]
        mu = jnp.mean(oh, axis=-1, keepdims=True)
        cen = oh - mu
        var = jnp.mean(cen * cen, axis=-1, keepdims=True)
        parts.append(cen * lax.rsqrt(var + EPS))
    on = jnp.concatenate(parts, axis=-1)
    rg = rg_ref[0]
    ret = _dot((on * (rg * jax.nn.sigmoid(rg))).astype(BF16), wr_ref[...])
    lru = _dot(((hf_ref[...] + hb_ref[...]) * jax.nn.gelu(lg_ref[0])).astype(BF16), wl_ref[...])
    y = jax.nn.sigmoid(bgr_ref[0]) * ret + jax.nn.sigmoid(bgl_ref[0]) * lru
    o_ref[...] = x_ref[...] + g1_ref[0] * _dot(y.astype(BF16), wo_ref[...])


def _merge(x2, o_f, o_b, h_f, h_b, p2, g1, wr_bf, wl_bf, wo_bf, seq_len):
    t, d = x2.shape
    tm = 512
    per_batch = seq_len // tm
    tok = lambda: pl.BlockSpec((tm, d), lambda i: (i, 0))
    pcol = lambda slab: pl.BlockSpec((1, tm, d), lambda i: (slab, i, 0))
    wspec = lambda: pl.BlockSpec((d, d), lambda i: (0, 0))
    return pl.pallas_call(
        _merge_body,
        grid=(t // tm,),
        in_specs=[tok(), tok(), tok(), tok(), tok(),
                  pcol(SLAB_RGATE), pcol(SLAB_LRUG), pcol(SLAB_BGR), pcol(SLAB_BGL),
                  pl.BlockSpec((1, 1, d), lambda i: (i // per_batch, 0, 0)),
                  wspec(), wspec(), wspec()],
        out_specs=tok(),
        out_shape=jax.ShapeDtypeStruct((t, d), F32),
        compiler_params=pltpu.CompilerParams(
            dimension_semantics=("parallel",), vmem_limit_bytes=_vmem_limit(52 << 20)),
        name="merge",
    )(x2, o_f, o_b, h_f, h_b, p2, p2, p2, p2, g1, wr_bf, wl_bf, wo_bf)


PEER_TM = 512
PEER_EBLK = 1024
PEER_SLABS = PEER_EBLK // PEER_NKEYS
PEER_RB = 256


_GELU_K1 = 2.0 * math.sqrt(2.0 / math.pi) / math.log(2.0)
_GELU_K2 = _GELU_K1 * 0.044715


def _gelu_tanh(x):
    return x / (1.0 + jnp.exp2(x * (-_GELU_K1 - _GELU_K2 * (x * x))))


def _max_first(v, order_groups):
    groups = [(v[k * SUBLANES:(k + 1) * SUBLANES], o) for k, o in enumerate(order_groups)]
    while len(groups) > 1:
        nxt = []
        for a in range(0, len(groups) - 1, 2):
            (va, oa), (vb, ob) = groups[a], groups[a + 1]
            later = vb > va
            nxt.append((jnp.where(later, vb, va), jnp.where(later, ob, oa)))
        if len(groups) % 2:
            nxt.append(groups[-1])
        groups = nxt
    v8, o8 = groups[0]
    m = jnp.max(v8, axis=0, keepdims=True)
    first = jnp.min(jnp.where(v8 == m, o8, 1e9), axis=0, keepdims=True)
    return m, first


def _extract_top(vals, order_groups, n_take, on_take):
    order = jnp.concatenate(order_groups, axis=0)

    def body(a, state):
        v, carry = state
        m, first = _max_first(v, order_groups)
        hit = order == first
        carry = on_take(jnp.asarray(a, F32), m, hit, carry)
        return jnp.where(hit, NEG_INF, v), carry
    return body, n_take


def _topk_keys(scores):
    nk, n = scores[0].shape
    order = lax.broadcasted_iota(jnp.int32, (nk, n), 0).astype(F32)
    row8 = lax.broadcasted_iota(jnp.int32, (SUBLANES, n), 0).astype(F32)
    order_groups = [row8 + float(k * SUBLANES) for k in range(nk // SUBLANES)]
    row16 = lax.broadcasted_iota(jnp.int32, (PEER_TOPK, n), 0).astype(F32)

    def body(a, state):
        take = row16 == jnp.asarray(a, F32)
        out = []
        for v, top, idx in state:
            m, first = _max_first(v, order_groups)
            out.append((jnp.where(order == first, NEG_INF, v), jnp.where(take, m, top),
                        jnp.where(take, first, idx)))
        return tuple(out)

    zeros = jnp.zeros((PEER_TOPK, n), F32)
    res = lax.fori_loop(0, PEER_TOPK, body, tuple((s, zeros, zeros) for s in scores))
    return [(idx, top) for _, top, idx in res]


def _by_rank(idx, values, fill, shape):
    order = lax.broadcasted_iota(jnp.int32, shape, 0).astype(F32)
    out = jnp.full(shape, fill, F32)
    for a in range(PEER_TOPK):
        out = jnp.where(order == idx[a:a + 1], values[a], out)
    return out


def _route_head(s1, s2):
    n = s1.shape[1]
    (i1, t1), = _topk_keys([s1])
    (i2, t2), = _topk_keys([s2])
    row8 = lax.broadcasted_iota(jnp.int32, (SUBLANES, n), 0).astype(F32)
    row16 = lax.broadcasted_iota(jnp.int32, (PEER_TOPK, n), 0).astype(F32)
    cands = [t2 + t1[0:1]]
    flats = [row8, row8 + float(SUBLANES)]
    for a in range(1, 8):
        n_a = PEER_TOPK // (a + 1)
        cands.append(jnp.where(row8 < n_a, t2[0:8] + t1[a:a + 1], NEG_INF))
        flats.append(row8 + float(a * PEER_TOPK))
    cands.append(t1[8:16] + t2[0:1])
    flats.append((row8 + 8.0) * float(PEER_TOPK))
    cand = jnp.concatenate(cands, axis=0)

    def on_take(a, m, hit, sel):
        return jnp.where(hit, 1.0, sel)

    body, n_take = _extract_top(cand, flats, PEER_TOPK, on_take)
    _, sel = lax.fori_loop(0, n_take, body, (cand, jnp.zeros_like(cand)))

    cmax = cand[0:1]
    z = jnp.sum(sel * jnp.exp(cand - cmax), axis=0, keepdims=True)
    l_top = jnp.where(row8 == 0.0, jnp.sum(sel[0:16], axis=0, keepdims=True), 0.0)
    for a in range(1, 8):
        l_a = jnp.sum(sel[8 + 8 * a:16 + 8 * a], axis=0, keepdims=True)
        l_top = jnp.where(row8 == float(a), l_a, l_top)
    l_mat = jnp.concatenate([l_top, sel[72:80]], axis=0)
    l1 = _by_rank(i1, [l_mat[a:a + 1] for a in range(PEER_TOPK)], 0.0, s1.shape)
    r2 = _by_rank(i2, [float(a) for a in range(PEER_TOPK)], float(PEER_TOPK), s2.shape)
    c1 = jnp.exp(s1 - t1[0:1]) * (1.0 / z)
    e2 = jnp.exp(s2 - t2[0:1])
    return r2, e2, l1, c1


def _peer_body(x_ref, g_ref, sh_ref, sc_ref, g2_ref, fg_ref, wqh_ref, wql_ref, kh_ref, kl_ref, u_ref, vt_ref,
               o_ref, h2t_scr, s_scr, r2_scr, e2_scr, l1_scr, c1_scr, a_scr, wf_scr, acc_scr):
    j = pl.program_id(1)
    nj = pl.num_programs(1)
    tm = x_ref.shape[0]

    @pl.when(j == 0)
    def _():
        h2 = _rms_mod(x_ref[...], g_ref[...], sh_ref[0], sc_ref[0])
        h2t = h2.T
        h_hi, h_lo = _split_bf16(h2t)
        h2t_scr[...] = h_hi
        qt = _dot3(wqh_ref[...], wql_ref[...], h_hi, h_lo)
        q_hi, q_lo = _split_bf16(qt)
        for hp in range(2 * PEER_HEADS):
            sl = slice(hp * PEER_DHALF, (hp + 1) * PEER_DHALF)
            s_scr[hp] = _dot3(kh_ref[hp], kl_ref[hp], q_hi[sl], q_lo[sl])
        acc_scr[...] = jnp.zeros_like(acc_scr)

        def head_body(h, carry):
            for pb in range(tm // PEER_RB):
                ps = slice(pb * PEER_RB, (pb + 1) * PEER_RB)
                r2, e2, l1, c1 = _route_head(s_scr[2 * h, :, ps], s_scr[2 * h + 1, :, ps])
                for q in range(PEER_RB // LANES):
                    qs = slice(q * LANES, (q + 1) * LANES)
                    r2_scr[h, pb * (PEER_RB // LANES) + q] = r2[:, qs]
                    e2_scr[h, pb * (PEER_RB // LANES) + q] = e2[:, qs]
                l1_scr[h, :, ps] = l1
                c1_scr[h, :, ps] = c1
            return carry

        lax.fori_loop(0, PEER_HEADS, head_body, 0)

    n_lb = tm // LANES

    res = _dot(u_ref[...], h2t_scr[...])
    for q in range(n_lb):
        a_scr[q] = res[:, q * LANES:(q + 1) * LANES]

    def slab_body(s, carry):
        i1 = j * PEER_SLABS + s
        r0 = pl.multiple_of(s * PEER_NKEYS, PEER_NKEYS)
        lrows = [l1_scr[h, pl.ds(i1, 1), :] for h in range(PEER_HEADS)]
        crows = [c1_scr[h, pl.ds(i1, 1), :] for h in range(PEER_HEADS)]
        for lb in range(n_lb):
            ls = slice(lb * LANES, (lb + 1) * LANES)
            w = None
            for h in range(PEER_HEADS):
                term = jnp.where(r2_scr[h, lb] < lrows[h][:, ls], e2_scr[h, lb], 0.0) * crows[h][:, ls]
                w = term if w is None else w + term
            act = _gelu_tanh(a_scr[lb, pl.ds(r0, PEER_NKEYS), :])
            wf_scr[lb, pl.ds(r0, PEER_NKEYS), :] = (w * act).astype(BF16)
        return carry

    lax.fori_loop(0, PEER_SLABS, slab_body, 0)
    rhs = jnp.concatenate([wf_scr[q] for q in range(n_lb)], axis=1)
    acc_scr[...] += _dot(vt_ref[0], rhs)

    @pl.when(j == nj - 1)
    def _():
        x2 = x_ref[...] + g2_ref[0] * acc_scr[...].T
        ms = jnp.mean(x2 * x2, axis=-1, keepdims=True)
        o_ref[...] = x2 * lax.rsqrt(ms + EPS) * fg_ref[...]


def _peer(x2, g, shift, scale, gate, final_g, wq_hi, wq_lo, k_hi, k_lo, u_bf, vt_bf, seq_len):
    t, d = x2.shape
    n_exp = u_bf.shape[0]
    tm = PEER_TM
    per_batch = seq_len // tm
    mod = lambda: pl.BlockSpec((1, 1, d), lambda i, j: (i // per_batch, 0, 0))
    vec = lambda: pl.BlockSpec((1, d), lambda i, j: (0, 0))
    nq = wq_hi.shape[0]
    tab = lambda: pltpu.VMEM((PEER_HEADS, PEER_NKEYS, tm), F32)
    nblk = n_exp // PEER_EBLK
    return pl.pallas_call(
        _peer_body,
        grid=(t // tm, nblk),
        in_specs=[pl.BlockSpec((tm, d), lambda i, j: (i, 0)),
                  vec(), mod(), mod(), mod(), vec(),
                  pl.BlockSpec((nq, d), lambda i, j: (0, 0)),
                  pl.BlockSpec((nq, d), lambda i, j: (0, 0)),
                  pl.BlockSpec((2 * PEER_HEADS, PEER_NKEYS, PEER_DHALF), lambda i, j: (0, 0, 0)),
                  pl.BlockSpec((2 * PEER_HEADS, PEER_NKEYS, PEER_DHALF), lambda i, j: (0, 0, 0)),
                  pl.BlockSpec((PEER_EBLK, d), lambda i, j: (j, 0)),
                  pl.BlockSpec((1, d, PEER_EBLK), lambda i, j: (j, 0, 0))],
        out_specs=pl.BlockSpec((tm, d), lambda i, j: (i, 0)),
        out_shape=jax.ShapeDtypeStruct((t, d), F32),
        scratch_shapes=[pltpu.VMEM((d, tm), BF16),
                        pltpu.VMEM((2 * PEER_HEADS, PEER_NKEYS, tm), F32),
                        pltpu.VMEM((PEER_HEADS, tm // LANES, PEER_NKEYS, LANES), F32),
                        pltpu.VMEM((PEER_HEADS, tm // LANES, PEER_NKEYS, LANES), F32),
                        tab(), tab(),
                        pltpu.VMEM((tm // LANES, PEER_EBLK, LANES), F32),
                        pltpu.VMEM((tm // LANES, PEER_EBLK, LANES), BF16),
                        pltpu.VMEM((d, tm), F32)],
        compiler_params=pltpu.CompilerParams(
            dimension_semantics=("parallel", "arbitrary"), vmem_limit_bytes=_vmem_limit(56 << 20)),
        name="peer",
    )(x2, g, shift, scale, gate, final_g, wq_hi, wq_lo, k_hi, k_lo, u_bf, vt_bf)


def kernel(x, c, ctx, c_ctx, mod_w, mod_b, norm1_g, norm2_g, w_in, ret_decay, conv_w, conv_b, lru_wa, lru_ba,
           lru_wx, lru_bx, lru_lambda, w_ret_out, w_lru_out, w_out, peer_wq, peer_keys, peer_u, peer_v, final_g):
    bsz, seq, d = x.shape
    ctx_len = ctx.shape[1]
    l = 0

    pad = (-(bsz + 1)) % SUBLANES
    c_all = jnp.concatenate([c, c_ctx[None, :], jnp.zeros((pad, d), F32)], axis=0)
    mod = _modulation(c_all, mod_w[l], mod_b[l][None, :])
    mod_x = mod[:bsz].reshape(bsz, 1, 6, d)
    sh1x, sc1x, g1x, sh2x, sc2x, g2x = [mod_x[:, :, i, :] for i in range(6)]
    mod_c = mod[bsz:bsz + 1].reshape(1, 1, 6, d)
    sh1c, sc1c = mod_c[:, :, 0, :], mod_c[:, :, 1, :]

    w_in_bf = w_in[l].astype(BF16)
    g1 = norm1_g[l][None, :]
    x2 = x.reshape(bsz * seq, d)
    ctx2 = ctx.reshape(bsz * ctx_len, d)
    px = _in_projection(x2, g1, sh1x, sc1x, w_in_bf, seq, rope=_rope_tables(seq))
    pc = _in_projection(ctx2, g1, sh1c, sc1c, w_in_bf, ctx_len)
    px3 = px.reshape(px.shape[0], bsz, seq, d)
    pc3 = pc.reshape(pc.shape[0], bsz, ctx_len, d)

    lg = jax.nn.log_sigmoid(ret_decay[l].astype(F32))
    zeros_s = jnp.zeros((bsz, RET_HEADS, RET_DK, RET_DV), F32)
    _, _, sf, sb = _retention(pc3, lg, zeros_s, zeros_s)
    o_f, o_b, _, _ = _retention(px3, lg, sf, sb)

    cw = conv_w[l]
    cb = conv_b[l][None, :]
    h_dirs = []
    for dr in range(2):
        wa_bf = lru_wa[l, dr].astype(BF16)
        wx_bf = lru_wx[l, dr].astype(BF16)
        ba = lru_ba[l, dr][None, :]
        bx = lru_bx[l, dr][None, :]
        lam = lru_lambda[l, dr][None, :]
        h0 = jnp.zeros((bsz, 1, cw.shape[1]), F32)
        _, fin = _lru_direction(pc3, cw, cb, wa_bf, ba, wx_bf, bx, lam, h0, reverse=bool(dr))
        hx, _ = _lru_direction(px3, cw, cb, wa_bf, ba, wx_bf, bx, lam, fin, reverse=bool(dr))
        h_dirs.append(hx.reshape(bsz * seq, -1))

    x1 = _merge(x2, o_f.reshape(bsz * seq, -1), o_b.reshape(bsz * seq, -1), h_dirs[0], h_dirs[1], px, g1x,
                w_ret_out[l].astype(BF16), w_lru_out[l].astype(BF16), w_out[l].astype(BF16), seq)

    wq_t = peer_wq[l].T
    wq_hi = wq_t.astype(BF16)
    wq_lo = (wq_t - wq_hi.astype(F32)).astype(BF16)
    keys = peer_keys[l].reshape(2 * PEER_HEADS, PEER_NKEYS, PEER_DHALF)
    k_hi = keys.astype(BF16)
    k_lo = (keys - k_hi.astype(F32)).astype(BF16)
    u_bf = peer_u[l].astype(BF16)
    vt_bf = peer_v[l].astype(BF16).reshape(-1, PEER_EBLK, d).transpose(0, 2, 1)
    out = _peer(x1, norm2_g[l][None, :], sh2x, sc2x, g2x, final_g[None, :], wq_hi, wq_lo, k_hi, k_lo,
                u_bf, vt_bf, seq)
    return out.reshape(bsz, seq, d)
```

```python
import functools
import math

import jax
import jax.numpy as jnp
import numpy as np
from jax import lax
from jax.experimental import pallas as pl
from jax.experimental.pallas import tpu as pltpu

F32 = jnp.float32
BF16 = jnp.bfloat16

EPS = 1e-6
GRID_W = 64
ROPE_BASE = 10000.0
RET_HEADS = 4
RET_DK = 128
RET_DV = 256
RET_CHUNK = 128
LRU_BLOCKS = 8
LRU_BS = 128
LRU_C = 8.0
CONV_W = 4
PEER_HEADS = 8
PEER_DHALF = 64
PEER_NKEYS = 128
PEER_TOPK = 16
SLAB_QK, SLAB_V, SLAB_RGATE, SLAB_LRUX, SLAB_LRUG, SLAB_BGR, SLAB_BGL = range(7)

V7X_VMEM_BYTES = 64 * 1024 * 1024
LANES = 128
SUBLANES = 8

NEG_INF = float("-inf")


def _vmem_limit(nbytes):
    return int(min(max(nbytes, 16 * 1024 * 1024), V7X_VMEM_BYTES - 6 * 1024 * 1024))


def _dot(a, b):
    return jnp.dot(a, b, preferred_element_type=F32)


def _split_bf16(a):
    hi = a.astype(BF16)
    lo = (a - hi.astype(F32)).astype(BF16)
    return hi, lo


def _dot3(a_hi, a_lo, b_hi, b_lo):
    return _dot(a_hi, b_hi) + (_dot(a_hi, b_lo) + _dot(a_lo, b_hi))


def _mod_body(c_ref, w_ref, b_ref, o_ref):
    c = c_ref[...]
    sc = c * jax.nn.sigmoid(c)
    c_hi, c_lo = _split_bf16(sc)
    w_hi, w_lo = _split_bf16(w_ref[...])
    o_ref[...] = _dot3(c_hi, c_lo, w_hi, w_lo) + b_ref[...]


def _modulation(c_all, mod_w, mod_b):
    rows, d = c_all.shape
    n = mod_w.shape[1]
    tn = 1536
    return pl.pallas_call(
        _mod_body,
        grid=(n // tn,),
        in_specs=[pl.BlockSpec((rows, d), lambda j: (0, 0)),
                  pl.BlockSpec((d, tn), lambda j: (0, j)),
                  pl.BlockSpec((1, tn), lambda j: (0, j))],
        out_specs=pl.BlockSpec((rows, tn), lambda j: (0, j)),
        out_shape=jax.ShapeDtypeStruct((rows, n), F32),
        compiler_params=pltpu.CompilerParams(
            dimension_semantics=("arbitrary",), vmem_limit_bytes=_vmem_limit(40 << 20)),
        name="modulation",
    )(c_all, mod_w, mod_b)


INPROJ_TM = 256

def _rms_mod(x, g, shift, scale):
    ms = jnp.mean(x * x, axis=-1, keepdims=True)
    h = x * lax.rsqrt(ms + EPS) * g
    return h * (1.0 + scale) + shift


def _inproj_body(x_ref, g_ref, sh_ref, sc_ref, w_ref, *rest, use_rope):
    if use_rope:
        cos_ref, sin_ref, o_ref = rest
    else:
        o_ref, = rest
    h = _rms_mod(x_ref[...], g_ref[...], sh_ref[0], sc_ref[0]).astype(BF16)
    d = h.shape[1]
    for k in range(o_ref.shape[0]):
        res = _dot(h, w_ref[:, k * d:(k + 1) * d])
        if k == SLAB_QK:
            lane = lax.broadcasted_iota(jnp.int32, (res.shape[0], RET_DK), 1)
            lane_low = (lane % 64) < 32
            parts = []
            for gi in range(d // RET_DK):
                part = res[:, gi * RET_DK:(gi + 1) * RET_DK]
                if use_rope:
                    part = _rope(part, cos_ref[...], sin_ref[...], lane_low)
                if gi >= RET_HEADS:
                    part = part * (RET_DK ** -0.5)
                parts.append(part)
            res = jnp.concatenate(parts, axis=-1)
        o_ref[k] = res


def _in_projection(x2, g, shift, scale, w_bf, seq_len, rope=None):
    t, d = x2.shape
    n = w_bf.shape[1]
    tm = min(INPROJ_TM, seq_len)
    per_batch = seq_len // tm
    if shift.shape[0] == 1:
        mod_map = lambda i: (0, 0, 0)
    else:
        mod_map = lambda i: (i // per_batch, 0, 0)
    vmem = 2 * (d * n * 2 + (n // d) * tm * d * 4 + tm * d * 4) + (4 << 20)
    in_specs = [pl.BlockSpec((tm, d), lambda i: (i, 0)),
                pl.BlockSpec((1, d), lambda i: (0, 0)),
                pl.BlockSpec((1, 1, d), mod_map),
                pl.BlockSpec((1, 1, d), mod_map),
                pl.BlockSpec((d, n), lambda i: (0, 0))]
    args = [x2, g, shift, scale, w_bf]
    if rope is not None:
        in_specs += [pl.BlockSpec((tm, RET_DK), lambda i: (i % per_batch, 0))] * 2
        args += list(rope)
    return pl.pallas_call(
        functools.partial(_inproj_body, use_rope=rope is not None),
        grid=(t // tm,),
        in_specs=in_specs,
        out_specs=pl.BlockSpec((n // d, tm, d), lambda i: (0, i, 0)),
        out_shape=jax.ShapeDtypeStruct((n // d, t, d), F32),
        compiler_params=pltpu.CompilerParams(
            dimension_semantics=("parallel",), vmem_limit_bytes=_vmem_limit(vmem)),
        name="in_projection",
    )(*args)


def _rope_tables(n_tok):
    quarter = RET_DK // 4
    pos = jnp.arange(n_tok)
    row = (pos // GRID_W).astype(F32)
    col = (pos % GRID_W).astype(F32)
    inv = ROPE_BASE ** (-jnp.arange(quarter, dtype=F32) / quarter)
    ar = row[:, None] * inv[None, :]
    ac = col[:, None] * inv[None, :]
    cos_t = jnp.concatenate([jnp.cos(ar), jnp.cos(ar), jnp.cos(ac), jnp.cos(ac)], axis=-1)
    sin_t = jnp.concatenate([-jnp.sin(ar), jnp.sin(ar), -jnp.sin(ac), jnp.sin(ac)], axis=-1)
    return cos_t, sin_t


def _rope(t, cos_t, sin_t, lane_low):
    swapped = jnp.where(lane_low, pltpu.roll(t, 96, 1), pltpu.roll(t, 32, 1))
    return t * cos_t + swapped * sin_t


def _ret_body(lg_ref, qkf_ref, vf_ref, qkb_ref, vb_ref, s0f_ref, s0b_ref,
              of_ref, ob_ref, sff_ref, sfb_ref, sf_scr, sb_scr, dmat_scr, xi_scr, zeta_scr, cdec_scr, *, n_batch):
    c = pl.program_id(0)
    n = pl.num_programs(0)
    C = RET_CHUNK

    @pl.when(c == 0)
    def _():
        sf_scr[...] = s0f_ref[...]
        sb_scr[...] = s0b_ref[...]
        ii = lax.broadcasted_iota(jnp.int32, (C, C), 0)
        jj = lax.broadcasted_iota(jnp.int32, (C, C), 1)
        col_i = ii.astype(F32)
        for d in range(2):
            for h in range(RET_HEADS):
                lg = lg_ref[d, h]
                if d == 0:
                    diff = (ii - jj).astype(F32)
                    keep = ii >= jj
                    xi = jnp.exp(lg * (col_i + 1.0))
                    zeta = jnp.exp(lg * (C - 1.0 - col_i))
                else:
                    diff = (jj - ii).astype(F32)
                    keep = jj > ii
                    xi = jnp.exp(lg * (C - col_i))
                    zeta = jnp.exp(lg * col_i)
                dmat_scr[d, h] = jnp.where(keep, jnp.exp(lg * jnp.maximum(diff, 0.0)), 0.0)
                xi_scr[d, h] = xi
                zeta_scr[d, h] = zeta
                cdec_scr[d, h] = jnp.exp(jnp.full((SUBLANES, RET_DV), lg * C, F32))

    refs = ((qkf_ref, vf_ref, sf_scr, of_ref), (qkb_ref, vb_ref, sb_scr, ob_ref))
    k_off = RET_HEADS * RET_DK

    def batch_body(b, carry):
        for d in range(2):
            qk_ref, v_ref, s_scr, o_ref = refs[d]
            for h in range(RET_HEADS):
                dmat, xi, zeta = dmat_scr[d, h], xi_scr[d, h], zeta_scr[d, h]
                cdec = cdec_scr[d, h, 0:1, :]
                q = qk_ref[0, b, :, h * RET_DK:(h + 1) * RET_DK]
                k = qk_ref[0, b, :, k_off + h * RET_DK:k_off + (h + 1) * RET_DK]
                v = v_ref[0, b, :, h * RET_DV:(h + 1) * RET_DV].astype(BF16)
                s_old = s_scr[b, h]
                scores = lax.dot_general(q.astype(BF16), k.astype(BF16), (((1,), (1,)), ((), ())),
                                         preferred_element_type=F32) * dmat
                o = _dot(scores.astype(BF16), v) + _dot((q * xi).astype(BF16), s_old.astype(BF16))
                kz_t = (k * zeta).T.astype(BF16)
                s_scr[b, h] = cdec * s_old + _dot(kz_t, v)
                o_ref[b, :, h * RET_DV:(h + 1) * RET_DV] = o
        return carry

    lax.fori_loop(0, n_batch, batch_body, 0)

    @pl.when(c == n - 1)
    def _():
        sff_ref[...] = sf_scr[...]
        sfb_ref[...] = sb_scr[...]


def _retention(p4, lg, s0f, s0b):
    _, bsz, seq, d = p4.shape
    n = seq // RET_CHUNK
    C = RET_CHUNK
    v_w = RET_HEADS * RET_DV
    fwd = lambda c: c
    bwd = lambda c: n - 1 - c

    def slab_spec(slab, order):
        return pl.BlockSpec((1, bsz, C, d), lambda c: (slab, 0, order(c), 0))

    st_spec = pl.BlockSpec((bsz, RET_HEADS, RET_DK, RET_DV), lambda c: (0, 0, 0, 0))
    st_shape = jax.ShapeDtypeStruct((bsz, RET_HEADS, RET_DK, RET_DV), F32)
    o_shape = jax.ShapeDtypeStruct((bsz, seq, v_w), F32)
    body = functools.partial(_ret_body, n_batch=bsz)
    return pl.pallas_call(
        body,
        grid=(n,),
        in_specs=[pl.BlockSpec(memory_space=pltpu.SMEM),
                  slab_spec(SLAB_QK, fwd), slab_spec(SLAB_V, fwd),
                  slab_spec(SLAB_QK, bwd), slab_spec(SLAB_V, bwd),
                  st_spec, st_spec],
        out_specs=[pl.BlockSpec((bsz, C, v_w), lambda c: (0, c, 0)),
                   pl.BlockSpec((bsz, C, v_w), lambda c: (0, n - 1 - c, 0)),
                   st_spec, st_spec],
        out_shape=[o_shape, o_shape, st_shape, st_shape],
        scratch_shapes=[pltpu.VMEM((bsz, RET_HEADS, RET_DK, RET_DV), F32),
                        pltpu.VMEM((bsz, RET_HEADS, RET_DK, RET_DV), F32),
                        pltpu.VMEM((2, RET_HEADS, C, C), F32),
                        pltpu.VMEM((2, RET_HEADS, C, RET_DK), F32),
                        pltpu.VMEM((2, RET_HEADS, C, RET_DK), F32),
                        pltpu.VMEM((2, RET_HEADS, SUBLANES, RET_DV), F32)],
        compiler_params=pltpu.CompilerParams(
            dimension_semantics=("arbitrary",), vmem_limit_bytes=_vmem_limit(52 << 20)),
        name="retention",
    )(lg, p4, p4, p4, p4, s0f, s0b)


def _lru_body(uc_ref, up_ref, un_ref, cw_ref, cb_ref, wa_ref, ba_ref, wx_ref, bx_ref, lam_ref, h0_ref,
              h_ref, fin_ref, ext_scr, a_scr, b_scr, carry_scr, *, reverse, t_blk):
    tb = pl.program_id(1)
    nt = pl.num_programs(1)
    pos = (nt - 1 - tb) if reverse else tb
    T = t_blk

    @pl.when(tb == 0)
    def _():
        carry_scr[...] = h0_ref[0]

    has_prev = (pos > 0).astype(F32)
    has_next = (pos < nt - 1).astype(F32)
    ext_scr[0:8, :] = up_ref[0, 0] * has_prev
    ext_scr[8:8 + T, :] = uc_ref[0, 0]
    ext_scr[8 + T:16 + T, :] = un_ref[0, 0] * has_next
    full = ext_scr[...]
    rows = full.shape[0]
    u = (cb_ref[...]
         + pltpu.roll(full, 2, 0)[8:8 + T] * cw_ref[0:1, :]
         + pltpu.roll(full, 1, 0)[8:8 + T] * cw_ref[1:2, :]
         + full[8:8 + T] * cw_ref[2:3, :]
         + pltpu.roll(full, rows - 1, 0)[8:8 + T] * cw_ref[3:4, :])
    ub = u.astype(BF16)
    rr = []
    xx = []
    for nb in range(LRU_BLOCKS):
        blk = ub[:, nb * LRU_BS:(nb + 1) * LRU_BS]
        rr.append(_dot(blk, wa_ref[nb]))
        xx.append(_dot(blk, wx_ref[nb]))
    r = jax.nn.sigmoid(jnp.concatenate(rr, axis=-1) + ba_ref[...])
    gate_i = jax.nn.sigmoid(jnp.concatenate(xx, axis=-1) + bx_ref[...])
    z = -lam_ref[...]
    softplus = jnp.maximum(z, 0.0) + jnp.log1p(jnp.exp(-jnp.abs(z)))
    log_a = (-LRU_C) * r * softplus
    a = jnp.exp(log_a)
    a_scr[...] = a
    b_scr[...] = jnp.sqrt(1.0 - a * a) * (gate_i * u)

    row = lax.broadcasted_iota(jnp.int32, (SUBLANES, a_scr.shape[1]), 0)
    n_grp = T // SUBLANES

    def step(i, hprev):
        g = (n_grp - 1 - i) if reverse else i
        r0 = pl.multiple_of(g * SUBLANES, SUBLANES)
        a = a_scr[pl.ds(r0, SUBLANES), :]
        b = b_scr[pl.ds(r0, SUBLANES), :]
        for s in (1, 2, 4):
            if reverse:
                valid = row < SUBLANES - s
                shift = SUBLANES - s
            else:
                valid = row >= s
                shift = s
            a_s = jnp.where(valid, pltpu.roll(a, shift, 0), 1.0)
            b_s = jnp.where(valid, pltpu.roll(b, shift, 0), 0.0)
            b = a * b_s + b
            a = a * a_s
        h = a * hprev + b
        h_ref[0, pl.ds(r0, SUBLANES), :] = h
        return h[0:1, :] if reverse else h[SUBLANES - 1:SUBLANES, :]

    hfin = lax.fori_loop(0, n_grp, step, carry_scr[...], unroll=2)
    carry_scr[...] = hfin
    fin_ref[0] = hfin


def _lru_direction(p4, cw, cb, wa_bf, ba, wx_bf, bx, lam, h0, reverse):
    _, bsz, seq, d = p4.shape
    T = min(512, seq)
    nt = seq // T
    sl = SLAB_LRUX
    hb = T // SUBLANES
    n_halo = seq // SUBLANES
    order = (lambda tb: nt - 1 - tb) if reverse else (lambda tb: tb)
    body = functools.partial(_lru_body, reverse=reverse, t_blk=T)
    vec = lambda: pl.BlockSpec((1, d), lambda b, tb: (0, 0))
    return pl.pallas_call(
        body,
        grid=(bsz, nt),
        in_specs=[pl.BlockSpec((1, 1, T, d), lambda b, tb: (sl, b, order(tb), 0)),
                  pl.BlockSpec((1, 1, SUBLANES, d), lambda b, tb: (sl, b, jnp.maximum(order(tb) * hb - 1, 0), 0)),
                  pl.BlockSpec((1, 1, SUBLANES, d),
                               lambda b, tb: (sl, b, jnp.minimum((order(tb) + 1) * hb, n_halo - 1), 0)),
                  pl.BlockSpec((CONV_W, d), lambda b, tb: (0, 0)),
                  vec(),
                  pl.BlockSpec((LRU_BLOCKS, LRU_BS, LRU_BS), lambda b, tb: (0, 0, 0)),
                  vec(),
                  pl.BlockSpec((LRU_BLOCKS, LRU_BS, LRU_BS), lambda b, tb: (0, 0, 0)),
                  vec(), vec(),
                  pl.BlockSpec((1, 1, d), lambda b, tb: (b, 0, 0))],
        out_specs=[pl.BlockSpec((1, T, d), lambda b, tb: (b, order(tb), 0)),
                   pl.BlockSpec((1, 1, d), lambda b, tb: (b, 0, 0))],
        out_shape=[jax.ShapeDtypeStruct((bsz, seq, d), F32), jax.ShapeDtypeStruct((bsz, 1, d), F32)],
        scratch_shapes=[pltpu.VMEM((T + 2 * SUBLANES, d), F32), pltpu.VMEM((T, d), F32),
                        pltpu.VMEM((T, d), F32), pltpu.VMEM((1, d), F32)],
        compiler_params=pltpu.CompilerParams(
            dimension_semantics=("parallel", "arbitrary"), vmem_limit_bytes=_vmem_limit(40 << 20)),
        name="rglru_rev" if reverse else "rglru_fwd",
    )(p4, p4, p4, cw, cb, wa_bf, ba, wx_bf, bx, lam, h0)


def _merge_body(x_ref, of_ref, ob_ref, hf_ref, hb_ref, rg_ref, lg_ref, bgr_ref, bgl_ref, g1_ref,
                wr_ref, wl_ref, wo_ref, o_ref):
    o = of_ref[...] + ob_ref[...]
    parts = []
    for h in range(RET_HEADS):
        oh = o[:, h * RET_DV:(h + 1) * RET_DV]
        mu = jnp.mean(oh, axis=-1, keepdims=True)
        cen = oh - mu
        var = jnp.mean(cen * cen, axis=-1, keepdims=True)
        parts.append(cen * lax.rsqrt(var + EPS))
    on = jnp.concatenate(parts, axis=-1)
    rg = rg_ref[0]
    ret = _dot((on * (rg * jax.nn.sigmoid(rg))).astype(BF16), wr_ref[...])
    lru = _dot(((hf_ref[...] + hb_ref[...]) * jax.nn.gelu(lg_ref[0])).astype(BF16), wl_ref[...])
    y = jax.nn.sigmoid(bgr_ref[0]) * ret + jax.nn.sigmoid(bgl_ref[0]) * lru
    o_ref[...] = x_ref[...] + g1_ref[0] * _dot(y.astype(BF16), wo_ref[...])


def _merge(x2, o_f, o_b, h_f, h_b, p2, g1, wr_bf, wl_bf, wo_bf, seq_len):
    t, d = x2.shape
    tm = 512
    per_batch = seq_len // tm
    tok = lambda: pl.BlockSpec((tm, d), lambda i: (i, 0))
    pcol = lambda slab: pl.BlockSpec((1, tm, d), lambda i: (slab, i, 0))
    wspec = lambda: pl.BlockSpec((d, d), lambda i: (0, 0))
    return pl.pallas_call(
        _merge_body,
        grid=(t // tm,),
        in_specs=[tok(), tok(), tok(), tok(), tok(),
                  pcol(SLAB_RGATE), pcol(SLAB_LRUG), pcol(SLAB_BGR), pcol(SLAB_BGL),
                  pl.BlockSpec((1, 1, d), lambda i: (i // per_batch, 0, 0)),
                  wspec(), wspec(), wspec()],
        out_specs=tok(),
        out_shape=jax.ShapeDtypeStruct((t, d), F32),
        compiler_params=pltpu.CompilerParams(
            dimension_semantics=("parallel",), vmem_limit_bytes=_vmem_limit(52 << 20)),
        name="merge",
    )(x2, o_f, o_b, h_f, h_b, p2, p2, p2, p2, g1, wr_bf, wl_bf, wo_bf)


PEER_TM = 512
PEER_EBLK = 2048
PEER_SLABS = PEER_EBLK // PEER_NKEYS
PEER_RB = 256


_GELU_K1 = 2.0 * math.sqrt(2.0 / math.pi) / math.log(2.0)
_GELU_K2 = _GELU_K1 * 0.044715


def _gelu_tanh(x):
    return x / (1.0 + jnp.exp2(x * (-_GELU_K1 - _GELU_K2 * (x * x))))


def _max_first(v, order_groups):
    groups = [(v[k * SUBLANES:(k + 1) * SUBLANES], o) for k, o in enumerate(order_groups)]
    while len(groups) > 1:
        nxt = []
        for a in range(0, len(groups) - 1, 2):
            (va, oa), (vb, ob) = groups[a], groups[a + 1]
            later = vb > va
            nxt.append((jnp.where(later, vb, va), jnp.where(later, ob, oa)))
        if len(groups) % 2:
            nxt.append(groups[-1])
        groups = nxt
    v8, o8 = groups[0]
    m = jnp.max(v8, axis=0, keepdims=True)
    first = jnp.min(jnp.where(v8 == m, o8, 1e9), axis=0, keepdims=True)
    return m, first


def _extract_top(vals, order_groups, n_take, on_take):
    order = jnp.concatenate(order_groups, axis=0)

    def body(a, state):
        v, carry = state
        m, first = _max_first(v, order_groups)
        hit = order == first
        carry = on_take(jnp.asarray(a, F32), m, hit, carry)
        return jnp.where(hit, NEG_INF, v), carry
    return body, n_take


def _topk_keys(scores):
    nk, n = scores[0].shape
    order = lax.broadcasted_iota(jnp.int32, (nk, n), 0).astype(F32)
    row8 = lax.broadcasted_iota(jnp.int32, (SUBLANES, n), 0).astype(F32)
    order_groups = [row8 + float(k * SUBLANES) for k in range(nk // SUBLANES)]
    row16 = lax.broadcasted_iota(jnp.int32, (PEER_TOPK, n), 0).astype(F32)

    def body(a, state):
        take = row16 == jnp.asarray(a, F32)
        out = []
        for v, top, idx in state:
            m, first = _max_first(v, order_groups)
            out.append((jnp.where(order == first, NEG_INF, v), jnp.where(take, m, top),
                        jnp.where(take, first, idx)))
        return tuple(out)

    zeros = jnp.zeros((PEER_TOPK, n), F32)
    res = lax.fori_loop(0, PEER_TOPK, body, tuple((s, zeros, zeros) for s in scores))
    return [(idx, top) for _, top, idx in res]


def _by_rank(idx, values, fill, shape):
    order = lax.broadcasted_iota(jnp.int32, shape, 0).astype(F32)
    out = jnp.full(shape, fill, F32)
    for a in range(PEER_TOPK):
        out = jnp.where(order == idx[a:a + 1], values[a], out)
    return out


def _route_head(s1, s2):
    n = s1.shape[1]
    (i1, t1), = _topk_keys([s1])
    (i2, t2), = _topk_keys([s2])
    row8 = lax.broadcasted_iota(jnp.int32, (SUBLANES, n), 0).astype(F32)
    row16 = lax.broadcasted_iota(jnp.int32, (PEER_TOPK, n), 0).astype(F32)
    cands = [t2 + t1[0:1]]
    flats = [row8, row8 + float(SUBLANES)]
    for a in range(1, 8):
        n_a = PEER_TOPK // (a + 1)
        cands.append(jnp.where(row8 < n_a, t2[0:8] + t1[a:a + 1], NEG_INF))
        flats.append(row8 + float(a * PEER_TOPK))
    cands.append(t1[8:16] + t2[0:1])
    flats.append((row8 + 8.0) * float(PEER_TOPK))
    cand = jnp.concatenate(cands, axis=0)

    def on_take(a, m, hit, sel):
        return jnp.where(hit, 1.0, sel)

    body, n_take = _extract_top(cand, flats, PEER_TOPK, on_take)
    _, sel = lax.fori_loop(0, n_take, body, (cand, jnp.zeros_like(cand)))

    cmax = cand[0:1]
    z = jnp.sum(sel * jnp.exp(cand - cmax), axis=0, keepdims=True)
    l_top = jnp.where(row8 == 0.0, jnp.sum(sel[0:16], axis=0, keepdims=True), 0.0)
    for a in range(1, 8):
        l_a = jnp.sum(sel[8 + 8 * a:16 + 8 * a], axis=0, keepdims=True)
        l_top = jnp.where(row8 == float(a), l_a, l_top)
    l_mat = jnp.concatenate([l_top, sel[72:80]], axis=0)
    l1 = _by_rank(i1, [l_mat[a:a + 1] for a in range(PEER_TOPK)], 0.0, s1.shape)
    r2 = _by_rank(i2, [float(a) for a in range(PEER_TOPK)], float(PEER_TOPK), s2.shape)
    c1 = jnp.exp(s1 - t1[0:1]) * (1.0 / z)
    e2 = jnp.exp(s2 - t2[0:1])
    return r2, e2, l1, c1


def _peer_body(x_ref, g_ref, sh_ref, sc_ref, g2_ref, fg_ref, wqh_ref, wql_ref, kh_ref, kl_ref, u_ref, vt_ref,
               o_ref, h2t_scr, s_scr, r2_scr, e2_scr, l1_scr, c1_scr, a_scr, wf_scr, acc_scr):
    j = pl.program_id(1)
    nj = pl.num_programs(1)
    tm = x_ref.shape[0]

    @pl.when(j == 0)
    def _():
        h2 = _rms_mod(x_ref[...], g_ref[...], sh_ref[0], sc_ref[0])
        h2t = h2.T
        h_hi, h_lo = _split_bf16(h2t)
        h2t_scr[...] = h_hi
        qt = _dot3(wqh_ref[...], wql_ref[...], h_hi, h_lo)
        q_hi, q_lo = _split_bf16(qt)
        for hp in range(2 * PEER_HEADS):
            sl = slice(hp * PEER_DHALF, (hp + 1) * PEER_DHALF)
            s_scr[hp] = _dot3(kh_ref[hp], kl_ref[hp], q_hi[sl], q_lo[sl])
        acc_scr[...] = jnp.zeros_like(acc_scr)

        def head_body(h, carry):
            for pb in range(tm // PEER_RB):
                ps = slice(pb * PEER_RB, (pb + 1) * PEER_RB)
                r2, e2, l1, c1 = _route_head(s_scr[2 * h, :, ps], s_scr[2 * h + 1, :, ps])
                for q in range(PEER_RB // LANES):
                    qs = slice(q * LANES, (q + 1) * LANES)
                    r2_scr[h, pb * (PEER_RB // LANES) + q] = r2[:, qs]
                    e2_scr[h, pb * (PEER_RB // LANES) + q] = e2[:, qs]
                l1_scr[h, :, ps] = l1
                c1_scr[h, :, ps] = c1
            return carry

        lax.fori_loop(0, PEER_HEADS, head_body, 0)

    n_lb = tm // LANES

    res = _dot(u_ref[...], h2t_scr[...])
    for q in range(n_lb):
        a_scr[q] = res[:, q * LANES:(q + 1) * LANES]

    def slab_body(s, carry):
        i1 = j * PEER_SLABS + s
        r0 = pl.multiple_of(s * PEER_NKEYS, PEER_NKEYS)
        lrows = [l1_scr[h, pl.ds(i1, 1), :] for h in range(PEER_HEADS)]
        crows = [c1_scr[h, pl.ds(i1, 1), :] for h in range(PEER_HEADS)]
        for lb in range(n_lb):
            ls = slice(lb * LANES, (lb + 1) * LANES)
            w = None
            for h in range(PEER_HEADS):
                term = jnp.where(r2_scr[h, lb] < lrows[h][:, ls], e2_scr[h, lb], 0.0) * crows[h][:, ls]
                w = term if w is None else w + term
            act = _gelu_tanh(a_scr[lb, pl.ds(r0, PEER_NKEYS), :])
            wf_scr[lb, pl.ds(r0, PEER_NKEYS), :] = (w * act).astype(BF16)
        return carry

    lax.fori_loop(0, PEER_SLABS, slab_body, 0)
    rhs = jnp.concatenate([wf_scr[q] for q in range(n_lb)], axis=1)
    acc_scr[...] += _dot(vt_ref[0], rhs)

    @pl.when(j == nj - 1)
    def _():
        x2 = x_ref[...] + g2_ref[0] * acc_scr[...].T
        ms = jnp.mean(x2 * x2, axis=-1, keepdims=True)
        o_ref[...] = x2 * lax.rsqrt(ms + EPS) * fg_ref[...]


def _peer(x2, g, shift, scale, gate, final_g, wq_hi, wq_lo, k_hi, k_lo, u_bf, vt_bf, seq_len):
    t, d = x2.shape
    n_exp = u_bf.shape[0]
    tm = PEER_TM
    per_batch = seq_len // tm
    mod = lambda: pl.BlockSpec((1, 1, d), lambda i, j: (i // per_batch, 0, 0))
    vec = lambda: pl.BlockSpec((1, d), lambda i, j: (0, 0))
    nq = wq_hi.shape[0]
    tab = lambda: pltpu.VMEM((PEER_HEADS, PEER_NKEYS, tm), F32)
    nblk = n_exp // PEER_EBLK
    return pl.pallas_call(
        _peer_body,
        grid=(t // tm, nblk),
        in_specs=[pl.BlockSpec((tm, d), lambda i, j: (i, 0)),
                  vec(), mod(), mod(), mod(), vec(),
                  pl.BlockSpec((nq, d), lambda i, j: (0, 0)),
                  pl.BlockSpec((nq, d), lambda i, j: (0, 0)),
                  pl.BlockSpec((2 * PEER_HEADS, PEER_NKEYS, PEER_DHALF), lambda i, j: (0, 0, 0)),
                  pl.BlockSpec((2 * PEER_HEADS, PEER_NKEYS, PEER_DHALF), lambda i, j: (0, 0, 0)),
                  pl.BlockSpec((PEER_EBLK, d), lambda i, j: (j, 0)),
                  pl.BlockSpec((1, d, PEER_EBLK), lambda i, j: (j, 0, 0))],
        out_specs=pl.BlockSpec((tm, d), lambda i, j: (i, 0)),
        out_shape=jax.ShapeDtypeStruct((t, d), F32),
        scratch_shapes=[pltpu.VMEM((d, tm), BF16),
                        pltpu.VMEM((2 * PEER_HEADS, PEER_NKEYS, tm), F32),
                        pltpu.VMEM((PEER_HEADS, tm // LANES, PEER_NKEYS, LANES), F32),
                        pltpu.VMEM((PEER_HEADS, tm // LANES, PEER_NKEYS, LANES), F32),
                        tab(), tab(),
                        pltpu.VMEM((tm // LANES, PEER_EBLK, LANES), F32),
                        pltpu.VMEM((tm // LANES, PEER_EBLK, LANES), BF16),
                        pltpu.VMEM((d, tm), F32)],
        compiler_params=pltpu.CompilerParams(
            dimension_semantics=("parallel", "arbitrary"), vmem_limit_bytes=_vmem_limit(58 << 20)),
        name="peer",
    )(x2, g, shift, scale, gate, final_g, wq_hi, wq_lo, k_hi, k_lo, u_bf, vt_bf)


def kernel(x, c, ctx, c_ctx, mod_w, mod_b, norm1_g, norm2_g, w_in, ret_decay, conv_w, conv_b, lru_wa, lru_ba,
           lru_wx, lru_bx, lru_lambda, w_ret_out, w_lru_out, w_out, peer_wq, peer_keys, peer_u, peer_v, final_g):
    bsz, seq, d = x.shape
    ctx_len = ctx.shape[1]
    l = 0

    pad = (-(bsz + 1)) % SUBLANES
    c_all = jnp.concatenate([c, c_ctx[None, :], jnp.zeros((pad, d), F32)], axis=0)
    mod = _modulation(c_all, mod_w[l], mod_b[l][None, :])
    mod_x = mod[:bsz].reshape(bsz, 1, 6, d)
    sh1x, sc1x, g1x, sh2x, sc2x, g2x = [mod_x[:, :, i, :] for i in range(6)]
    mod_c = mod[bsz:bsz + 1].reshape(1, 1, 6, d)
    sh1c, sc1c = mod_c[:, :, 0, :], mod_c[:, :, 1, :]

    w_in_bf = w_in[l].astype(BF16)
    g1 = norm1_g[l][None, :]
    x2 = x.reshape(bsz * seq, d)
    ctx2 = ctx.reshape(bsz * ctx_len, d)
    px = _in_projection(x2, g1, sh1x, sc1x, w_in_bf, seq, rope=_rope_tables(seq))
    pc = _in_projection(ctx2, g1, sh1c, sc1c, w_in_bf, ctx_len)
    px3 = px.reshape(px.shape[0], bsz, seq, d)
    pc3 = pc.reshape(pc.shape[0], bsz, ctx_len, d)

    lg = jax.nn.log_sigmoid(ret_decay[l].astype(F32))
    zeros_s = jnp.zeros((bsz, RET_HEADS, RET_DK, RET_DV), F32)
    _, _, sf, sb = _retention(pc3, lg, zeros_s, zeros_s)
    o_f, o_b, _, _ = _retention(px3, lg, sf, sb)

    cw = conv_w[l]
    cb = conv_b[l][None, :]
    h_dirs = []
    for dr in range(2):
        wa_bf = lru_wa[l, dr].astype(BF16)
        wx_bf = lru_wx[l, dr].astype(BF16)
        ba = lru_ba[l, dr][None, :]
        bx = lru_bx[l, dr][None, :]
        lam = lru_lambda[l, dr][None, :]
        h0 = jnp.zeros((bsz, 1, cw.shape[1]), F32)
        _, fin = _lru_direction(pc3, cw, cb, wa_bf, ba, wx_bf, bx, lam, h0, reverse=bool(dr))
        hx, _ = _lru_direction(px3, cw, cb, wa_bf, ba, wx_bf, bx, lam, fin, reverse=bool(dr))
        h_dirs.append(hx.reshape(bsz * seq, -1))

    x1 = _merge(x2, o_f.reshape(bsz * seq, -1), o_b.reshape(bsz * seq, -1), h_dirs[0], h_dirs[1], px, g1x,
                w_ret_out[l].astype(BF16), w_lru_out[l].astype(BF16), w_out[l].astype(BF16), seq)

    wq_t = peer_wq[l].T
    wq_hi = wq_t.astype(BF16)
    wq_lo = (wq_t - wq_hi.astype(F32)).astype(BF16)
    keys = peer_keys[l].reshape(2 * PEER_HEADS, PEER_NKEYS, PEER_DHALF)
    k_hi = keys.astype(BF16)
    k_lo = (keys - k_hi.astype(F32)).astype(BF16)
    u_bf = peer_u[l].astype(BF16)
    vt_bf = peer_v[l].astype(BF16).reshape(-1, PEER_EBLK, d).transpose(0, 2, 1)
    out = _peer(x1, norm2_g[l][None, :], sh2x, sc2x, g2x, final_g[None, :], wq_hi, wq_lo, k_hi, k_lo,
                u_bf, vt_bf, seq)
    return out.reshape(bsz, seq, d)
```

```python
import functools
import math

import jax
import jax.numpy as jnp
import numpy as np
from jax import lax
from jax.experimental import pallas as pl
from jax.experimental.pallas import tpu as pltpu

F32 = jnp.float32
BF16 = jnp.bfloat16

EPS = 1e-6
GRID_W = 64
ROPE_BASE = 10000.0
RET_HEADS = 4
RET_DK = 128
RET_DV = 256
RET_CHUNK = 128
LRU_BLOCKS = 8
LRU_BS = 128
LRU_C = 8.0
CONV_W = 4
PEER_HEADS = 8
PEER_DHALF = 64
PEER_NKEYS = 128
PEER_TOPK = 16
SLAB_QK, SLAB_V, SLAB_RGATE, SLAB_LRUX, SLAB_LRUG, SLAB_BGR, SLAB_BGL = range(7)

V7X_VMEM_BYTES = 64 * 1024 * 1024
LANES = 128
SUBLANES = 8

NEG_INF = float("-inf")


def _vmem_limit(nbytes):
    return int(min(max(nbytes, 16 * 1024 * 1024), V7X_VMEM_BYTES - 6 * 1024 * 1024))


def _dot(a, b):
    return jnp.dot(a, b, preferred_element_type=F32)


def _split_bf16(a):
    hi = a.astype(BF16)
    lo = (a - hi.astype(F32)).astype(BF16)
    return hi, lo


def _dot3(a_hi, a_lo, b_hi, b_lo):
    return _dot(a_hi, b_hi) + (_dot(a_hi, b_lo) + _dot(a_lo, b_hi))


def _mod_body(c_ref, w_ref, b_ref, o_ref):
    c = c_ref[...]
    sc = c * jax.nn.sigmoid(c)
    c_hi, c_lo = _split_bf16(sc)
    w_hi, w_lo = _split_bf16(w_ref[...])
    o_ref[...] = _dot3(c_hi, c_lo, w_hi, w_lo) + b_ref[...]


def _modulation(c_all, mod_w, mod_b):
    rows, d = c_all.shape
    n = mod_w.shape[1]
    tn = 1536
    return pl.pallas_call(
        _mod_body,
        grid=(n // tn,),
        in_specs=[pl.BlockSpec((rows, d), lambda j: (0, 0)),
                  pl.BlockSpec((d, tn), lambda j: (0, j)),
                  pl.BlockSpec((1, tn), lambda j: (0, j))],
        out_specs=pl.BlockSpec((rows, tn), lambda j: (0, j)),
        out_shape=jax.ShapeDtypeStruct((rows, n), F32),
        compiler_params=pltpu.CompilerParams(
            dimension_semantics=("arbitrary",), vmem_limit_bytes=_vmem_limit(40 << 20)),
        name="modulation",
    )(c_all, mod_w, mod_b)


INPROJ_TM = 256

def _rms_mod(x, g, shift, scale):
    ms = jnp.mean(x * x, axis=-1, keepdims=True)
    h = x * lax.rsqrt(ms + EPS) * g
    return h * (1.0 + scale) + shift


def _inproj_body(x_ref, g_ref, sh_ref, sc_ref, w_ref, *rest, use_rope):
    if use_rope:
        cos_ref, sin_ref, o_ref = rest
    else:
        o_ref, = rest
    h = _rms_mod(x_ref[...], g_ref[...], sh_ref[0], sc_ref[0]).astype(BF16)
    d = h.shape[1]
    for k in range(o_ref.shape[0]):
        res = _dot(h, w_ref[:, k * d:(k + 1) * d])
        if k == SLAB_QK:
            lane = lax.broadcasted_iota(jnp.int32, (res.shape[0], RET_DK), 1)
            lane_low = (lane % 64) < 32
            parts = []
            for gi in range(d // RET_DK):
                part = res[:, gi * RET_DK:(gi + 1) * RET_DK]
                if use_rope:
                    part = _rope(part, cos_ref[...], sin_ref[...], lane_low)
                if gi >= RET_HEADS:
                    part = part * (RET_DK ** -0.5)
                parts.append(part)
            res = jnp.concatenate(parts, axis=-1)
        o_ref[k] = res


def _in_projection(x2, g, shift, scale, w_bf, seq_len, rope=None):
    t, d = x2.shape
    n = w_bf.shape[1]
    tm = min(INPROJ_TM, seq_len)
    per_batch = seq_len // tm
    if shift.shape[0] == 1:
        mod_map = lambda i: (0, 0, 0)
    else:
        mod_map = lambda i: (i // per_batch, 0, 0)
    vmem = 2 * (d * n * 2 + (n // d) * tm * d * 4 + tm * d * 4) + (4 << 20)
    in_specs = [pl.BlockSpec((tm, d), lambda i: (i, 0)),
                pl.BlockSpec((1, d), lambda i: (0, 0)),
                pl.BlockSpec((1, 1, d), mod_map),
                pl.BlockSpec((1, 1, d), mod_map),
                pl.BlockSpec((d, n), lambda i: (0, 0))]
    args = [x2, g, shift, scale, w_bf]
    if rope is not None:
        in_specs += [pl.BlockSpec((tm, RET_DK), lambda i: (i % per_batch, 0))] * 2
        args += list(rope)
    return pl.pallas_call(
        functools.partial(_inproj_body, use_rope=rope is not None),
        grid=(t // tm,),
        in_specs=in_specs,
        out_specs=pl.BlockSpec((n // d, tm, d), lambda i: (0, i, 0)),
        out_shape=jax.ShapeDtypeStruct((n // d, t, d), F32),
        compiler_params=pltpu.CompilerParams(
            dimension_semantics=("parallel",), vmem_limit_bytes=_vmem_limit(vmem)),
        name="in_projection",
    )(*args)


def _rope_tables(n_tok):
    quarter = RET_DK // 4
    pos = np.arange(n_tok)
    row = (pos // GRID_W).astype(np.float32)
    col = (pos % GRID_W).astype(np.float32)
    inv = np.power(np.float32(ROPE_BASE), -np.arange(quarter, dtype=np.float32) / np.float32(quarter))
    ar = (row[:, None] * inv[None, :]).astype(np.float64)
    ac = (col[:, None] * inv[None, :]).astype(np.float64)
    cos_t = np.concatenate([np.cos(ar), np.cos(ar), np.cos(ac), np.cos(ac)], axis=-1)
    sin_t = np.concatenate([-np.sin(ar), np.sin(ar), -np.sin(ac), np.sin(ac)], axis=-1)
    return jnp.asarray(cos_t, F32), jnp.asarray(sin_t, F32)


def _rope(t, cos_t, sin_t, lane_low):
    swapped = jnp.where(lane_low, pltpu.roll(t, 96, 1), pltpu.roll(t, 32, 1))
    return t * cos_t + swapped * sin_t


def _ret_body(lg_ref, qkf_ref, vf_ref, qkb_ref, vb_ref, s0f_ref, s0b_ref,
              of_ref, ob_ref, sff_ref, sfb_ref, sf_scr, sb_scr, dmat_scr, xi_scr, zeta_scr, cdec_scr, *, n_batch):
    c = pl.program_id(0)
    n = pl.num_programs(0)
    C = RET_CHUNK

    @pl.when(c == 0)
    def _():
        sf_scr[...] = s0f_ref[...]
        sb_scr[...] = s0b_ref[...]
        ii = lax.broadcasted_iota(jnp.int32, (C, C), 0)
        jj = lax.broadcasted_iota(jnp.int32, (C, C), 1)
        col_i = ii.astype(F32)
        for d in range(2):
            for h in range(RET_HEADS):
                lg = lg_ref[d, h]
                if d == 0:
                    diff = (ii - jj).astype(F32)
                    keep = ii >= jj
                    xi = jnp.exp(lg * (col_i + 1.0))
                    zeta = jnp.exp(lg * (C - 1.0 - col_i))
                else:
                    diff = (jj - ii).astype(F32)
                    keep = jj > ii
                    xi = jnp.exp(lg * (C - col_i))
                    zeta = jnp.exp(lg * col_i)
                dmat_scr[d, h] = jnp.where(keep, jnp.exp(lg * jnp.maximum(diff, 0.0)), 0.0)
                xi_scr[d, h] = xi
                zeta_scr[d, h] = zeta
                cdec_scr[d, h] = jnp.exp(jnp.full((SUBLANES, RET_DV), lg * C, F32))

    refs = ((qkf_ref, vf_ref, sf_scr, of_ref), (qkb_ref, vb_ref, sb_scr, ob_ref))
    k_off = RET_HEADS * RET_DK

    def batch_body(b, carry):
        for d in range(2):
            qk_ref, v_ref, s_scr, o_ref = refs[d]
            for h in range(RET_HEADS):
                dmat, xi, zeta = dmat_scr[d, h], xi_scr[d, h], zeta_scr[d, h]
                cdec = cdec_scr[d, h, 0:1, :]
                q = qk_ref[0, b, :, h * RET_DK:(h + 1) * RET_DK]
                k = qk_ref[0, b, :, k_off + h * RET_DK:k_off + (h + 1) * RET_DK]
                v = v_ref[0, b, :, h * RET_DV:(h + 1) * RET_DV].astype(BF16)
                s_old = s_scr[b, h]
                scores = lax.dot_general(q.astype(BF16), k.astype(BF16), (((1,), (1,)), ((), ())),
                                         preferred_element_type=F32) * dmat
                o = _dot(scores.astype(BF16), v) + _dot((q * xi).astype(BF16), s_old.astype(BF16))
                kz_t = (k * zeta).T.astype(BF16)
                s_scr[b, h] = cdec * s_old + _dot(kz_t, v)
                o_ref[b, :, h * RET_DV:(h + 1) * RET_DV] = o
        return carry

    lax.fori_loop(0, n_batch, batch_body, 0)

    @pl.when(c == n - 1)
    def _():
        sff_ref[...] = sf_scr[...]
        sfb_ref[...] = sb_scr[...]


def _retention(p4, lg, s0f, s0b):
    _, bsz, seq, d = p4.shape
    n = seq // RET_CHUNK
    C = RET_CHUNK
    v_w = RET_HEADS * RET_DV
    fwd = lambda c: c
    bwd = lambda c: n - 1 - c

    def slab_spec(slab, order):
        return pl.BlockSpec((1, bsz, C, d), lambda c: (slab, 0, order(c), 0))

    st_spec = pl.BlockSpec((bsz, RET_HEADS, RET_DK, RET_DV), lambda c: (0, 0, 0, 0))
    st_shape = jax.ShapeDtypeStruct((bsz, RET_HEADS, RET_DK, RET_DV), F32)
    o_shape = jax.ShapeDtypeStruct((bsz, seq, v_w), F32)
    body = functools.partial(_ret_body, n_batch=bsz)
    return pl.pallas_call(
        body,
        grid=(n,),
        in_specs=[pl.BlockSpec(memory_space=pltpu.SMEM),
                  slab_spec(SLAB_QK, fwd), slab_spec(SLAB_V, fwd),
                  slab_spec(SLAB_QK, bwd), slab_spec(SLAB_V, bwd),
                  st_spec, st_spec],
        out_specs=[pl.BlockSpec((bsz, C, v_w), lambda c: (0, c, 0)),
                   pl.BlockSpec((bsz, C, v_w), lambda c: (0, n - 1 - c, 0)),
                   st_spec, st_spec],
        out_shape=[o_shape, o_shape, st_shape, st_shape],
        scratch_shapes=[pltpu.VMEM((bsz, RET_HEADS, RET_DK, RET_DV), F32),
                        pltpu.VMEM((bsz, RET_HEADS, RET_DK, RET_DV), F32),
                        pltpu.VMEM((2, RET_HEADS, C, C), F32),
                        pltpu.VMEM((2, RET_HEADS, C, RET_DK), F32),
                        pltpu.VMEM((2, RET_HEADS, C, RET_DK), F32),
                        pltpu.VMEM((2, RET_HEADS, SUBLANES, RET_DV), F32)],
        compiler_params=pltpu.CompilerParams(
            dimension_semantics=("arbitrary",), vmem_limit_bytes=_vmem_limit(52 << 20)),
        name="retention",
    )(lg, p4, p4, p4, p4, s0f, s0b)


LRU_TBLK = 1024

def _lru_body(uc_ref, up_ref, un_ref, cw_ref, cb_ref, wa_ref, ba_ref, wx_ref, bx_ref, lam_ref, h0_ref,
              h_ref, fin_ref, ext_scr, a_scr, b_scr, carry_scr, *, reverse, t_blk):
    tb = pl.program_id(1)
    nt = pl.num_programs(1)
    pos = (nt - 1 - tb) if reverse else tb
    T = t_blk

    @pl.when(tb == 0)
    def _():
        carry_scr[...] = h0_ref[0]

    has_prev = (pos > 0).astype(F32)
    has_next = (pos < nt - 1).astype(F32)
    ext_scr[0:8, :] = up_ref[0, 0] * has_prev
    ext_scr[8:8 + T, :] = uc_ref[0, 0]
    ext_scr[8 + T:16 + T, :] = un_ref[0, 0] * has_next
    full = ext_scr[...]
    rows = full.shape[0]
    u = (cb_ref[...]
         + pltpu.roll(full, 2, 0)[8:8 + T] * cw_ref[0:1, :]
         + pltpu.roll(full, 1, 0)[8:8 + T] * cw_ref[1:2, :]
         + full[8:8 + T] * cw_ref[2:3, :]
         + pltpu.roll(full, rows - 1, 0)[8:8 + T] * cw_ref[3:4, :])
    ub = u.astype(BF16)
    rr = []
    xx = []
    for nb in range(LRU_BLOCKS):
        blk = ub[:, nb * LRU_BS:(nb + 1) * LRU_BS]
        rr.append(_dot(blk, wa_ref[nb]))
        xx.append(_dot(blk, wx_ref[nb]))
    r = jax.nn.sigmoid(jnp.concatenate(rr, axis=-1) + ba_ref[...])
    gate_i = jax.nn.sigmoid(jnp.concatenate(xx, axis=-1) + bx_ref[...])
    z = -lam_ref[...]
    softplus = jnp.maximum(z, 0.0) + jnp.log1p(jnp.exp(-jnp.abs(z)))
    log_a = (-LRU_C) * r * softplus
    a = jnp.exp(log_a)
    a_scr[...] = a
    b_scr[...] = jnp.sqrt(1.0 - a * a) * (gate_i * u)

    row = lax.broadcasted_iota(jnp.int32, (SUBLANES, a_scr.shape[1]), 0)
    n_grp = T // SUBLANES

    def step(i, hprev):
        g = (n_grp - 1 - i) if reverse else i
        r0 = pl.multiple_of(g * SUBLANES, SUBLANES)
        a = a_scr[pl.ds(r0, SUBLANES), :]
        b = b_scr[pl.ds(r0, SUBLANES), :]
        for s in (1, 2, 4):
            if reverse:
                valid = row < SUBLANES - s
                shift = SUBLANES - s
            else:
                valid = row >= s
                shift = s
            a_s = jnp.where(valid, pltpu.roll(a, shift, 0), 1.0)
            b_s = jnp.where(valid, pltpu.roll(b, shift, 0), 0.0)
            b = a * b_s + b
            a = a * a_s
        h = a * hprev + b
        h_ref[0, pl.ds(r0, SUBLANES), :] = h
        return h[0:1, :] if reverse else h[SUBLANES - 1:SUBLANES, :]

    hfin = lax.fori_loop(0, n_grp, step, carry_scr[...], unroll=2)
    carry_scr[...] = hfin
    fin_ref[0] = hfin


def _lru_direction(p4, cw, cb, wa_bf, ba, wx_bf, bx, lam, h0, reverse):
    _, bsz, seq, d = p4.shape
    T = min(LRU_TBLK, seq)
    nt = seq // T
    sl = SLAB_LRUX
    hb = T // SUBLANES
    n_halo = seq // SUBLANES
    order = (lambda tb: nt - 1 - tb) if reverse else (lambda tb: tb)
    body = functools.partial(_lru_body, reverse=reverse, t_blk=T)
    vec = lambda: pl.BlockSpec((1, d), lambda b, tb: (0, 0))
    return pl.pallas_call(
        body,
        grid=(bsz, nt),
        in_specs=[pl.BlockSpec((1, 1, T, d), lambda b, tb: (sl, b, order(tb), 0)),
                  pl.BlockSpec((1, 1, SUBLANES, d), lambda b, tb: (sl, b, jnp.maximum(order(tb) * hb - 1, 0), 0)),
                  pl.BlockSpec((1, 1, SUBLANES, d),
                               lambda b, tb: (sl, b, jnp.minimum((order(tb) + 1) * hb, n_halo - 1), 0)),
                  pl.BlockSpec((CONV_W, d), lambda b, tb: (0, 0)),
                  vec(),
                  pl.BlockSpec((LRU_BLOCKS, LRU_BS, LRU_BS), lambda b, tb: (0, 0, 0)),
                  vec(),
                  pl.BlockSpec((LRU_BLOCKS, LRU_BS, LRU_BS), lambda b, tb: (0, 0, 0)),
                  vec(), vec(),
                  pl.BlockSpec((1, 1, d), lambda b, tb: (b, 0, 0))],
        out_specs=[pl.BlockSpec((1, T, d), lambda b, tb: (b, order(tb), 0)),
                   pl.BlockSpec((1, 1, d), lambda b, tb: (b, 0, 0))],
        out_shape=[jax.ShapeDtypeStruct((bsz, seq, d), F32), jax.ShapeDtypeStruct((bsz, 1, d), F32)],
        scratch_shapes=[pltpu.VMEM((T + 2 * SUBLANES, d), F32), pltpu.VMEM((T, d), F32),
                        pltpu.VMEM((T, d), F32), pltpu.VMEM((1, d), F32)],
        compiler_params=pltpu.CompilerParams(
            dimension_semantics=("parallel", "arbitrary"), vmem_limit_bytes=_vmem_limit(40 << 20)),
        name="rglru_rev" if reverse else "rglru_fwd",
    )(p4, p4, p4, cw, cb, wa_bf, ba, wx_bf, bx, lam, h0)


def _merge_body(x_ref, of_ref, ob_ref, hf_ref, hb_ref, rg_ref, lg_ref, bgr_ref, bgl_ref, g1_ref,
                wr_ref, wl_ref, wo_ref, o_ref):
    o = of_ref[...] + ob_ref[...]
    parts = []
    for h in range(RET_HEADS):
        oh = o[:, h * RET_DV:(h + 1) * RET_DV]
        mu = jnp.mean(oh, axis=-1, keepdims=True)
        cen = oh - mu
        var = jnp.mean(cen * cen, axis=-1, keepdims=True)
        parts.append(cen * lax.rsqrt(var + EPS))
    on = jnp.concatenate(parts, axis=-1)
    rg = rg_ref[0]
    ret = _dot((on * (rg * jax.nn.sigmoid(rg))).astype(BF16), wr_ref[...])
    lru = _dot(((hf_ref[...] + hb_ref[...]) * jax.nn.gelu(lg_ref[0])).astype(BF16), wl_ref[...])
    y = jax.nn.sigmoid(bgr_ref[0]) * ret + jax.nn.sigmoid(bgl_ref[0]) * lru
    o_ref[...] = x_ref[...] + g1_ref[0] * _dot(y.astype(BF16), wo_ref[...])


def _merge(x2, o_f, o_b, h_f, h_b, p2, g1, wr_bf, wl_bf, wo_bf, seq_len):
    t, d = x2.shape
    tm = 512
    per_batch = seq_len // tm
    tok = lambda: pl.BlockSpec((tm, d), lambda i: (i, 0))
    pcol = lambda slab: pl.BlockSpec((1, tm, d), lambda i: (slab, i, 0))
    wspec = lambda: pl.BlockSpec((d, d), lambda i: (0, 0))
    return pl.pallas_call(
        _merge_body,
        grid=(t // tm,),
        in_specs=[tok(), tok(), tok(), tok(), tok(),
                  pcol(SLAB_RGATE), pcol(SLAB_LRUG), pcol(SLAB_BGR), pcol(SLAB_BGL),
                  pl.BlockSpec((1, 1, d), lambda i: (i // per_batch, 0, 0)),
                  wspec(), wspec(), wspec()],
        out_specs=tok(),
        out_shape=jax.ShapeDtypeStruct((t, d), F32),
        compiler_params=pltpu.CompilerParams(
            dimension_semantics=("parallel",), vmem_limit_bytes=_vmem_limit(52 << 20)),
        name="merge",
    )(x2, o_f, o_b, h_f, h_b, p2, p2, p2, p2, g1, wr_bf, wl_bf, wo_bf)


PEER_TM = 512
PEER_EBLK = 2048
PEER_SLABS = PEER_EBLK // PEER_NKEYS
PEER_RB = 256


_GELU_K1 = 2.0 * math.sqrt(2.0 / math.pi) / math.log(2.0)
_GELU_K2 = _GELU_K1 * 0.044715


def _gelu_tanh(x):
    return x / (1.0 + jnp.exp2(x * (-_GELU_K1 - _GELU_K2 * (x * x))))


def _max_first(v, order_groups):
    groups = [(v[k * SUBLANES:(k + 1) * SUBLANES], o) for k, o in enumerate(order_groups)]
    while len(groups) > 1:
        nxt = []
        for a in range(0, len(groups) - 1, 2):
            (va, oa), (vb, ob) = groups[a], groups[a + 1]
            later = vb > va
            nxt.append((jnp.where(later, vb, va), jnp.where(later, ob, oa)))
        if len(groups) % 2:
            nxt.append(groups[-1])
        groups = nxt
    v8, o8 = groups[0]
    m = jnp.max(v8, axis=0, keepdims=True)
    first = jnp.min(jnp.where(v8 == m, o8, 1e9), axis=0, keepdims=True)
    return m, first


def _extract_top(vals, order_groups, n_take, on_take):
    order = jnp.concatenate(order_groups, axis=0)

    def body(a, state):
        v, carry = state
        m, first = _max_first(v, order_groups)
        hit = order == first
        carry = on_take(jnp.asarray(a, F32), m, hit, carry)
        return jnp.where(hit, NEG_INF, v), carry
    return body, n_take


def _topk_keys(scores):
    nk, n = scores[0].shape
    order = lax.broadcasted_iota(jnp.int32, (nk, n), 0).astype(F32)
    row8 = lax.broadcasted_iota(jnp.int32, (SUBLANES, n), 0).astype(F32)
    order_groups = [row8 + float(k * SUBLANES) for k in range(nk // SUBLANES)]
    row16 = lax.broadcasted_iota(jnp.int32, (PEER_TOPK, n), 0).astype(F32)

    def body(a, state):
        take = row16 == jnp.asarray(a, F32)
        out = []
        for v, top, idx in state:
            m, first = _max_first(v, order_groups)
            out.append((jnp.where(order == first, NEG_INF, v), jnp.where(take, m, top),
                        jnp.where(take, first, idx)))
        return tuple(out)

    zeros = jnp.zeros((PEER_TOPK, n), F32)
    res = lax.fori_loop(0, PEER_TOPK, body, tuple((s, zeros, zeros) for s in scores))
    return [(idx, top) for _, top, idx in res]


def _by_rank(idx, values, fill, shape):
    order = lax.broadcasted_iota(jnp.int32, shape, 0).astype(F32)
    out = jnp.full(shape, fill, F32)
    for a in range(PEER_TOPK):
        out = jnp.where(order == idx[a:a + 1], values[a], out)
    return out


def _route_head(s1, s2):
    n = s1.shape[1]
    (i1, t1), = _topk_keys([s1])
    (i2, t2), = _topk_keys([s2])
    row8 = lax.broadcasted_iota(jnp.int32, (SUBLANES, n), 0).astype(F32)
    row16 = lax.broadcasted_iota(jnp.int32, (PEER_TOPK, n), 0).astype(F32)
    cands = [t2 + t1[0:1]]
    flats = [row8, row8 + float(SUBLANES)]
    for a in range(1, 8):
        n_a = PEER_TOPK // (a + 1)
        cands.append(jnp.where(row8 < n_a, t2[0:8] + t1[a:a + 1], NEG_INF))
        flats.append(row8 + float(a * PEER_TOPK))
    cands.append(t1[8:16] + t2[0:1])
    flats.append((row8 + 8.0) * float(PEER_TOPK))
    cand = jnp.concatenate(cands, axis=0)

    def on_take(a, m, hit, sel):
        return jnp.where(hit, 1.0, sel)

    body, n_take = _extract_top(cand, flats, PEER_TOPK, on_take)
    _, sel = lax.fori_loop(0, n_take, body, (cand, jnp.zeros_like(cand)))

    cmax = cand[0:1]
    z = jnp.sum(sel * jnp.exp(cand - cmax), axis=0, keepdims=True)
    l_top = jnp.where(row8 == 0.0, jnp.sum(sel[0:16], axis=0, keepdims=True), 0.0)
    for a in range(1, 8):
        l_a = jnp.sum(sel[8 + 8 * a:16 + 8 * a], axis=0, keepdims=True)
        l_top = jnp.where(row8 == float(a), l_a, l_top)
    l_mat = jnp.concatenate([l_top, sel[72:80]], axis=0)
    l1 = _by_rank(i1, [l_mat[a:a + 1] for a in range(PEER_TOPK)], 0.0, s1.shape)
    r2 = _by_rank(i2, [float(a) for a in range(PEER_TOPK)], float(PEER_TOPK), s2.shape)
    c1 = jnp.exp(s1 - t1[0:1]) * (1.0 / z)
    e2 = jnp.exp(s2 - t2[0:1])
    return r2, e2, l1, c1


def _peer_body(x_ref, g_ref, sh_ref, sc_ref, g2_ref, fg_ref, wqh_ref, wql_ref, kh_ref, kl_ref, u_ref, vt_ref,
               o_ref, h2t_scr, s_scr, r2_scr, e2_scr, l1_scr, c1_scr, a_scr, wf_scr, acc_scr):
    j = pl.program_id(1)
    nj = pl.num_programs(1)
    tm = x_ref.shape[0]

    @pl.when(j == 0)
    def _():
        h2 = _rms_mod(x_ref[...], g_ref[...], sh_ref[0], sc_ref[0])
        h2t = h2.T
        h_hi, h_lo = _split_bf16(h2t)
        h2t_scr[...] = h_hi
        qt = _dot3(wqh_ref[...], wql_ref[...], h_hi, h_lo)
        q_hi, q_lo = _split_bf16(qt)
        for hp in range(2 * PEER_HEADS):
            sl = slice(hp * PEER_DHALF, (hp + 1) * PEER_DHALF)
            s_scr[hp] = _dot3(kh_ref[hp], kl_ref[hp], q_hi[sl], q_lo[sl])
        acc_scr[...] = jnp.zeros_like(acc_scr)

        def head_body(h, carry):
            for pb in range(tm // PEER_RB):
                ps = slice(pb * PEER_RB, (pb + 1) * PEER_RB)
                r2, e2, l1, c1 = _route_head(s_scr[2 * h, :, ps], s_scr[2 * h + 1, :, ps])
                for q in range(PEER_RB // LANES):
                    qs = slice(q * LANES, (q + 1) * LANES)
                    r2_scr[h, pb * (PEER_RB // LANES) + q] = r2[:, qs]
                    e2_scr[h, pb * (PEER_RB // LANES) + q] = e2[:, qs]
                l1_scr[h, :, ps] = l1
                c1_scr[h, :, ps] = c1
            return carry

        lax.fori_loop(0, PEER_HEADS, head_body, 0)

    n_lb = tm // LANES

    res = _dot(u_ref[...], h2t_scr[...])
    for q in range(n_lb):
        a_scr[q] = res[:, q * LANES:(q + 1) * LANES]

    def slab_body(s, carry):
        i1 = j * PEER_SLABS + s
        r0 = pl.multiple_of(s * PEER_NKEYS, PEER_NKEYS)
        lrows = [l1_scr[h, pl.ds(i1, 1), :] for h in range(PEER_HEADS)]
        crows = [c1_scr[h, pl.ds(i1, 1), :] for h in range(PEER_HEADS)]
        for lb in range(n_lb):
            ls = slice(lb * LANES, (lb + 1) * LANES)
            w = None
            for h in range(PEER_HEADS):
                term = jnp.where(r2_scr[h, lb] < lrows[h][:, ls], e2_scr[h, lb], 0.0) * crows[h][:, ls]
                w = term if w is None else w + term
            act = _gelu_tanh(a_scr[lb, pl.ds(r0, PEER_NKEYS), :])
            wf_scr[lb, pl.ds(r0, PEER_NKEYS), :] = (w * act).astype(BF16)
        return carry

    lax.fori_loop(0, PEER_SLABS, slab_body, 0)
    rhs = jnp.concatenate([wf_scr[q] for q in range(n_lb)], axis=1)
    acc_scr[...] += _dot(vt_ref[0], rhs)

    @pl.when(j == nj - 1)
    def _():
        x2 = x_ref[...] + g2_ref[0] * acc_scr[...].T
        ms = jnp.mean(x2 * x2, axis=-1, keepdims=True)
        o_ref[...] = x2 * lax.rsqrt(ms + EPS) * fg_ref[...]


def _peer(x2, g, shift, scale, gate, final_g, wq_hi, wq_lo, k_hi, k_lo, u_bf, vt_bf, seq_len):
    t, d = x2.shape
    n_exp = u_bf.shape[0]
    tm = PEER_TM
    per_batch = seq_len // tm
    mod = lambda: pl.BlockSpec((1, 1, d), lambda i, j: (i // per_batch, 0, 0))
    vec = lambda: pl.BlockSpec((1, d), lambda i, j: (0, 0))
    nq = wq_hi.shape[0]
    tab = lambda: pltpu.VMEM((PEER_HEADS, PEER_NKEYS, tm), F32)
    nblk = n_exp // PEER_EBLK
    return pl.pallas_call(
        _peer_body,
        grid=(t // tm, nblk),
        in_specs=[pl.BlockSpec((tm, d), lambda i, j: (i, 0)),
                  vec(), mod(), mod(), mod(), vec(),
                  pl.BlockSpec((nq, d), lambda i, j: (0, 0)),
                  pl.BlockSpec((nq, d), lambda i, j: (0, 0)),
                  pl.BlockSpec((2 * PEER_HEADS, PEER_NKEYS, PEER_DHALF), lambda i, j: (0, 0, 0)),
                  pl.BlockSpec((2 * PEER_HEADS, PEER_NKEYS, PEER_DHALF), lambda i, j: (0, 0, 0)),
                  pl.BlockSpec((PEER_EBLK, d), lambda i, j: (j, 0)),
                  pl.BlockSpec((1, d, PEER_EBLK), lambda i, j: (j, 0, 0))],
        out_specs=pl.BlockSpec((tm, d), lambda i, j: (i, 0)),
        out_shape=jax.ShapeDtypeStruct((t, d), F32),
        scratch_shapes=[pltpu.VMEM((d, tm), BF16),
                        pltpu.VMEM((2 * PEER_HEADS, PEER_NKEYS, tm), F32),
                        pltpu.VMEM((PEER_HEADS, tm // LANES, PEER_NKEYS, LANES), F32),
                        pltpu.VMEM((PEER_HEADS, tm // LANES, PEER_NKEYS, LANES), F32),
                        tab(), tab(),
                        pltpu.VMEM((tm // LANES, PEER_EBLK, LANES), F32),
                        pltpu.VMEM((tm // LANES, PEER_EBLK, LANES), BF16),
                        pltpu.VMEM((d, tm), F32)],
        compiler_params=pltpu.CompilerParams(
            dimension_semantics=("parallel", "arbitrary"), vmem_limit_bytes=_vmem_limit(58 << 20)),
        name="peer",
    )(x2, g, shift, scale, gate, final_g, wq_hi, wq_lo, k_hi, k_lo, u_bf, vt_bf)


def kernel(x, c, ctx, c_ctx, mod_w, mod_b, norm1_g, norm2_g, w_in, ret_decay, conv_w, conv_b, lru_wa, lru_ba,
           lru_wx, lru_bx, lru_lambda, w_ret_out, w_lru_out, w_out, peer_wq, peer_keys, peer_u, peer_v, final_g):
    bsz, seq, d = x.shape
    ctx_len = ctx.shape[1]
    l = 0

    pad = (-(bsz + 1)) % SUBLANES
    c_all = jnp.concatenate([c, c_ctx[None, :], jnp.zeros((pad, d), F32)], axis=0)
    mod = _modulation(c_all, mod_w[l], mod_b[l][None, :])
    mod_x = mod[:bsz].reshape(bsz, 1, 6, d)
    sh1x, sc1x, g1x, sh2x, sc2x, g2x = [mod_x[:, :, i, :] for i in range(6)]
    mod_c = mod[bsz:bsz + 1].reshape(1, 1, 6, d)
    sh1c, sc1c = mod_c[:, :, 0, :], mod_c[:, :, 1, :]

    w_in_bf = w_in[l].astype(BF16)
    g1 = norm1_g[l][None, :]
    x2 = x.reshape(bsz * seq, d)
    ctx2 = ctx.reshape(bsz * ctx_len, d)
    px = _in_projection(x2, g1, sh1x, sc1x, w_in_bf, seq, rope=_rope_tables(seq))
    pc = _in_projection(ctx2, g1, sh1c, sc1c, w_in_bf, ctx_len)
    px3 = px.reshape(px.shape[0], bsz, seq, d)
    pc3 = pc.reshape(pc.shape[0], bsz, ctx_len, d)

    lg = jax.nn.log_sigmoid(ret_decay[l].astype(F32))
    zeros_s = jnp.zeros((bsz, RET_HEADS, RET_DK, RET_DV), F32)
    _, _, sf, sb = _retention(pc3, lg, zeros_s, zeros_s)
    o_f, o_b, _, _ = _retention(px3, lg, sf, sb)

    cw = conv_w[l]
    cb = conv_b[l][None, :]
    h_dirs = []
    for dr in range(2):
        wa_bf = lru_wa[l, dr].astype(BF16)
        wx_bf = lru_wx[l, dr].astype(BF16)
        ba = lru_ba[l, dr][None, :]
        bx = lru_bx[l, dr][None, :]
        lam = lru_lambda[l, dr][None, :]
        h0 = jnp.zeros((bsz, 1, cw.shape[1]), F32)
        _, fin = _lru_direction(pc3, cw, cb, wa_bf, ba, wx_bf, bx, lam, h0, reverse=bool(dr))
        hx, _ = _lru_direction(px3, cw, cb, wa_bf, ba, wx_bf, bx, lam, fin, reverse=bool(dr))
        h_dirs.append(hx.reshape(bsz * seq, -1))

    x1 = _merge(x2, o_f.reshape(bsz * seq, -1), o_b.reshape(bsz * seq, -1), h_dirs[0], h_dirs[1], px, g1x,
                w_ret_out[l].astype(BF16), w_lru_out[l].astype(BF16), w_out[l].astype(BF16), seq)

    wq_t = peer_wq[l].T
    wq_hi = wq_t.astype(BF16)
    wq_lo = (wq_t - wq_hi.astype(F32)).astype(BF16)
    keys = peer_keys[l].reshape(2 * PEER_HEADS, PEER_NKEYS, PEER_DHALF)
    k_hi = keys.astype(BF16)
    k_lo = (keys - k_hi.astype(F32)).astype(BF16)
    u_bf = peer_u[l].astype(BF16)
    vt_bf = peer_v[l].astype(BF16).reshape(-1, PEER_EBLK, d).transpose(0, 2, 1)
    out = _peer(x1, norm2_g[l][None, :], sh2x, sc2x, g2x, final_g[None, :], wq_hi, wq_lo, k_hi, k_lo,
                u_bf, vt_bf, seq)
    return out.reshape(bsz, seq, d)
```

```python
import functools
import math

import jax
import jax.numpy as jnp
import numpy as np
from jax import lax
from jax.experimental import pallas as pl
from jax.experimental.pallas import tpu as pltpu

F32 = jnp.float32
BF16 = jnp.bfloat16

EPS = 1e-6
GRID_W = 64
ROPE_BASE = 10000.0
RET_HEADS = 4
RET_DK = 128
RET_DV = 256
RET_CHUNK = 128
LRU_BLOCKS = 8
LRU_BS = 128
LRU_C = 8.0
CONV_W = 4
PEER_HEADS = 8
PEER_DHALF = 64
PEER_NKEYS = 128
PEER_TOPK = 16
SLAB_QK, SLAB_V, SLAB_RGATE, SLAB_LRUX, SLAB_LRUG, SLAB_BGR, SLAB_BGL = range(7)

V7X_VMEM_BYTES = 64 * 1024 * 1024
LANES = 128
SUBLANES = 8

NEG_INF = float("-inf")


def _vmem_limit(nbytes):
    return int(min(max(nbytes, 16 * 1024 * 1024), V7X_VMEM_BYTES - 6 * 1024 * 1024))


def _dot(a, b):
    return jnp.dot(a, b, preferred_element_type=F32)


def _split_bf16(a):
    hi = a.astype(BF16)
    lo = (a - hi.astype(F32)).astype(BF16)
    return hi, lo


def _dot3(a_hi, a_lo, b_hi, b_lo):
    return _dot(a_hi, b_hi) + (_dot(a_hi, b_lo) + _dot(a_lo, b_hi))


def _mod_body(c_ref, w_ref, b_ref, o_ref):
    c = c_ref[...]
    sc = c * jax.nn.sigmoid(c)
    c_hi, c_lo = _split_bf16(sc)
    w_hi, w_lo = _split_bf16(w_ref[...])
    o_ref[...] = _dot3(c_hi, c_lo, w_hi, w_lo) + b_ref[...]


def _modulation(c_all, mod_w, mod_b):
    rows, d = c_all.shape
    n = mod_w.shape[1]
    tn = 1536
    return pl.pallas_call(
        _mod_body,
        grid=(n // tn,),
        in_specs=[pl.BlockSpec((rows, d), lambda j: (0, 0)),
                  pl.BlockSpec((d, tn), lambda j: (0, j)),
                  pl.BlockSpec((1, tn), lambda j: (0, j))],
        out_specs=pl.BlockSpec((rows, tn), lambda j: (0, j)),
        out_shape=jax.ShapeDtypeStruct((rows, n), F32),
        compiler_params=pltpu.CompilerParams(
            dimension_semantics=("arbitrary",), vmem_limit_bytes=_vmem_limit(40 << 20)),
        name="modulation",
    )(c_all, mod_w, mod_b)


INPROJ_TM = 256

def _rms_mod(x, g, shift, scale):
    ms = jnp.mean(x * x, axis=-1, keepdims=True)
    h = x * lax.rsqrt(ms + EPS) * g
    return h * (1.0 + scale) + shift


def _inproj_body(x_ref, g_ref, sh_ref, sc_ref, w_ref, *rest, use_rope):
    if use_rope:
        cos_ref, sin_ref, o_ref = rest
    else:
        o_ref, = rest
    h = _rms_mod(x_ref[...], g_ref[...], sh_ref[0], sc_ref[0]).astype(BF16)
    d = h.shape[1]
    for k in range(o_ref.shape[0]):
        res = _dot(h, w_ref[:, k * d:(k + 1) * d])
        if k == SLAB_QK:
            lane = lax.broadcasted_iota(jnp.int32, (res.shape[0], RET_DK), 1)
            lane_low = (lane % 64) < 32
            parts = []
            for gi in range(d // RET_DK):
                part = res[:, gi * RET_DK:(gi + 1) * RET_DK]
                if use_rope:
                    part = _rope(part, cos_ref[...], sin_ref[...], lane_low)
                if gi >= RET_HEADS:
                    part = part * (RET_DK ** -0.5)
                parts.append(part)
            res = jnp.concatenate(parts, axis=-1)
        o_ref[k] = res


def _in_projection(x2, g, shift, scale, w_bf, seq_len, rope=None):
    t, d = x2.shape
    n = w_bf.shape[1]
    tm = min(INPROJ_TM, seq_len)
    per_batch = seq_len // tm
    if shift.shape[0] == 1:
        mod_map = lambda i: (0, 0, 0)
    else:
        mod_map = lambda i: (i // per_batch, 0, 0)
    vmem = 2 * (d * n * 2 + (n // d) * tm * d * 4 + tm * d * 4) + (4 << 20)
    in_specs = [pl.BlockSpec((tm, d), lambda i: (i, 0)),
                pl.BlockSpec((1, d), lambda i: (0, 0)),
                pl.BlockSpec((1, 1, d), mod_map),
                pl.BlockSpec((1, 1, d), mod_map),
                pl.BlockSpec((d, n), lambda i: (0, 0))]
    args = [x2, g, shift, scale, w_bf]
    if rope is not None:
        in_specs += [pl.BlockSpec((tm, RET_DK), lambda i: (i % per_batch, 0))] * 2
        args += list(rope)
    return pl.pallas_call(
        functools.partial(_inproj_body, use_rope=rope is not None),
        grid=(t // tm,),
        in_specs=in_specs,
        out_specs=pl.BlockSpec((n // d, tm, d), lambda i: (0, i, 0)),
        out_shape=jax.ShapeDtypeStruct((n // d, t, d), F32),
        compiler_params=pltpu.CompilerParams(
            dimension_semantics=("parallel",), vmem_limit_bytes=_vmem_limit(vmem)),
        name="in_projection",
    )(*args)


def _rope_tables(n_tok):
    quarter = RET_DK // 4
    pos = np.arange(n_tok)
    row = (pos // GRID_W).astype(np.float32)
    col = (pos % GRID_W).astype(np.float32)
    inv = np.power(np.float32(ROPE_BASE), -np.arange(quarter, dtype=np.float32) / np.float32(quarter))
    ar = (row[:, None] * inv[None, :]).astype(np.float64)
    ac = (col[:, None] * inv[None, :]).astype(np.float64)
    cos_t = np.concatenate([np.cos(ar), np.cos(ar), np.cos(ac), np.cos(ac)], axis=-1)
    sin_t = np.concatenate([-np.sin(ar), np.sin(ar), -np.sin(ac), np.sin(ac)], axis=-1)
    return jnp.asarray(cos_t, F32), jnp.asarray(sin_t, F32)


def _rope(t, cos_t, sin_t, lane_low):
    swapped = jnp.where(lane_low, pltpu.roll(t, 96, 1), pltpu.roll(t, 32, 1))
    return t * cos_t + swapped * sin_t


def _ret_body(lg_ref, qkf_ref, vf_ref, qkb_ref, vb_ref, s0f_ref, s0b_ref,
              of_ref, ob_ref, sff_ref, sfb_ref, sf_scr, sb_scr, dmat_scr, xi_scr, zeta_scr, cdec_scr, *, n_batch):
    c = pl.program_id(0)
    n = pl.num_programs(0)
    C = RET_CHUNK

    @pl.when(c == 0)
    def _():
        sf_scr[...] = s0f_ref[...]
        sb_scr[...] = s0b_ref[...]
        ii = lax.broadcasted_iota(jnp.int32, (C, C), 0)
        jj = lax.broadcasted_iota(jnp.int32, (C, C), 1)
        col_i = ii.astype(F32)
        for d in range(2):
            for h in range(RET_HEADS):
                lg = lg_ref[d, h]
                if d == 0:
                    diff = (ii - jj).astype(F32)
                    keep = ii >= jj
                    xi = jnp.exp(lg * (col_i + 1.0))
                    zeta = jnp.exp(lg * (C - 1.0 - col_i))
                else:
                    diff = (jj - ii).astype(F32)
                    keep = jj > ii
                    xi = jnp.exp(lg * (C - col_i))
                    zeta = jnp.exp(lg * col_i)
                dmat_scr[d, h] = jnp.where(keep, jnp.exp(lg * jnp.maximum(diff, 0.0)), 0.0)
                xi_scr[d, h] = xi
                zeta_scr[d, h] = zeta
                cdec_scr[d, h] = jnp.exp(jnp.full((SUBLANES, RET_DV), lg * C, F32))

    refs = ((qkf_ref, vf_ref, sf_scr, of_ref), (qkb_ref, vb_ref, sb_scr, ob_ref))
    k_off = RET_HEADS * RET_DK

    def batch_body(b, carry):
        for d in range(2):
            qk_ref, v_ref, s_scr, o_ref = refs[d]
            for h in range(RET_HEADS):
                dmat, xi, zeta = dmat_scr[d, h], xi_scr[d, h], zeta_scr[d, h]
                cdec = cdec_scr[d, h, 0:1, :]
                q = qk_ref[0, b, :, h * RET_DK:(h + 1) * RET_DK]
                k = qk_ref[0, b, :, k_off + h * RET_DK:k_off + (h + 1) * RET_DK]
                v = v_ref[0, b, :, h * RET_DV:(h + 1) * RET_DV].astype(BF16)
                s_old = s_scr[b, h]
                scores = lax.dot_general(q.astype(BF16), k.astype(BF16), (((1,), (1,)), ((), ())),
                                         preferred_element_type=F32) * dmat
                o = _dot(scores.astype(BF16), v) + _dot((q * xi).astype(BF16), s_old.astype(BF16))
                kz_t = (k * zeta).T.astype(BF16)
                s_scr[b, h] = cdec * s_old + _dot(kz_t, v)
                o_ref[b, :, h * RET_DV:(h + 1) * RET_DV] = o
        return carry

    lax.fori_loop(0, n_batch, batch_body, 0)

    @pl.when(c == n - 1)
    def _():
        sff_ref[...] = sf_scr[...]
        sfb_ref[...] = sb_scr[...]


def _retention(p4, lg, s0f, s0b):
    _, bsz, seq, d = p4.shape
    n = seq // RET_CHUNK
    C = RET_CHUNK
    v_w = RET_HEADS * RET_DV
    fwd = lambda c: c
    bwd = lambda c: n - 1 - c

    def slab_spec(slab, order):
        return pl.BlockSpec((1, bsz, C, d), lambda c: (slab, 0, order(c), 0))

    st_spec = pl.BlockSpec((bsz, RET_HEADS, RET_DK, RET_DV), lambda c: (0, 0, 0, 0))
    st_shape = jax.ShapeDtypeStruct((bsz, RET_HEADS, RET_DK, RET_DV), F32)
    o_shape = jax.ShapeDtypeStruct((bsz, seq, v_w), F32)
    body = functools.partial(_ret_body, n_batch=bsz)
    return pl.pallas_call(
        body,
        grid=(n,),
        in_specs=[pl.BlockSpec(memory_space=pltpu.SMEM),
                  slab_spec(SLAB_QK, fwd), slab_spec(SLAB_V, fwd),
                  slab_spec(SLAB_QK, bwd), slab_spec(SLAB_V, bwd),
                  st_spec, st_spec],
        out_specs=[pl.BlockSpec((bsz, C, v_w), lambda c: (0, c, 0)),
                   pl.BlockSpec((bsz, C, v_w), lambda c: (0, n - 1 - c, 0)),
                   st_spec, st_spec],
        out_shape=[o_shape, o_shape, st_shape, st_shape],
        scratch_shapes=[pltpu.VMEM((bsz, RET_HEADS, RET_DK, RET_DV), F32),
                        pltpu.VMEM((bsz, RET_HEADS, RET_DK, RET_DV), F32),
                        pltpu.VMEM((2, RET_HEADS, C, C), F32),
                        pltpu.VMEM((2, RET_HEADS, C, RET_DK), F32),
                        pltpu.VMEM((2, RET_HEADS, C, RET_DK), F32),
                        pltpu.VMEM((2, RET_HEADS, SUBLANES, RET_DV), F32)],
        compiler_params=pltpu.CompilerParams(
            dimension_semantics=("arbitrary",), vmem_limit_bytes=_vmem_limit(52 << 20)),
        name="retention",
    )(lg, p4, p4, p4, p4, s0f, s0b)


LRU_TBLK = 1024

def _lru_body(uc_ref, up_ref, un_ref, cw_ref, cb_ref, wa_ref, ba_ref, wx_ref, bx_ref, lam_ref, h0_ref,
              h_ref, fin_ref, ext_scr, a_scr, b_scr, carry_scr, *, reverse, t_blk):
    tb = pl.program_id(1)
    nt = pl.num_programs(1)
    pos = (nt - 1 - tb) if reverse else tb
    T = t_blk

    @pl.when(tb == 0)
    def _():
        carry_scr[...] = h0_ref[0]

    has_prev = (pos > 0).astype(F32)
    has_next = (pos < nt - 1).astype(F32)
    ext_scr[0:8, :] = up_ref[0, 0] * has_prev
    ext_scr[8:8 + T, :] = uc_ref[0, 0]
    ext_scr[8 + T:16 + T, :] = un_ref[0, 0] * has_next
    full = ext_scr[...]
    rows = full.shape[0]
    u = (cb_ref[...]
         + pltpu.roll(full, 2, 0)[8:8 + T] * cw_ref[0:1, :]
         + pltpu.roll(full, 1, 0)[8:8 + T] * cw_ref[1:2, :]
         + full[8:8 + T] * cw_ref[2:3, :]
         + pltpu.roll(full, rows - 1, 0)[8:8 + T] * cw_ref[3:4, :])
    ub = u.astype(BF16)
    rr = []
    xx = []
    for nb in range(LRU_BLOCKS):
        blk = ub[:, nb * LRU_BS:(nb + 1) * LRU_BS]
        rr.append(_dot(blk, wa_ref[nb]))
        xx.append(_dot(blk, wx_ref[nb]))
    r = jax.nn.sigmoid(jnp.concatenate(rr, axis=-1) + ba_ref[...])
    gate_i = jax.nn.sigmoid(jnp.concatenate(xx, axis=-1) + bx_ref[...])
    z = -lam_ref[...]
    softplus = jnp.maximum(z, 0.0) + jnp.log1p(jnp.exp(-jnp.abs(z)))
    log_a = (-LRU_C) * r * softplus
    a = jnp.exp(log_a)
    a_scr[...] = a
    b_scr[...] = jnp.sqrt(1.0 - a * a) * (gate_i * u)

    row = lax.broadcasted_iota(jnp.int32, (SUBLANES, a_scr.shape[1]), 0)
    n_grp = T // SUBLANES

    def step(i, hprev):
        g = (n_grp - 1 - i) if reverse else i
        r0 = pl.multiple_of(g * SUBLANES, SUBLANES)
        a = a_scr[pl.ds(r0, SUBLANES), :]
        b = b_scr[pl.ds(r0, SUBLANES), :]
        for s in (1, 2, 4):
            if reverse:
                valid = row < SUBLANES - s
                shift = SUBLANES - s
            else:
                valid = row >= s
                shift = s
            a_s = jnp.where(valid, pltpu.roll(a, shift, 0), 1.0)
            b_s = jnp.where(valid, pltpu.roll(b, shift, 0), 0.0)
            b = a * b_s + b
            a = a * a_s
        h = a * hprev + b
        h_ref[0, pl.ds(r0, SUBLANES), :] = h
        return h[0:1, :] if reverse else h[SUBLANES - 1:SUBLANES, :]

    hfin = lax.fori_loop(0, n_grp, step, carry_scr[...], unroll=2)
    carry_scr[...] = hfin
    fin_ref[0] = hfin


def _lru_direction(p4, cw, cb, wa_bf, ba, wx_bf, bx, lam, h0, reverse):
    _, bsz, seq, d = p4.shape
    T = min(LRU_TBLK, seq)
    nt = seq // T
    sl = SLAB_LRUX
    hb = T // SUBLANES
    n_halo = seq // SUBLANES
    order = (lambda tb: nt - 1 - tb) if reverse else (lambda tb: tb)
    body = functools.partial(_lru_body, reverse=reverse, t_blk=T)
    vec = lambda: pl.BlockSpec((1, d), lambda b, tb: (0, 0))
    return pl.pallas_call(
        body,
        grid=(bsz, nt),
        in_specs=[pl.BlockSpec((1, 1, T, d), lambda b, tb: (sl, b, order(tb), 0)),
                  pl.BlockSpec((1, 1, SUBLANES, d), lambda b, tb: (sl, b, jnp.maximum(order(tb) * hb - 1, 0), 0)),
                  pl.BlockSpec((1, 1, SUBLANES, d),
                               lambda b, tb: (sl, b, jnp.minimum((order(tb) + 1) * hb, n_halo - 1), 0)),
                  pl.BlockSpec((CONV_W, d), lambda b, tb: (0, 0)),
                  vec(),
                  pl.BlockSpec((LRU_BLOCKS, LRU_BS, LRU_BS), lambda b, tb: (0, 0, 0)),
                  vec(),
                  pl.BlockSpec((LRU_BLOCKS, LRU_BS, LRU_BS), lambda b, tb: (0, 0, 0)),
                  vec(), vec(),
                  pl.BlockSpec((1, 1, d), lambda b, tb: (b, 0, 0))],
        out_specs=[pl.BlockSpec((1, T, d), lambda b, tb: (b, order(tb), 0)),
                   pl.BlockSpec((1, 1, d), lambda b, tb: (b, 0, 0))],
        out_shape=[jax.ShapeDtypeStruct((bsz, seq, d), F32), jax.ShapeDtypeStruct((bsz, 1, d), F32)],
        scratch_shapes=[pltpu.VMEM((T + 2 * SUBLANES, d), F32), pltpu.VMEM((T, d), F32),
                        pltpu.VMEM((T, d), F32), pltpu.VMEM((1, d), F32)],
        compiler_params=pltpu.CompilerParams(
            dimension_semantics=("parallel", "arbitrary"), vmem_limit_bytes=_vmem_limit(40 << 20)),
        name="rglru_rev" if reverse else "rglru_fwd",
    )(p4, p4, p4, cw, cb, wa_bf, ba, wx_bf, bx, lam, h0)


def _merge_body(x_ref, of_ref, ob_ref, hf_ref, hb_ref, rg_ref, lg_ref, bgr_ref, bgl_ref, g1_ref,
                wr_ref, wl_ref, wo_ref, o_ref):
    o = of_ref[...] + ob_ref[...]
    parts = []
    for h in range(RET_HEADS):
        oh = o[:, h * RET_DV:(h + 1) * RET_DV]
        mu = jnp.mean(oh, axis=-1, keepdims=True)
        cen = oh - mu
        var = jnp.mean(cen * cen, axis=-1, keepdims=True)
        parts.append(cen * lax.rsqrt(var + EPS))
    on = jnp.concatenate(parts, axis=-1)
    rg = rg_ref[0]
    ret = _dot((on * (rg * jax.nn.sigmoid(rg))).astype(BF16), wr_ref[...])
    lru = _dot(((hf_ref[...] + hb_ref[...]) * jax.nn.gelu(lg_ref[0])).astype(BF16), wl_ref[...])
    y = jax.nn.sigmoid(bgr_ref[0]) * ret + jax.nn.sigmoid(bgl_ref[0]) * lru
    o_ref[...] = x_ref[...] + g1_ref[0] * _dot(y.astype(BF16), wo_ref[...])


def _merge(x2, o_f, o_b, h_f, h_b, p2, g1, wr_bf, wl_bf, wo_bf, seq_len):
    t, d = x2.shape
    tm = 512
    per_batch = seq_len // tm
    tok = lambda: pl.BlockSpec((tm, d), lambda i: (i, 0))
    pcol = lambda slab: pl.BlockSpec((1, tm, d), lambda i: (slab, i, 0))
    wspec = lambda: pl.BlockSpec((d, d), lambda i: (0, 0))
    return pl.pallas_call(
        _merge_body,
        grid=(t // tm,),
        in_specs=[tok(), tok(), tok(), tok(), tok(),
                  pcol(SLAB_RGATE), pcol(SLAB_LRUG), pcol(SLAB_BGR), pcol(SLAB_BGL),
                  pl.BlockSpec((1, 1, d), lambda i: (i // per_batch, 0, 0)),
                  wspec(), wspec(), wspec()],
        out_specs=tok(),
        out_shape=jax.ShapeDtypeStruct((t, d), F32),
        compiler_params=pltpu.CompilerParams(
            dimension_semantics=("parallel",), vmem_limit_bytes=_vmem_limit(52 << 20)),
        name="merge",
    )(x2, o_f, o_b, h_f, h_b, p2, p2, p2, p2, g1, wr_bf, wl_bf, wo_bf)


PEER_TM = 512
PEER_EBLK = 2048
PEER_SLABS = PEER_EBLK // PEER_NKEYS
PEER_RB = 256
N_CAND_ROWS = 16 + 7 * 8 + 8


_GELU_K1 = 2.0 * math.sqrt(2.0 / math.pi) / math.log(2.0)
_GELU_K2 = _GELU_K1 * 0.044715


def _gelu_tanh(x):
    return x / (1.0 + jnp.exp2(x * (-_GELU_K1 - _GELU_K2 * (x * x))))


def _max_first(v, order_groups):
    groups = [(v[k * SUBLANES:(k + 1) * SUBLANES], o) for k, o in enumerate(order_groups)]
    while len(groups) > 1:
        nxt = []
        for a in range(0, len(groups) - 1, 2):
            (va, oa), (vb, ob) = groups[a], groups[a + 1]
            later = vb > va
            nxt.append((jnp.where(later, vb, va), jnp.where(later, ob, oa)))
        if len(groups) % 2:
            nxt.append(groups[-1])
        groups = nxt
    v8, o8 = groups[0]
    m = jnp.max(v8, axis=0, keepdims=True)
    first = jnp.min(jnp.where(v8 == m, o8, 1e9), axis=0, keepdims=True)
    return m, first


def _topk_keys(work_ref, n_prob):
    _, nk, n = work_ref.shape
    order = lax.broadcasted_iota(jnp.int32, (nk, n), 0).astype(F32)
    row8 = lax.broadcasted_iota(jnp.int32, (SUBLANES, n), 0).astype(F32)
    order_groups = [row8 + float(k * SUBLANES) for k in range(nk // SUBLANES)]
    row16 = lax.broadcasted_iota(jnp.int32, (PEER_TOPK, n), 0).astype(F32)

    def body(a, state):
        take = row16 == jnp.asarray(a, F32)
        out = []
        for p, (top, idx) in enumerate(state):
            v = work_ref[p]
            m, first = _max_first(v, order_groups)
            work_ref[p] = jnp.where(order == first, NEG_INF, v)
            out.append((jnp.where(take, m, top), jnp.where(take, first, idx)))
        return tuple(out)

    zeros = jnp.zeros((PEER_TOPK, n), F32)
    res = lax.fori_loop(0, PEER_TOPK, body, tuple((zeros, zeros) for _ in range(n_prob)))
    return [(idx, top) for top, idx in res]


def _by_rank(idx, values, fill, shape):
    order = lax.broadcasted_iota(jnp.int32, shape, 0).astype(F32)
    out = jnp.full(shape, fill, F32)
    for a in range(PEER_TOPK):
        out = jnp.where(order == idx[a:a + 1], values[a], out)
    return out


def _candidates(t1, t2):
    n = t1.shape[1]
    row8 = lax.broadcasted_iota(jnp.int32, (SUBLANES, n), 0).astype(F32)
    cands = [t2 + t1[0:1]]
    flats = [row8, row8 + float(SUBLANES)]
    for a in range(1, 8):
        n_a = PEER_TOPK // (a + 1)
        cands.append(jnp.where(row8 < n_a, t2[0:8] + t1[a:a + 1], NEG_INF))
        flats.append(row8 + float(a * PEER_TOPK))
    cands.append(t1[8:16] + t2[0:1])
    flats.append((row8 + 8.0) * float(PEER_TOPK))
    return jnp.concatenate(cands, axis=0), flats


def _select_candidates(cand_ref, flats, n_prob):
    flat = jnp.concatenate(flats, axis=0)

    def body(a, carry):
        for p in range(n_prob):
            v = cand_ref[p]
            _, first = _max_first(v, flats)
            cand_ref[p] = jnp.where(flat == first, NEG_INF, v)
        return carry

    lax.fori_loop(0, PEER_TOPK, body, 0)


def _route_head(s1, s2, i1, t1, i2, t2, cand_left):
    n = s1.shape[1]
    row8 = lax.broadcasted_iota(jnp.int32, (SUBLANES, n), 0).astype(F32)
    cand, _ = _candidates(t1, t2)
    sel = jnp.where(cand_left != cand, 1.0, 0.0)
    cmax = cand[0:1]
    z = jnp.sum(sel * jnp.exp(cand - cmax), axis=0, keepdims=True)
    l_top = jnp.where(row8 == 0.0, jnp.sum(sel[0:16], axis=0, keepdims=True), 0.0)
    for a in range(1, 8):
        l_a = jnp.sum(sel[8 + 8 * a:16 + 8 * a], axis=0, keepdims=True)
        l_top = jnp.where(row8 == float(a), l_a, l_top)
    l_mat = jnp.concatenate([l_top, sel[72:80]], axis=0)
    l1 = _by_rank(i1, [l_mat[a:a + 1] for a in range(PEER_TOPK)], 0.0, s1.shape)
    r2 = _by_rank(i2, [float(a) for a in range(PEER_TOPK)], float(PEER_TOPK), s2.shape)
    c1 = jnp.exp(s1 - t1[0:1]) * (1.0 / z)
    e2 = jnp.exp(s2 - t2[0:1])
    return r2, e2, l1, c1


def _peer_body(x_ref, g_ref, sh_ref, sc_ref, g2_ref, fg_ref, wqh_ref, wql_ref, kh_ref, kl_ref, u_ref, vt_ref,
               o_ref, h2t_scr, s_scr, work_scr, cand_scr, r2_scr, e2_scr, l1_scr, c1_scr, a_scr, wf_scr, acc_scr):
    j = pl.program_id(1)
    nj = pl.num_programs(1)
    tm = x_ref.shape[0]

    @pl.when(j == 0)
    def _():
        h2 = _rms_mod(x_ref[...], g_ref[...], sh_ref[0], sc_ref[0])
        h2t = h2.T
        h_hi, h_lo = _split_bf16(h2t)
        h2t_scr[...] = h_hi
        qt = _dot3(wqh_ref[...], wql_ref[...], h_hi, h_lo)
        q_hi, q_lo = _split_bf16(qt)
        for hp in range(2 * PEER_HEADS):
            sl = slice(hp * PEER_DHALF, (hp + 1) * PEER_DHALF)
            s_scr[hp] = _dot3(kh_ref[hp], kl_ref[hp], q_hi[sl], q_lo[sl])
        acc_scr[...] = jnp.zeros_like(acc_scr)

        def head_body(h, carry):
            n_pb = tm // PEER_RB
            for pb in range(n_pb):
                ps = slice(pb * PEER_RB, (pb + 1) * PEER_RB)
                work_scr[2 * pb] = s_scr[2 * h, :, ps]
                work_scr[2 * pb + 1] = s_scr[2 * h + 1, :, ps]
            tops = _topk_keys(work_scr, 2 * n_pb)
            for pb in range(n_pb):
                cand_scr[pb], flats = _candidates(tops[2 * pb][1], tops[2 * pb + 1][1])
            _select_candidates(cand_scr, flats, n_pb)
            for pb in range(n_pb):
                ps = slice(pb * PEER_RB, (pb + 1) * PEER_RB)
                (i1, t1), (i2, t2) = tops[2 * pb], tops[2 * pb + 1]
                r2, e2, l1, c1 = _route_head(s_scr[2 * h, :, ps], s_scr[2 * h + 1, :, ps], i1, t1, i2, t2,
                                             cand_scr[pb])
                for q in range(PEER_RB // LANES):
                    qs = slice(q * LANES, (q + 1) * LANES)
                    r2_scr[h, pb * (PEER_RB // LANES) + q] = r2[:, qs]
                    e2_scr[h, pb * (PEER_RB // LANES) + q] = e2[:, qs]
                l1_scr[h, :, ps] = l1
                c1_scr[h, :, ps] = c1
            return carry

        lax.fori_loop(0, PEER_HEADS, head_body, 0)

    n_lb = tm // LANES

    res = _dot(u_ref[...], h2t_scr[...])
    for q in range(n_lb):
        a_scr[q] = res[:, q * LANES:(q + 1) * LANES]

    def slab_body(s, carry):
        i1 = j * PEER_SLABS + s
        r0 = pl.multiple_of(s * PEER_NKEYS, PEER_NKEYS)
        lrows = [l1_scr[h, pl.ds(i1, 1), :] for h in range(PEER_HEADS)]
        crows = [c1_scr[h, pl.ds(i1, 1), :] for h in range(PEER_HEADS)]
        for lb in range(n_lb):
            ls = slice(lb * LANES, (lb + 1) * LANES)
            w = None
            for h in range(PEER_HEADS):
                term = jnp.where(r2_scr[h, lb] < lrows[h][:, ls], e2_scr[h, lb], 0.0) * crows[h][:, ls]
                w = term if w is None else w + term
            act = _gelu_tanh(a_scr[lb, pl.ds(r0, PEER_NKEYS), :])
            wf_scr[lb, pl.ds(r0, PEER_NKEYS), :] = (w * act).astype(BF16)
        return carry

    lax.fori_loop(0, PEER_SLABS, slab_body, 0)
    rhs = jnp.concatenate([wf_scr[q] for q in range(n_lb)], axis=1)
    acc_scr[...] += _dot(vt_ref[0], rhs)

    @pl.when(j == nj - 1)
    def _():
        x2 = x_ref[...] + g2_ref[0] * acc_scr[...].T
        ms = jnp.mean(x2 * x2, axis=-1, keepdims=True)
        o_ref[...] = x2 * lax.rsqrt(ms + EPS) * fg_ref[...]


def _peer(x2, g, shift, scale, gate, final_g, wq_hi, wq_lo, k_hi, k_lo, u_bf, vt_bf, seq_len):
    t, d = x2.shape
    n_exp = u_bf.shape[0]
    tm = PEER_TM
    per_batch = seq_len // tm
    mod = lambda: pl.BlockSpec((1, 1, d), lambda i, j: (i // per_batch, 0, 0))
    vec = lambda: pl.BlockSpec((1, d), lambda i, j: (0, 0))
    nq = wq_hi.shape[0]
    tab = lambda: pltpu.VMEM((PEER_HEADS, PEER_NKEYS, tm), F32)
    nblk = n_exp // PEER_EBLK
    return pl.pallas_call(
        _peer_body,
        grid=(t // tm, nblk),
        in_specs=[pl.BlockSpec((tm, d), lambda i, j: (i, 0), pipeline_mode=pl.Buffered(1)),
                  vec(), mod(), mod(), mod(), vec(),
                  pl.BlockSpec((nq, d), lambda i, j: (0, 0)),
                  pl.BlockSpec((nq, d), lambda i, j: (0, 0)),
                  pl.BlockSpec((2 * PEER_HEADS, PEER_NKEYS, PEER_DHALF), lambda i, j: (0, 0, 0)),
                  pl.BlockSpec((2 * PEER_HEADS, PEER_NKEYS, PEER_DHALF), lambda i, j: (0, 0, 0)),
                  pl.BlockSpec((PEER_EBLK, d), lambda i, j: (j, 0)),
                  pl.BlockSpec((1, d, PEER_EBLK), lambda i, j: (j, 0, 0))],
        out_specs=pl.BlockSpec((tm, d), lambda i, j: (i, 0)),
        out_shape=jax.ShapeDtypeStruct((t, d), F32),
        scratch_shapes=[pltpu.VMEM((d, tm), BF16),
                        pltpu.VMEM((2 * PEER_HEADS, PEER_NKEYS, tm), F32),
                        pltpu.VMEM((2 * (tm // PEER_RB), PEER_NKEYS, PEER_RB), F32),
                        pltpu.VMEM((tm // PEER_RB, N_CAND_ROWS, PEER_RB), F32),
                        pltpu.VMEM((PEER_HEADS, tm // LANES, PEER_NKEYS, LANES), F32),
                        pltpu.VMEM((PEER_HEADS, tm // LANES, PEER_NKEYS, LANES), F32),
                        tab(), tab(),
                        pltpu.VMEM((tm // LANES, PEER_EBLK, LANES), F32),
                        pltpu.VMEM((tm // LANES, PEER_EBLK, LANES), BF16),
                        pltpu.VMEM((d, tm), F32)],
        compiler_params=pltpu.CompilerParams(
            dimension_semantics=("parallel", "arbitrary"), vmem_limit_bytes=_vmem_limit(58 << 20)),
        name="peer",
    )(x2, g, shift, scale, gate, final_g, wq_hi, wq_lo, k_hi, k_lo, u_bf, vt_bf)


def kernel(x, c, ctx, c_ctx, mod_w, mod_b, norm1_g, norm2_g, w_in, ret_decay, conv_w, conv_b, lru_wa, lru_ba,
           lru_wx, lru_bx, lru_lambda, w_ret_out, w_lru_out, w_out, peer_wq, peer_keys, peer_u, peer_v, final_g):
    bsz, seq, d = x.shape
    ctx_len = ctx.shape[1]
    l = 0

    pad = (-(bsz + 1)) % SUBLANES
    c_all = jnp.concatenate([c, c_ctx[None, :], jnp.zeros((pad, d), F32)], axis=0)
    mod = _modulation(c_all, mod_w[l], mod_b[l][None, :])
    mod_x = mod[:bsz].reshape(bsz, 1, 6, d)
    sh1x, sc1x, g1x, sh2x, sc2x, g2x = [mod_x[:, :, i, :] for i in range(6)]
    mod_c = mod[bsz:bsz + 1].reshape(1, 1, 6, d)
    sh1c, sc1c = mod_c[:, :, 0, :], mod_c[:, :, 1, :]

    w_in_bf = w_in[l].astype(BF16)
    g1 = norm1_g[l][None, :]
    x2 = x.reshape(bsz * seq, d)
    ctx2 = ctx.reshape(bsz * ctx_len, d)
    px = _in_projection(x2, g1, sh1x, sc1x, w_in_bf, seq, rope=_rope_tables(seq))
    pc = _in_projection(ctx2, g1, sh1c, sc1c, w_in_bf, ctx_len)
    px3 = px.reshape(px.shape[0], bsz, seq, d)
    pc3 = pc.reshape(pc.shape[0], bsz, ctx_len, d)

    lg = jax.nn.log_sigmoid(ret_decay[l].astype(F32))
    zeros_s = jnp.zeros((bsz, RET_HEADS, RET_DK, RET_DV), F32)
    _, _, sf, sb = _retention(pc3, lg, zeros_s, zeros_s)
    o_f, o_b, _, _ = _retention(px3, lg, sf, sb)

    cw = conv_w[l]
    cb = conv_b[l][None, :]
    h_dirs = []
    for dr in range(2):
        wa_bf = lru_wa[l, dr].astype(BF16)
        wx_bf = lru_wx[l, dr].astype(BF16)
        ba = lru_ba[l, dr][None, :]
        bx = lru_bx[l, dr][None, :]
        lam = lru_lambda[l, dr][None, :]
        h0 = jnp.zeros((bsz, 1, cw.shape[1]), F32)
        _, fin = _lru_direction(pc3, cw, cb, wa_bf, ba, wx_bf, bx, lam, h0, reverse=bool(dr))
        hx, _ = _lru_direction(px3, cw, cb, wa_bf, ba, wx_bf, bx, lam, fin, reverse=bool(dr))
        h_dirs.append(hx.reshape(bsz * seq, -1))

    x1 = _merge(x2, o_f.reshape(bsz * seq, -1), o_b.reshape(bsz * seq, -1), h_dirs[0], h_dirs[1], px, g1x,
                w_ret_out[l].astype(BF16), w_lru_out[l].astype(BF16), w_out[l].astype(BF16), seq)

    wq_t = peer_wq[l].T
    wq_hi = wq_t.astype(BF16)
    wq_lo = (wq_t - wq_hi.astype(F32)).astype(BF16)
    keys = peer_keys[l].reshape(2 * PEER_HEADS, PEER_NKEYS, PEER_DHALF)
    k_hi = keys.astype(BF16)
    k_lo = (keys - k_hi.astype(F32)).astype(BF16)
    u_bf = peer_u[l].astype(BF16)
    vt_bf = peer_v[l].astype(BF16).reshape(-1, PEER_EBLK, d).transpose(0, 2, 1)
    out = _peer(x1, norm2_g[l][None, :], sh2x, sc2x, g2x, final_g[None, :], wq_hi, wq_lo, k_hi, k_lo,
                u_bf, vt_bf, seq)
    return out.reshape(bsz, seq, d)
```

```python
import functools
import math

import jax
import jax.numpy as jnp
import numpy as np
from jax import lax
from jax.experimental import pallas as pl
from jax.experimental.pallas import tpu as pltpu

F32 = jnp.float32
BF16 = jnp.bfloat16

EPS = 1e-6
GRID_W = 64
ROPE_BASE = 10000.0
RET_HEADS = 4
RET_DK = 128
RET_DV = 256
RET_CHUNK = 128
LRU_BLOCKS = 8
LRU_BS = 128
LRU_C = 8.0
CONV_W = 4
PEER_HEADS = 8
PEER_DHALF = 64
PEER_NKEYS = 128
PEER_TOPK = 16
SLAB_QK, SLAB_V, SLAB_RGATE, SLAB_LRUX, SLAB_LRUG, SLAB_BGR, SLAB_BGL = range(7)

V7X_VMEM_BYTES = 64 * 1024 * 1024
LANES = 128
SUBLANES = 8

NEG_INF = float("-inf")


def _vmem_limit(nbytes):
    return int(min(max(nbytes, 16 * 1024 * 1024), V7X_VMEM_BYTES - 6 * 1024 * 1024))


def _dot(a, b):
    return jnp.dot(a, b, preferred_element_type=F32)


def _split_bf16(a):
    hi = a.astype(BF16)
    lo = (a - hi.astype(F32)).astype(BF16)
    return hi, lo


def _dot3(a_hi, a_lo, b_hi, b_lo):
    return _dot(a_hi, b_hi) + (_dot(a_hi, b_lo) + _dot(a_lo, b_hi))


def _mod_body(c_ref, w_ref, b_ref, o_ref):
    c = c_ref[...]
    sc = c * jax.nn.sigmoid(c)
    c_hi, c_lo = _split_bf16(sc)
    w_hi, w_lo = _split_bf16(w_ref[...])
    o_ref[...] = _dot3(c_hi, c_lo, w_hi, w_lo) + b_ref[...]


def _modulation(c_all, mod_w, mod_b):
    rows, d = c_all.shape
    n = mod_w.shape[1]
    tn = 1536
    return pl.pallas_call(
        _mod_body,
        grid=(n // tn,),
        in_specs=[pl.BlockSpec((rows, d), lambda j: (0, 0)),
                  pl.BlockSpec((d, tn), lambda j: (0, j)),
                  pl.BlockSpec((1, tn), lambda j: (0, j))],
        out_specs=pl.BlockSpec((rows, tn), lambda j: (0, j)),
        out_shape=jax.ShapeDtypeStruct((rows, n), F32),
        compiler_params=pltpu.CompilerParams(
            dimension_semantics=("arbitrary",), vmem_limit_bytes=_vmem_limit(40 << 20)),
        name="modulation",
    )(c_all, mod_w, mod_b)


INPROJ_TM = 256

def _rms_mod(x, g, shift, scale):
    ms = jnp.mean(x * x, axis=-1, keepdims=True)
    h = x * lax.rsqrt(ms + EPS) * g
    return h * (1.0 + scale) + shift


def _inproj_body(x_ref, g_ref, sh_ref, sc_ref, w_ref, *rest, use_rope):
    if use_rope:
        cos_ref, sin_ref, o_ref = rest
    else:
        o_ref, = rest
    h = _rms_mod(x_ref[...], g_ref[...], sh_ref[0], sc_ref[0]).astype(BF16)
    d = h.shape[1]
    for k in range(o_ref.shape[0]):
        res = _dot(h, w_ref[:, k * d:(k + 1) * d])
        if k == SLAB_QK:
            lane = lax.broadcasted_iota(jnp.int32, (res.shape[0], RET_DK), 1)
            lane_low = (lane % 64) < 32
            parts = []
            for gi in range(d // RET_DK):
                part = res[:, gi * RET_DK:(gi + 1) * RET_DK]
                if use_rope:
                    part = _rope(part, cos_ref[...], sin_ref[...], lane_low)
                if gi >= RET_HEADS:
                    part = part * (RET_DK ** -0.5)
                parts.append(part)
            res = jnp.concatenate(parts, axis=-1)
        o_ref[k] = res


def _in_projection(x2, g, shift, scale, w_bf, seq_len, rope=None):
    t, d = x2.shape
    n = w_bf.shape[1]
    tm = min(INPROJ_TM, seq_len)
    per_batch = seq_len // tm
    if shift.shape[0] == 1:
        mod_map = lambda i: (0, 0, 0)
    else:
        mod_map = lambda i: (i // per_batch, 0, 0)
    vmem = 2 * (d * n * 2 + (n // d) * tm * d * 4 + tm * d * 4) + (4 << 20)
    in_specs = [pl.BlockSpec((tm, d), lambda i: (i, 0)),
                pl.BlockSpec((1, d), lambda i: (0, 0)),
                pl.BlockSpec((1, 1, d), mod_map),
                pl.BlockSpec((1, 1, d), mod_map),
                pl.BlockSpec((d, n), lambda i: (0, 0))]
    args = [x2, g, shift, scale, w_bf]
    if rope is not None:
        in_specs += [pl.BlockSpec((tm, RET_DK), lambda i: (i % per_batch, 0))] * 2
        args += list(rope)
    return pl.pallas_call(
        functools.partial(_inproj_body, use_rope=rope is not None),
        grid=(t // tm,),
        in_specs=in_specs,
        out_specs=pl.BlockSpec((n // d, tm, d), lambda i: (0, i, 0)),
        out_shape=jax.ShapeDtypeStruct((n // d, t, d), F32),
        compiler_params=pltpu.CompilerParams(
            dimension_semantics=("parallel",), vmem_limit_bytes=_vmem_limit(vmem)),
        name="in_projection",
    )(*args)


def _rope_tables(n_tok):
    quarter = RET_DK // 4
    pos = np.arange(n_tok)
    row = (pos // GRID_W).astype(np.float32)
    col = (pos % GRID_W).astype(np.float32)
    inv = np.power(np.float32(ROPE_BASE), -np.arange(quarter, dtype=np.float32) / np.float32(quarter))
    ar = (row[:, None] * inv[None, :]).astype(np.float64)
    ac = (col[:, None] * inv[None, :]).astype(np.float64)
    cos_t = np.concatenate([np.cos(ar), np.cos(ar), np.cos(ac), np.cos(ac)], axis=-1)
    sin_t = np.concatenate([-np.sin(ar), np.sin(ar), -np.sin(ac), np.sin(ac)], axis=-1)
    return jnp.asarray(cos_t, F32), jnp.asarray(sin_t, F32)


def _rope(t, cos_t, sin_t, lane_low):
    swapped = jnp.where(lane_low, pltpu.roll(t, 96, 1), pltpu.roll(t, 32, 1))
    return t * cos_t + swapped * sin_t


def _ret_body(lg_ref, qkf_ref, vf_ref, qkb_ref, vb_ref, s0f_ref, s0b_ref,
              of_ref, ob_ref, sff_ref, sfb_ref, sf_scr, sb_scr, dmat_scr, xi_scr, zeta_scr, cdec_scr, *, n_batch):
    c = pl.program_id(0)
    n = pl.num_programs(0)
    C = RET_CHUNK

    @pl.when(c == 0)
    def _():
        sf_scr[...] = s0f_ref[...]
        sb_scr[...] = s0b_ref[...]
        ii = lax.broadcasted_iota(jnp.int32, (C, C), 0)
        jj = lax.broadcasted_iota(jnp.int32, (C, C), 1)
        col_i = ii.astype(F32)
        for d in range(2):
            for h in range(RET_HEADS):
                lg = lg_ref[d, h]
                if d == 0:
                    diff = (ii - jj).astype(F32)
                    keep = ii >= jj
                    xi = jnp.exp(lg * (col_i + 1.0))
                    zeta = jnp.exp(lg * (C - 1.0 - col_i))
                else:
                    diff = (jj - ii).astype(F32)
                    keep = jj > ii
                    xi = jnp.exp(lg * (C - col_i))
                    zeta = jnp.exp(lg * col_i)
                dmat_scr[d, h] = jnp.where(keep, jnp.exp(lg * jnp.maximum(diff, 0.0)), 0.0)
                xi_scr[d, h] = xi
                zeta_scr[d, h] = zeta
                cdec_scr[d, h] = jnp.exp(jnp.full((SUBLANES, RET_DV), lg * C, F32))

    refs = ((qkf_ref, vf_ref, sf_scr, of_ref), (qkb_ref, vb_ref, sb_scr, ob_ref))
    k_off = RET_HEADS * RET_DK

    def batch_body(b, carry):
        for d in range(2):
            qk_ref, v_ref, s_scr, o_ref = refs[d]
            for h in range(RET_HEADS):
                dmat, xi, zeta = dmat_scr[d, h], xi_scr[d, h], zeta_scr[d, h]
                cdec = cdec_scr[d, h, 0:1, :]
                q = qk_ref[0, b, :, h * RET_DK:(h + 1) * RET_DK]
                k = qk_ref[0, b, :, k_off + h * RET_DK:k_off + (h + 1) * RET_DK]
                v = v_ref[0, b, :, h * RET_DV:(h + 1) * RET_DV].astype(BF16)
                s_old = s_scr[b, h]
                scores = lax.dot_general(q.astype(BF16), k.astype(BF16), (((1,), (1,)), ((), ())),
                                         preferred_element_type=F32) * dmat
                o = _dot(scores.astype(BF16), v) + _dot((q * xi).astype(BF16), s_old.astype(BF16))
                kz_t = (k * zeta).T.astype(BF16)
                s_scr[b, h] = cdec * s_old + _dot(kz_t, v)
                o_ref[b, :, h * RET_DV:(h + 1) * RET_DV] = o
        return carry

    lax.fori_loop(0, n_batch, batch_body, 0)

    @pl.when(c == n - 1)
    def _():
        sff_ref[...] = sf_scr[...]
        sfb_ref[...] = sb_scr[...]


def _retention(p4, lg, s0f, s0b):
    _, bsz, seq, d = p4.shape
    n = seq // RET_CHUNK
    C = RET_CHUNK
    v_w = RET_HEADS * RET_DV
    fwd = lambda c: c
    bwd = lambda c: n - 1 - c

    def slab_spec(slab, order):
        return pl.BlockSpec((1, bsz, C, d), lambda c: (slab, 0, order(c), 0))

    st_spec = pl.BlockSpec((bsz, RET_HEADS, RET_DK, RET_DV), lambda c: (0, 0, 0, 0))
    st_shape = jax.ShapeDtypeStruct((bsz, RET_HEADS, RET_DK, RET_DV), F32)
    o_shape = jax.ShapeDtypeStruct((bsz, seq, v_w), F32)
    body = functools.partial(_ret_body, n_batch=bsz)
    return pl.pallas_call(
        body,
        grid=(n,),
        in_specs=[pl.BlockSpec(memory_space=pltpu.SMEM),
                  slab_spec(SLAB_QK, fwd), slab_spec(SLAB_V, fwd),
                  slab_spec(SLAB_QK, bwd), slab_spec(SLAB_V, bwd),
                  st_spec, st_spec],
        out_specs=[pl.BlockSpec((bsz, C, v_w), lambda c: (0, c, 0)),
                   pl.BlockSpec((bsz, C, v_w), lambda c: (0, n - 1 - c, 0)),
                   st_spec, st_spec],
        out_shape=[o_shape, o_shape, st_shape, st_shape],
        scratch_shapes=[pltpu.VMEM((bsz, RET_HEADS, RET_DK, RET_DV), F32),
                        pltpu.VMEM((bsz, RET_HEADS, RET_DK, RET_DV), F32),
                        pltpu.VMEM((2, RET_HEADS, C, C), F32),
                        pltpu.VMEM((2, RET_HEADS, C, RET_DK), F32),
                        pltpu.VMEM((2, RET_HEADS, C, RET_DK), F32),
                        pltpu.VMEM((2, RET_HEADS, SUBLANES, RET_DV), F32)],
        compiler_params=pltpu.CompilerParams(
            dimension_semantics=("arbitrary",), vmem_limit_bytes=_vmem_limit(52 << 20)),
        name="retention",
    )(lg, p4, p4, p4, p4, s0f, s0b)


LRU_TBLK = 1024

def _lru_body(uc_ref, up_ref, un_ref, cw_ref, cb_ref, wa_ref, ba_ref, wx_ref, bx_ref, lam_ref, h0_ref,
              h_ref, fin_ref, ext_scr, a_scr, b_scr, carry_scr, *, reverse, t_blk):
    tb = pl.program_id(1)
    nt = pl.num_programs(1)
    pos = (nt - 1 - tb) if reverse else tb
    T = t_blk

    @pl.when(tb == 0)
    def _():
        carry_scr[...] = h0_ref[0]

    has_prev = (pos > 0).astype(F32)
    has_next = (pos < nt - 1).astype(F32)
    ext_scr[0:8, :] = up_ref[0, 0] * has_prev
    ext_scr[8:8 + T, :] = uc_ref[0, 0]
    ext_scr[8 + T:16 + T, :] = un_ref[0, 0] * has_next
    full = ext_scr[...]
    rows = full.shape[0]
    u = (cb_ref[...]
         + pltpu.roll(full, 2, 0)[8:8 + T] * cw_ref[0:1, :]
         + pltpu.roll(full, 1, 0)[8:8 + T] * cw_ref[1:2, :]
         + full[8:8 + T] * cw_ref[2:3, :]
         + pltpu.roll(full, rows - 1, 0)[8:8 + T] * cw_ref[3:4, :])
    ub = u.astype(BF16)
    rr = []
    xx = []
    for nb in range(LRU_BLOCKS):
        blk = ub[:, nb * LRU_BS:(nb + 1) * LRU_BS]
        rr.append(_dot(blk, wa_ref[nb]))
        xx.append(_dot(blk, wx_ref[nb]))
    r = jax.nn.sigmoid(jnp.concatenate(rr, axis=-1) + ba_ref[...])
    gate_i = jax.nn.sigmoid(jnp.concatenate(xx, axis=-1) + bx_ref[...])
    z = -lam_ref[...]
    softplus = jnp.maximum(z, 0.0) + jnp.log1p(jnp.exp(-jnp.abs(z)))
    log_a = (-LRU_C) * r * softplus
    a = jnp.exp(log_a)
    a_scr[...] = a
    b_scr[...] = jnp.sqrt(1.0 - a * a) * (gate_i * u)

    row = lax.broadcasted_iota(jnp.int32, (SUBLANES, a_scr.shape[1]), 0)
    n_grp = T // SUBLANES

    def step(i, hprev):
        g = (n_grp - 1 - i) if reverse else i
        r0 = pl.multiple_of(g * SUBLANES, SUBLANES)
        a = a_scr[pl.ds(r0, SUBLANES), :]
        b = b_scr[pl.ds(r0, SUBLANES), :]
        for s in (1, 2, 4):
            if reverse:
                valid = row < SUBLANES - s
                shift = SUBLANES - s
            else:
                valid = row >= s
                shift = s
            a_s = jnp.where(valid, pltpu.roll(a, shift, 0), 1.0)
            b_s = jnp.where(valid, pltpu.roll(b, shift, 0), 0.0)
            b = a * b_s + b
            a = a * a_s
        h = a * hprev + b
        h_ref[0, pl.ds(r0, SUBLANES), :] = h
        return h[0:1, :] if reverse else h[SUBLANES - 1:SUBLANES, :]

    hfin = lax.fori_loop(0, n_grp, step, carry_scr[...], unroll=2)
    carry_scr[...] = hfin
    fin_ref[0] = hfin


def _lru_direction(p4, cw, cb, wa_bf, ba, wx_bf, bx, lam, h0, reverse):
    _, bsz, seq, d = p4.shape
    T = min(LRU_TBLK, seq)
    nt = seq // T
    sl = SLAB_LRUX
    hb = T // SUBLANES
    n_halo = seq // SUBLANES
    order = (lambda tb: nt - 1 - tb) if reverse else (lambda tb: tb)
    body = functools.partial(_lru_body, reverse=reverse, t_blk=T)
    vec = lambda: pl.BlockSpec((1, d), lambda b, tb: (0, 0))
    return pl.pallas_call(
        body,
        grid=(bsz, nt),
        in_specs=[pl.BlockSpec((1, 1, T, d), lambda b, tb: (sl, b, order(tb), 0)),
                  pl.BlockSpec((1, 1, SUBLANES, d), lambda b, tb: (sl, b, jnp.maximum(order(tb) * hb - 1, 0), 0)),
                  pl.BlockSpec((1, 1, SUBLANES, d),
                               lambda b, tb: (sl, b, jnp.minimum((order(tb) + 1) * hb, n_halo - 1), 0)),
                  pl.BlockSpec((CONV_W, d), lambda b, tb: (0, 0)),
                  vec(),
                  pl.BlockSpec((LRU_BLOCKS, LRU_BS, LRU_BS), lambda b, tb: (0, 0, 0)),
                  vec(),
                  pl.BlockSpec((LRU_BLOCKS, LRU_BS, LRU_BS), lambda b, tb: (0, 0, 0)),
                  vec(), vec(),
                  pl.BlockSpec((1, 1, d), lambda b, tb: (b, 0, 0))],
        out_specs=[pl.BlockSpec((1, T, d), lambda b, tb: (b, order(tb), 0)),
                   pl.BlockSpec((1, 1, d), lambda b, tb: (b, 0, 0))],
        out_shape=[jax.ShapeDtypeStruct((bsz, seq, d), F32), jax.ShapeDtypeStruct((bsz, 1, d), F32)],
        scratch_shapes=[pltpu.VMEM((T + 2 * SUBLANES, d), F32), pltpu.VMEM((T, d), F32),
                        pltpu.VMEM((T, d), F32), pltpu.VMEM((1, d), F32)],
        compiler_params=pltpu.CompilerParams(
            dimension_semantics=("parallel", "arbitrary"), vmem_limit_bytes=_vmem_limit(40 << 20)),
        name="rglru_rev" if reverse else "rglru_fwd",
    )(p4, p4, p4, cw, cb, wa_bf, ba, wx_bf, bx, lam, h0)


def _merge_body(x_ref, of_ref, ob_ref, hf_ref, hb_ref, rg_ref, lg_ref, bgr_ref, bgl_ref, g1_ref,
                wr_ref, wl_ref, wo_ref, o_ref):
    o = of_ref[...] + ob_ref[...]
    parts = []
    for h in range(RET_HEADS):
        oh = o[:, h * RET_DV:(h + 1) * RET_DV]
        mu = jnp.mean(oh, axis=-1, keepdims=True)
        cen = oh - mu
        var = jnp.mean(cen * cen, axis=-1, keepdims=True)
        parts.append(cen * lax.rsqrt(var + EPS))
    on = jnp.concatenate(parts, axis=-1)
    rg = rg_ref[0]
    ret = _dot((on * (rg * jax.nn.sigmoid(rg))).astype(BF16), wr_ref[...])
    lru = _dot(((hf_ref[...] + hb_ref[...]) * jax.nn.gelu(lg_ref[0])).astype(BF16), wl_ref[...])
    y = jax.nn.sigmoid(bgr_ref[0]) * ret + jax.nn.sigmoid(bgl_ref[0]) * lru
    o_ref[...] = x_ref[...] + g1_ref[0] * _dot(y.astype(BF16), wo_ref[...])


def _merge(x2, o_f, o_b, h_f, h_b, p2, g1, wr_bf, wl_bf, wo_bf, seq_len):
    t, d = x2.shape
    tm = 512
    per_batch = seq_len // tm
    tok = lambda: pl.BlockSpec((tm, d), lambda i: (i, 0))
    pcol = lambda slab: pl.BlockSpec((1, tm, d), lambda i: (slab, i, 0))
    wspec = lambda: pl.BlockSpec((d, d), lambda i: (0, 0))
    return pl.pallas_call(
        _merge_body,
        grid=(t // tm,),
        in_specs=[tok(), tok(), tok(), tok(), tok(),
                  pcol(SLAB_RGATE), pcol(SLAB_LRUG), pcol(SLAB_BGR), pcol(SLAB_BGL),
                  pl.BlockSpec((1, 1, d), lambda i: (i // per_batch, 0, 0)),
                  wspec(), wspec(), wspec()],
        out_specs=tok(),
        out_shape=jax.ShapeDtypeStruct((t, d), F32),
        compiler_params=pltpu.CompilerParams(
            dimension_semantics=("parallel",), vmem_limit_bytes=_vmem_limit(52 << 20)),
        name="merge",
    )(x2, o_f, o_b, h_f, h_b, p2, p2, p2, p2, g1, wr_bf, wl_bf, wo_bf)


PEER_TM = 512
PEER_EBLK = 2048
PEER_SLABS = PEER_EBLK // PEER_NKEYS
PEER_RB = 256
N_CAND_ROWS = 16 + 7 * 8 + 8


_GELU_K1 = 2.0 * math.sqrt(2.0 / math.pi) / math.log(2.0)
_GELU_K2 = _GELU_K1 * 0.044715


def _gelu_tanh(x):
    return x / (1.0 + jnp.exp2(x * (-_GELU_K1 - _GELU_K2 * (x * x))))


def _max_first(v, order_groups):
    groups = [(v[k * SUBLANES:(k + 1) * SUBLANES], o) for k, o in enumerate(order_groups)]
    while len(groups) > 1:
        nxt = []
        for a in range(0, len(groups) - 1, 2):
            (va, oa), (vb, ob) = groups[a], groups[a + 1]
            later = vb > va
            nxt.append((jnp.where(later, vb, va), jnp.where(later, ob, oa)))
        if len(groups) % 2:
            nxt.append(groups[-1])
        groups = nxt
    v8, o8 = groups[0]
    m = jnp.max(v8, axis=0, keepdims=True)
    first = jnp.min(jnp.where(v8 == m, o8, 1e9), axis=0, keepdims=True)
    return m, first


def _topk_keys(work_ref, top_ref, idx_ref, n_prob):
    _, nk, n = work_ref.shape
    order = lax.broadcasted_iota(jnp.int32, (nk, n), 0).astype(F32)
    row8 = lax.broadcasted_iota(jnp.int32, (SUBLANES, n), 0).astype(F32)
    order_groups = [row8 + float(k * SUBLANES) for k in range(nk // SUBLANES)]

    def body(a, carry):
        for p in range(n_prob):
            v = work_ref[p]
            m, first = _max_first(v, order_groups)
            work_ref[p] = jnp.where(order == first, NEG_INF, v)
            top_ref[p, pl.ds(a, 1), :] = m
            idx_ref[p, pl.ds(a, 1), :] = first
        return carry

    lax.fori_loop(0, PEER_TOPK, body, 0)
    return [(idx_ref[p], top_ref[p]) for p in range(n_prob)]


def _by_rank(idx, values, fill, shape):
    order = lax.broadcasted_iota(jnp.int32, shape, 0).astype(F32)
    out = jnp.full(shape, fill, F32)
    for a in range(PEER_TOPK):
        out = jnp.where(order == idx[a:a + 1], values[a], out)
    return out


def _candidates(t1, t2):
    n = t1.shape[1]
    row8 = lax.broadcasted_iota(jnp.int32, (SUBLANES, n), 0).astype(F32)
    cands = [t2 + t1[0:1]]
    flats = [row8, row8 + float(SUBLANES)]
    for a in range(1, 8):
        n_a = PEER_TOPK // (a + 1)
        cands.append(jnp.where(row8 < n_a, t2[0:8] + t1[a:a + 1], NEG_INF))
        flats.append(row8 + float(a * PEER_TOPK))
    cands.append(t1[8:16] + t2[0:1])
    flats.append((row8 + 8.0) * float(PEER_TOPK))
    return jnp.concatenate(cands, axis=0), flats


def _select_candidates(cand_ref, flats, n_prob):
    flat = jnp.concatenate(flats, axis=0)

    def body(a, carry):
        for p in range(n_prob):
            v = cand_ref[p]
            _, first = _max_first(v, flats)
            cand_ref[p] = jnp.where(flat == first, NEG_INF, v)
        return carry

    lax.fori_loop(0, PEER_TOPK, body, 0)


def _route_head(s1, s2, i1, t1, i2, t2, cand_left):
    n = s1.shape[1]
    row8 = lax.broadcasted_iota(jnp.int32, (SUBLANES, n), 0).astype(F32)
    cand, _ = _candidates(t1, t2)
    sel = jnp.where(cand_left != cand, 1.0, 0.0)
    cmax = cand[0:1]
    z = jnp.sum(sel * jnp.exp(cand - cmax), axis=0, keepdims=True)
    l_top = jnp.where(row8 == 0.0, jnp.sum(sel[0:16], axis=0, keepdims=True), 0.0)
    for a in range(1, 8):
        l_a = jnp.sum(sel[8 + 8 * a:16 + 8 * a], axis=0, keepdims=True)
        l_top = jnp.where(row8 == float(a), l_a, l_top)
    l_mat = jnp.concatenate([l_top, sel[72:80]], axis=0)
    l1 = _by_rank(i1, [l_mat[a:a + 1] for a in range(PEER_TOPK)], 0.0, s1.shape)
    r2 = _by_rank(i2, [float(a) for a in range(PEER_TOPK)], float(PEER_TOPK), s2.shape)
    c1 = jnp.exp(s1 - t1[0:1]) * (1.0 / z)
    e2 = jnp.exp(s2 - t2[0:1])
    return r2, e2, l1, c1


def _peer_body(x_ref, g_ref, sh_ref, sc_ref, g2_ref, fg_ref, wqh_ref, wql_ref, kh_ref, kl_ref, u_ref, vt_ref,
               o_ref, h2t_scr, s_scr, work_scr, top_scr, idx_scr, cand_scr, r2_scr, e2_scr, l1_scr, c1_scr,
               a_scr, wf_scr, acc_scr):
    j = pl.program_id(1)
    nj = pl.num_programs(1)
    tm = x_ref.shape[0]

    @pl.when(j == 0)
    def _():
        h2 = _rms_mod(x_ref[...], g_ref[...], sh_ref[0], sc_ref[0])
        h2t = h2.T
        h_hi, h_lo = _split_bf16(h2t)
        h2t_scr[...] = h_hi
        qt = _dot3(wqh_ref[...], wql_ref[...], h_hi, h_lo)
        q_hi, q_lo = _split_bf16(qt)
        for hp in range(2 * PEER_HEADS):
            sl = slice(hp * PEER_DHALF, (hp + 1) * PEER_DHALF)
            s_scr[hp] = _dot3(kh_ref[hp], kl_ref[hp], q_hi[sl], q_lo[sl])
        acc_scr[...] = jnp.zeros_like(acc_scr)

        def head_body(h, carry):
            n_pb = tm // PEER_RB
            for pb in range(n_pb):
                ps = slice(pb * PEER_RB, (pb + 1) * PEER_RB)
                work_scr[2 * pb] = s_scr[2 * h, :, ps]
                work_scr[2 * pb + 1] = s_scr[2 * h + 1, :, ps]
            tops = _topk_keys(work_scr, top_scr, idx_scr, 2 * n_pb)
            for pb in range(n_pb):
                cand_scr[pb], flats = _candidates(tops[2 * pb][1], tops[2 * pb + 1][1])
            _select_candidates(cand_scr, flats, n_pb)
            for pb in range(n_pb):
                ps = slice(pb * PEER_RB, (pb + 1) * PEER_RB)
                (i1, t1), (i2, t2) = tops[2 * pb], tops[2 * pb + 1]
                r2, e2, l1, c1 = _route_head(s_scr[2 * h, :, ps], s_scr[2 * h + 1, :, ps], i1, t1, i2, t2,
                                             cand_scr[pb])
                for q in range(PEER_RB // LANES):
                    qs = slice(q * LANES, (q + 1) * LANES)
                    r2_scr[h, pb * (PEER_RB // LANES) + q] = r2[:, qs]
                    e2_scr[h, pb * (PEER_RB // LANES) + q] = e2[:, qs]
                l1_scr[h, :, ps] = l1
                c1_scr[h, :, ps] = c1
            return carry

        lax.fori_loop(0, PEER_HEADS, head_body, 0)

    n_lb = tm // LANES

    res = _dot(u_ref[...], h2t_scr[...])
    for q in range(n_lb):
        a_scr[q] = res[:, q * LANES:(q + 1) * LANES]

    def slab_body(s, carry):
        i1 = j * PEER_SLABS + s
        r0 = pl.multiple_of(s * PEER_NKEYS, PEER_NKEYS)
        lrows = [l1_scr[h, pl.ds(i1, 1), :] for h in range(PEER_HEADS)]
        crows = [c1_scr[h, pl.ds(i1, 1), :] for h in range(PEER_HEADS)]
        for lb in range(n_lb):
            ls = slice(lb * LANES, (lb + 1) * LANES)
            w = None
            for h in range(PEER_HEADS):
                term = jnp.where(r2_scr[h, lb] < lrows[h][:, ls], e2_scr[h, lb], 0.0) * crows[h][:, ls]
                w = term if w is None else w + term
            act = _gelu_tanh(a_scr[lb, pl.ds(r0, PEER_NKEYS), :])
            wf_scr[lb, pl.ds(r0, PEER_NKEYS), :] = (w * act).astype(BF16)
        return carry

    lax.fori_loop(0, PEER_SLABS, slab_body, 0)
    rhs = jnp.concatenate([wf_scr[q] for q in range(n_lb)], axis=1)
    acc_scr[...] += _dot(vt_ref[0], rhs)

    @pl.when(j == nj - 1)
    def _():
        x2 = x_ref[...] + g2_ref[0] * acc_scr[...].T
        ms = jnp.mean(x2 * x2, axis=-1, keepdims=True)
        o_ref[...] = x2 * lax.rsqrt(ms + EPS) * fg_ref[...]


def _peer(x2, g, shift, scale, gate, final_g, wq_hi, wq_lo, k_hi, k_lo, u_bf, vt_bf, seq_len):
    t, d = x2.shape
    n_exp = u_bf.shape[0]
    tm = PEER_TM
    per_batch = seq_len // tm
    mod = lambda: pl.BlockSpec((1, 1, d), lambda i, j: (i // per_batch, 0, 0))
    vec = lambda: pl.BlockSpec((1, d), lambda i, j: (0, 0))
    nq = wq_hi.shape[0]
    tab = lambda: pltpu.VMEM((PEER_HEADS, PEER_NKEYS, tm), F32)
    nblk = n_exp // PEER_EBLK
    return pl.pallas_call(
        _peer_body,
        grid=(t // tm, nblk),
        in_specs=[pl.BlockSpec((tm, d), lambda i, j: (i, 0), pipeline_mode=pl.Buffered(1)),
                  vec(), mod(), mod(), mod(), vec(),
                  pl.BlockSpec((nq, d), lambda i, j: (0, 0)),
                  pl.BlockSpec((nq, d), lambda i, j: (0, 0)),
                  pl.BlockSpec((2 * PEER_HEADS, PEER_NKEYS, PEER_DHALF), lambda i, j: (0, 0, 0)),
                  pl.BlockSpec((2 * PEER_HEADS, PEER_NKEYS, PEER_DHALF), lambda i, j: (0, 0, 0)),
                  pl.BlockSpec((PEER_EBLK, d), lambda i, j: (j, 0)),
                  pl.BlockSpec((1, d, PEER_EBLK), lambda i, j: (j, 0, 0))],
        out_specs=pl.BlockSpec((tm, d), lambda i, j: (i, 0)),
        out_shape=jax.ShapeDtypeStruct((t, d), F32),
        scratch_shapes=[pltpu.VMEM((d, tm), BF16),
                        pltpu.VMEM((2 * PEER_HEADS, PEER_NKEYS, tm), F32),
                        pltpu.VMEM((2 * (tm // PEER_RB), PEER_NKEYS, PEER_RB), F32),
                        pltpu.VMEM((2 * (tm // PEER_RB), PEER_TOPK, PEER_RB), F32),
                        pltpu.VMEM((2 * (tm // PEER_RB), PEER_TOPK, PEER_RB), F32),
                        pltpu.VMEM((tm // PEER_RB, N_CAND_ROWS, PEER_RB), F32),
                        pltpu.VMEM((PEER_HEADS, tm // LANES, PEER_NKEYS, LANES), F32),
                        pltpu.VMEM((PEER_HEADS, tm // LANES, PEER_NKEYS, LANES), F32),
                        tab(), tab(),
                        pltpu.VMEM((tm // LANES, PEER_EBLK, LANES), F32),
                        pltpu.VMEM((tm // LANES, PEER_EBLK, LANES), BF16),
                        pltpu.VMEM((d, tm), F32)],
        compiler_params=pltpu.CompilerParams(
            dimension_semantics=("parallel", "arbitrary"), vmem_limit_bytes=_vmem_limit(58 << 20)),
        name="peer",
    )(x2, g, shift, scale, gate, final_g, wq_hi, wq_lo, k_hi, k_lo, u_bf, vt_bf)


def kernel(x, c, ctx, c_ctx, mod_w, mod_b, norm1_g, norm2_g, w_in, ret_decay, conv_w, conv_b, lru_wa, lru_ba,
           lru_wx, lru_bx, lru_lambda, w_ret_out, w_lru_out, w_out, peer_wq, peer_keys, peer_u, peer_v, final_g):
    bsz, seq, d = x.shape
    ctx_len = ctx.shape[1]
    l = 0

    pad = (-(bsz + 1)) % SUBLANES
    c_all = jnp.concatenate([c, c_ctx[None, :], jnp.zeros((pad, d), F32)], axis=0)
    mod = _modulation(c_all, mod_w[l], mod_b[l][None, :])
    mod_x = mod[:bsz].reshape(bsz, 1, 6, d)
    sh1x, sc1x, g1x, sh2x, sc2x, g2x = [mod_x[:, :, i, :] for i in range(6)]
    mod_c = mod[bsz:bsz + 1].reshape(1, 1, 6, d)
    sh1c, sc1c = mod_c[:, :, 0, :], mod_c[:, :, 1, :]

    w_in_bf = w_in[l].astype(BF16)
    g1 = norm1_g[l][None, :]
    x2 = x.reshape(bsz * seq, d)
    ctx2 = ctx.reshape(bsz * ctx_len, d)
    px = _in_projection(x2, g1, sh1x, sc1x, w_in_bf, seq, rope=_rope_tables(seq))
    pc = _in_projection(ctx2, g1, sh1c, sc1c, w_in_bf, ctx_len)
    px3 = px.reshape(px.shape[0], bsz, seq, d)
    pc3 = pc.reshape(pc.shape[0], bsz, ctx_len, d)

    lg = jax.nn.log_sigmoid(ret_decay[l].astype(F32))
    zeros_s = jnp.zeros((bsz, RET_HEADS, RET_DK, RET_DV), F32)
    _, _, sf, sb = _retention(pc3, lg, zeros_s, zeros_s)
    o_f, o_b, _, _ = _retention(px3, lg, sf, sb)

    cw = conv_w[l]
    cb = conv_b[l][None, :]
    h_dirs = []
    for dr in range(2):
        wa_bf = lru_wa[l, dr].astype(BF16)
        wx_bf = lru_wx[l, dr].astype(BF16)
        ba = lru_ba[l, dr][None, :]
        bx = lru_bx[l, dr][None, :]
        lam = lru_lambda[l, dr][None, :]
        h0 = jnp.zeros((bsz, 1, cw.shape[1]), F32)
        _, fin = _lru_direction(pc3, cw, cb, wa_bf, ba, wx_bf, bx, lam, h0, reverse=bool(dr))
        hx, _ = _lru_direction(px3, cw, cb, wa_bf, ba, wx_bf, bx, lam, fin, reverse=bool(dr))
        h_dirs.append(hx.reshape(bsz * seq, -1))

    x1 = _merge(x2, o_f.reshape(bsz * seq, -1), o_b.reshape(bsz * seq, -1), h_dirs[0], h_dirs[1], px, g1x,
                w_ret_out[l].astype(BF16), w_lru_out[l].astype(BF16), w_out[l].astype(BF16), seq)

    wq_t = peer_wq[l].T
    wq_hi = wq_t.astype(BF16)
    wq_lo = (wq_t - wq_hi.astype(F32)).astype(BF16)
    keys = peer_keys[l].reshape(2 * PEER_HEADS, PEER_NKEYS, PEER_DHALF)
    k_hi = keys.astype(BF16)
    k_lo = (keys - k_hi.astype(F32)).astype(BF16)
    u_bf = peer_u[l].astype(BF16)
    vt_bf = peer_v[l].astype(BF16).reshape(-1, PEER_EBLK, d).transpose(0, 2, 1)
    out = _peer(x1, norm2_g[l][None, :], sh2x, sc2x, g2x, final_g[None, :], wq_hi, wq_lo, k_hi, k_lo,
                u_bf, vt_bf, seq)
    return out.reshape(bsz, seq, d)
```

```python
import functools
import math

import jax
import jax.numpy as jnp
import numpy as np
from jax import lax
from jax.experimental import pallas as pl
from jax.experimental.pallas import tpu as pltpu

F32 = jnp.float32
BF16 = jnp.bfloat16

EPS = 1e-6
GRID_W = 64
ROPE_BASE = 10000.0
RET_HEADS = 4
RET_DK = 128
RET_DV = 256
RET_CHUNK = 128
LRU_BLOCKS = 8
LRU_BS = 128
LRU_C = 8.0
CONV_W = 4
PEER_HEADS = 8
PEER_DHALF = 64
PEER_NKEYS = 128
PEER_TOPK = 16
SLAB_QK, SLAB_V, SLAB_RGATE, SLAB_LRUX, SLAB_LRUG, SLAB_BGR, SLAB_BGL = range(7)

V7X_VMEM_BYTES = 64 * 1024 * 1024
LANES = 128
SUBLANES = 8

NEG_INF = float("-inf")


def _vmem_limit(nbytes):
    return int(min(max(nbytes, 16 * 1024 * 1024), V7X_VMEM_BYTES - 6 * 1024 * 1024))


def _dot(a, b):
    return jnp.dot(a, b, preferred_element_type=F32)


def _split_bf16(a):
    hi = a.astype(BF16)
    lo = (a - hi.astype(F32)).astype(BF16)
    return hi, lo


def _dot3(a_hi, a_lo, b_hi, b_lo):
    return _dot(a_hi, b_hi) + (_dot(a_hi, b_lo) + _dot(a_lo, b_hi))


def _mod_body(c_ref, w_ref, b_ref, o_ref):
    c = c_ref[...]
    sc = c * jax.nn.sigmoid(c)
    c_hi, c_lo = _split_bf16(sc)
    w_hi, w_lo = _split_bf16(w_ref[...])
    o_ref[...] = _dot3(c_hi, c_lo, w_hi, w_lo) + b_ref[...]


def _modulation(c_all, mod_w, mod_b):
    rows, d = c_all.shape
    n = mod_w.shape[1]
    tn = 1536
    return pl.pallas_call(
        _mod_body,
        grid=(n // tn,),
        in_specs=[pl.BlockSpec((rows, d), lambda j: (0, 0)),
                  pl.BlockSpec((d, tn), lambda j: (0, j)),
                  pl.BlockSpec((1, tn), lambda j: (0, j))],
        out_specs=pl.BlockSpec((rows, tn), lambda j: (0, j)),
        out_shape=jax.ShapeDtypeStruct((rows, n), F32),
        compiler_params=pltpu.CompilerParams(
            dimension_semantics=("arbitrary",), vmem_limit_bytes=_vmem_limit(40 << 20)),
        name="modulation",
    )(c_all, mod_w, mod_b)


INPROJ_TM = 256

def _rms_mod(x, g, shift, scale):
    ms = jnp.mean(x * x, axis=-1, keepdims=True)
    h = x * lax.rsqrt(ms + EPS) * g
    return h * (1.0 + scale) + shift


def _inproj_body(x_ref, g_ref, sh_ref, sc_ref, w_ref, *rest, use_rope):
    if use_rope:
        cos_ref, sin_ref, o_ref = rest
    else:
        o_ref, = rest
    h = _rms_mod(x_ref[...], g_ref[...], sh_ref[0], sc_ref[0]).astype(BF16)
    d = h.shape[1]
    for k in range(o_ref.shape[0]):
        res = _dot(h, w_ref[:, k * d:(k + 1) * d])
        if k == SLAB_QK:
            lane = lax.broadcasted_iota(jnp.int32, (res.shape[0], RET_DK), 1)
            lane_low = (lane % 64) < 32
            parts = []
            for gi in range(d // RET_DK):
                part = res[:, gi * RET_DK:(gi + 1) * RET_DK]
                if use_rope:
                    part = _rope(part, cos_ref[...], sin_ref[...], lane_low)
                if gi >= RET_HEADS:
                    part = part * (RET_DK ** -0.5)
                parts.append(part)
            res = jnp.concatenate(parts, axis=-1)
        o_ref[k] = res


def _in_projection(x2, g, shift, scale, w_bf, seq_len, rope=None):
    t, d = x2.shape
    n = w_bf.shape[1]
    tm = min(INPROJ_TM, seq_len)
    per_batch = seq_len // tm
    if shift.shape[0] == 1:
        mod_map = lambda i: (0, 0, 0)
    else:
        mod_map = lambda i: (i // per_batch, 0, 0)
    vmem = 2 * (d * n * 2 + (n // d) * tm * d * 4 + tm * d * 4) + (4 << 20)
    in_specs = [pl.BlockSpec((tm, d), lambda i: (i, 0)),
                pl.BlockSpec((1, d), lambda i: (0, 0)),
                pl.BlockSpec((1, 1, d), mod_map),
                pl.BlockSpec((1, 1, d), mod_map),
                pl.BlockSpec((d, n), lambda i: (0, 0))]
    args = [x2, g, shift, scale, w_bf]
    if rope is not None:
        in_specs += [pl.BlockSpec((tm, RET_DK), lambda i: (i % per_batch, 0))] * 2
        args += list(rope)
    return pl.pallas_call(
        functools.partial(_inproj_body, use_rope=rope is not None),
        grid=(t // tm,),
        in_specs=in_specs,
        out_specs=pl.BlockSpec((n // d, tm, d), lambda i: (0, i, 0)),
        out_shape=jax.ShapeDtypeStruct((n // d, t, d), F32),
        compiler_params=pltpu.CompilerParams(
            dimension_semantics=("parallel",), vmem_limit_bytes=_vmem_limit(vmem)),
        name="in_projection",
    )(*args)


def _rope_tables(n_tok):
    quarter = RET_DK // 4
    pos = np.arange(n_tok)
    row = (pos // GRID_W).astype(np.float32)
    col = (pos % GRID_W).astype(np.float32)
    inv = np.power(np.float32(ROPE_BASE), -np.arange(quarter, dtype=np.float32) / np.float32(quarter))
    ar = (row[:, None] * inv[None, :]).astype(np.float64)
    ac = (col[:, None] * inv[None, :]).astype(np.float64)
    cos_t = np.concatenate([np.cos(ar), np.cos(ar), np.cos(ac), np.cos(ac)], axis=-1)
    sin_t = np.concatenate([-np.sin(ar), np.sin(ar), -np.sin(ac), np.sin(ac)], axis=-1)
    return jnp.asarray(cos_t, F32), jnp.asarray(sin_t, F32)


def _rope(t, cos_t, sin_t, lane_low):
    swapped = jnp.where(lane_low, pltpu.roll(t, 96, 1), pltpu.roll(t, 32, 1))
    return t * cos_t + swapped * sin_t


def _ret_body(lg_ref, qkf_ref, vf_ref, qkb_ref, vb_ref, s0f_ref, s0b_ref,
              of_ref, ob_ref, sff_ref, sfb_ref, sf_scr, sb_scr, dmat_scr, xi_scr, zeta_scr, cdec_scr, *, n_batch):
    c = pl.program_id(0)
    n = pl.num_programs(0)
    C = RET_CHUNK

    @pl.when(c == 0)
    def _():
        sf_scr[...] = s0f_ref[...]
        sb_scr[...] = s0b_ref[...]
        ii = lax.broadcasted_iota(jnp.int32, (C, C), 0)
        jj = lax.broadcasted_iota(jnp.int32, (C, C), 1)
        col_i = ii.astype(F32)
        for d in range(2):
            for h in range(RET_HEADS):
                lg = lg_ref[d, h]
                if d == 0:
                    diff = (ii - jj).astype(F32)
                    keep = ii >= jj
                    xi = jnp.exp(lg * (col_i + 1.0))
                    zeta = jnp.exp(lg * (C - 1.0 - col_i))
                else:
                    diff = (jj - ii).astype(F32)
                    keep = jj > ii
                    xi = jnp.exp(lg * (C - col_i))
                    zeta = jnp.exp(lg * col_i)
                dmat_scr[d, h] = jnp.where(keep, jnp.exp(lg * jnp.maximum(diff, 0.0)), 0.0)
                xi_scr[d, h] = xi
                zeta_scr[d, h] = zeta
                cdec_scr[d, h] = jnp.exp(jnp.full((SUBLANES, RET_DV), lg * C, F32))

    refs = ((qkf_ref, vf_ref, sf_scr, of_ref), (qkb_ref, vb_ref, sb_scr, ob_ref))
    k_off = RET_HEADS * RET_DK

    def batch_body(b, carry):
        for d in range(2):
            qk_ref, v_ref, s_scr, o_ref = refs[d]
            for h in range(RET_HEADS):
                dmat, xi, zeta = dmat_scr[d, h], xi_scr[d, h], zeta_scr[d, h]
                cdec = cdec_scr[d, h, 0:1, :]
                q = qk_ref[0, b, :, h * RET_DK:(h + 1) * RET_DK]
                k = qk_ref[0, b, :, k_off + h * RET_DK:k_off + (h + 1) * RET_DK]
                v = v_ref[0, b, :, h * RET_DV:(h + 1) * RET_DV].astype(BF16)
                s_old = s_scr[b, h]
                scores = lax.dot_general(q.astype(BF16), k.astype(BF16), (((1,), (1,)), ((), ())),
                                         preferred_element_type=F32) * dmat
                o = _dot(scores.astype(BF16), v) + _dot((q * xi).astype(BF16), s_old.astype(BF16))
                kz_t = (k * zeta).T.astype(BF16)
                s_scr[b, h] = cdec * s_old + _dot(kz_t, v)
                o_ref[b, :, h * RET_DV:(h + 1) * RET_DV] = o
        return carry

    lax.fori_loop(0, n_batch, batch_body, 0)

    @pl.when(c == n - 1)
    def _():
        sff_ref[...] = sf_scr[...]
        sfb_ref[...] = sb_scr[...]


def _retention(p4, lg, s0f, s0b):
    _, bsz, seq, d = p4.shape
    n = seq // RET_CHUNK
    C = RET_CHUNK
    v_w = RET_HEADS * RET_DV
    fwd = lambda c: c
    bwd = lambda c: n - 1 - c

    def slab_spec(slab, order):
        return pl.BlockSpec((1, bsz, C, d), lambda c: (slab, 0, order(c), 0))

    st_spec = pl.BlockSpec((bsz, RET_HEADS, RET_DK, RET_DV), lambda c: (0, 0, 0, 0))
    st_shape = jax.ShapeDtypeStruct((bsz, RET_HEADS, RET_DK, RET_DV), F32)
    o_shape = jax.ShapeDtypeStruct((bsz, seq, v_w), F32)
    body = functools.partial(_ret_body, n_batch=bsz)
    return pl.pallas_call(
        body,
        grid=(n,),
        in_specs=[pl.BlockSpec(memory_space=pltpu.SMEM),
                  slab_spec(SLAB_QK, fwd), slab_spec(SLAB_V, fwd),
                  slab_spec(SLAB_QK, bwd), slab_spec(SLAB_V, bwd),
                  st_spec, st_spec],
        out_specs=[pl.BlockSpec((bsz, C, v_w), lambda c: (0, c, 0)),
                   pl.BlockSpec((bsz, C, v_w), lambda c: (0, n - 1 - c, 0)),
                   st_spec, st_spec],
        out_shape=[o_shape, o_shape, st_shape, st_shape],
        scratch_shapes=[pltpu.VMEM((bsz, RET_HEADS, RET_DK, RET_DV), F32),
                        pltpu.VMEM((bsz, RET_HEADS, RET_DK, RET_DV), F32),
                        pltpu.VMEM((2, RET_HEADS, C, C), F32),
                        pltpu.VMEM((2, RET_HEADS, C, RET_DK), F32),
                        pltpu.VMEM((2, RET_HEADS, C, RET_DK), F32),
                        pltpu.VMEM((2, RET_HEADS, SUBLANES, RET_DV), F32)],
        compiler_params=pltpu.CompilerParams(
            dimension_semantics=("arbitrary",), vmem_limit_bytes=_vmem_limit(52 << 20)),
        name="retention",
    )(lg, p4, p4, p4, p4, s0f, s0b)


LRU_TBLK = 1024

def _lru_body(uc_ref, up_ref, un_ref, cw_ref, cb_ref, wa_ref, ba_ref, wx_ref, bx_ref, lam_ref, h0_ref,
              h_ref, fin_ref, ext_scr, a_scr, b_scr, carry_scr, *, reverse, t_blk):
    tb = pl.program_id(1)
    nt = pl.num_programs(1)
    pos = (nt - 1 - tb) if reverse else tb
    T = t_blk

    @pl.when(tb == 0)
    def _():
        carry_scr[...] = h0_ref[0]

    has_prev = (pos > 0).astype(F32)
    has_next = (pos < nt - 1).astype(F32)
    ext_scr[0:8, :] = up_ref[0, 0] * has_prev
    ext_scr[8:8 + T, :] = uc_ref[0, 0]
    ext_scr[8 + T:16 + T, :] = un_ref[0, 0] * has_next
    full = ext_scr[...]
    rows = full.shape[0]
    u = (cb_ref[...]
         + pltpu.roll(full, 2, 0)[8:8 + T] * cw_ref[0:1, :]
         + pltpu.roll(full, 1, 0)[8:8 + T] * cw_ref[1:2, :]
         + full[8:8 + T] * cw_ref[2:3, :]
         + pltpu.roll(full, rows - 1, 0)[8:8 + T] * cw_ref[3:4, :])
    ub = u.astype(BF16)
    rr = []
    xx = []
    for nb in range(LRU_BLOCKS):
        blk = ub[:, nb * LRU_BS:(nb + 1) * LRU_BS]
        rr.append(_dot(blk, wa_ref[nb]))
        xx.append(_dot(blk, wx_ref[nb]))
    r = jax.nn.sigmoid(jnp.concatenate(rr, axis=-1) + ba_ref[...])
    gate_i = jax.nn.sigmoid(jnp.concatenate(xx, axis=-1) + bx_ref[...])
    z = -lam_ref[...]
    softplus = jnp.maximum(z, 0.0) + jnp.log1p(jnp.exp(-jnp.abs(z)))
    log_a = (-LRU_C) * r * softplus
    a = jnp.exp(log_a)
    a_scr[...] = a
    b_scr[...] = jnp.sqrt(1.0 - a * a) * (gate_i * u)

    row = lax.broadcasted_iota(jnp.int32, (SUBLANES, a_scr.shape[1]), 0)
    n_grp = T // SUBLANES

    def step(i, hprev):
        g = (n_grp - 1 - i) if reverse else i
        r0 = pl.multiple_of(g * SUBLANES, SUBLANES)
        a = a_scr[pl.ds(r0, SUBLANES), :]
        b = b_scr[pl.ds(r0, SUBLANES), :]
        for s in (1, 2, 4):
            if reverse:
                valid = row < SUBLANES - s
                shift = SUBLANES - s
            else:
                valid = row >= s
                shift = s
            a_s = jnp.where(valid, pltpu.roll(a, shift, 0), 1.0)
            b_s = jnp.where(valid, pltpu.roll(b, shift, 0), 0.0)
            b = a * b_s + b
            a = a * a_s
        h = a * hprev + b
        h_ref[0, pl.ds(r0, SUBLANES), :] = h
        return h[0:1, :] if reverse else h[SUBLANES - 1:SUBLANES, :]

    hfin = lax.fori_loop(0, n_grp, step, carry_scr[...], unroll=2)
    carry_scr[...] = hfin
    fin_ref[0] = hfin


def _lru_direction(p4, cw, cb, wa_bf, ba, wx_bf, bx, lam, h0, reverse):
    _, bsz, seq, d = p4.shape
    T = min(LRU_TBLK, seq)
    nt = seq // T
    sl = SLAB_LRUX
    hb = T // SUBLANES
    n_halo = seq // SUBLANES
    order = (lambda tb: nt - 1 - tb) if reverse else (lambda tb: tb)
    body = functools.partial(_lru_body, reverse=reverse, t_blk=T)
    vec = lambda: pl.BlockSpec((1, d), lambda b, tb: (0, 0))
    return pl.pallas_call(
        body,
        grid=(bsz, nt),
        in_specs=[pl.BlockSpec((1, 1, T, d), lambda b, tb: (sl, b, order(tb), 0)),
                  pl.BlockSpec((1, 1, SUBLANES, d), lambda b, tb: (sl, b, jnp.maximum(order(tb) * hb - 1, 0), 0)),
                  pl.BlockSpec((1, 1, SUBLANES, d),
                               lambda b, tb: (sl, b, jnp.minimum((order(tb) + 1) * hb, n_halo - 1), 0)),
                  pl.BlockSpec((CONV_W, d), lambda b, tb: (0, 0)),
                  vec(),
                  pl.BlockSpec((LRU_BLOCKS, LRU_BS, LRU_BS), lambda b, tb: (0, 0, 0)),
                  vec(),
                  pl.BlockSpec((LRU_BLOCKS, LRU_BS, LRU_BS), lambda b, tb: (0, 0, 0)),
                  vec(), vec(),
                  pl.BlockSpec((1, 1, d), lambda b, tb: (b, 0, 0))],
        out_specs=[pl.BlockSpec((1, T, d), lambda b, tb: (b, order(tb), 0)),
                   pl.BlockSpec((1, 1, d), lambda b, tb: (b, 0, 0))],
        out_shape=[jax.ShapeDtypeStruct((bsz, seq, d), F32), jax.ShapeDtypeStruct((bsz, 1, d), F32)],
        scratch_shapes=[pltpu.VMEM((T + 2 * SUBLANES, d), F32), pltpu.VMEM((T, d), F32),
                        pltpu.VMEM((T, d), F32), pltpu.VMEM((1, d), F32)],
        compiler_params=pltpu.CompilerParams(
            dimension_semantics=("parallel", "arbitrary"), vmem_limit_bytes=_vmem_limit(40 << 20)),
        name="rglru_rev" if reverse else "rglru_fwd",
    )(p4, p4, p4, cw, cb, wa_bf, ba, wx_bf, bx, lam, h0)


def _merge_body(x_ref, of_ref, ob_ref, hf_ref, hb_ref, rg_ref, lg_ref, bgr_ref, bgl_ref, g1_ref,
                wr_ref, wl_ref, wo_ref, o_ref):
    o = of_ref[...] + ob_ref[...]
    parts = []
    for h in range(RET_HEADS):
        oh = o[:, h * RET_DV:(h + 1) * RET_DV]
        mu = jnp.mean(oh, axis=-1, keepdims=True)
        cen = oh - mu
        var = jnp.mean(cen * cen, axis=-1, keepdims=True)
        parts.append(cen * lax.rsqrt(var + EPS))
    on = jnp.concatenate(parts, axis=-1)
    rg = rg_ref[0]
    ret = _dot((on * (rg * jax.nn.sigmoid(rg))).astype(BF16), wr_ref[...])
    lru = _dot(((hf_ref[...] + hb_ref[...]) * jax.nn.gelu(lg_ref[0])).astype(BF16), wl_ref[...])
    y = jax.nn.sigmoid(bgr_ref[0]) * ret + jax.nn.sigmoid(bgl_ref[0]) * lru
    o_ref[...] = x_ref[...] + g1_ref[0] * _dot(y.astype(BF16), wo_ref[...])


def _merge(x2, o_f, o_b, h_f, h_b, p2, g1, wr_bf, wl_bf, wo_bf, seq_len):
    t, d = x2.shape
    tm = 512
    per_batch = seq_len // tm
    tok = lambda: pl.BlockSpec((tm, d), lambda i: (i, 0))
    pcol = lambda slab: pl.BlockSpec((1, tm, d), lambda i: (slab, i, 0))
    wspec = lambda: pl.BlockSpec((d, d), lambda i: (0, 0))
    return pl.pallas_call(
        _merge_body,
        grid=(t // tm,),
        in_specs=[tok(), tok(), tok(), tok(), tok(),
                  pcol(SLAB_RGATE), pcol(SLAB_LRUG), pcol(SLAB_BGR), pcol(SLAB_BGL),
                  pl.BlockSpec((1, 1, d), lambda i: (i // per_batch, 0, 0)),
                  wspec(), wspec(), wspec()],
        out_specs=tok(),
        out_shape=jax.ShapeDtypeStruct((t, d), F32),
        compiler_params=pltpu.CompilerParams(
            dimension_semantics=("parallel",), vmem_limit_bytes=_vmem_limit(52 << 20)),
        name="merge",
    )(x2, o_f, o_b, h_f, h_b, p2, p2, p2, p2, g1, wr_bf, wl_bf, wo_bf)


PEER_TM = 512
PEER_EBLK = 2048
PEER_SLABS = PEER_EBLK // PEER_NKEYS
PEER_RB = 256
PEER_HG = 2
N_CAND_ROWS = 16 + 7 * 8 + 8


_GELU_K1 = 2.0 * math.sqrt(2.0 / math.pi) / math.log(2.0)
_GELU_K2 = _GELU_K1 * 0.044715


def _gelu_tanh(x):
    return x / (1.0 + jnp.exp2(x * (-_GELU_K1 - _GELU_K2 * (x * x))))


def _max_first(v, order_groups):
    groups = [(v[k * SUBLANES:(k + 1) * SUBLANES], o) for k, o in enumerate(order_groups)]
    while len(groups) > 1:
        nxt = []
        for a in range(0, len(groups) - 1, 2):
            (va, oa), (vb, ob) = groups[a], groups[a + 1]
            later = vb > va
            nxt.append((jnp.where(later, vb, va), jnp.where(later, ob, oa)))
        if len(groups) % 2:
            nxt.append(groups[-1])
        groups = nxt
    v8, o8 = groups[0]
    m = jnp.max(v8, axis=0, keepdims=True)
    first = jnp.min(jnp.where(v8 == m, o8, 1e9), axis=0, keepdims=True)
    return m, first


def _topk_keys(work_ref, top_ref, idx_ref, n_prob):
    _, nk, n = work_ref.shape
    order = lax.broadcasted_iota(jnp.int32, (nk, n), 0).astype(F32)
    row8 = lax.broadcasted_iota(jnp.int32, (SUBLANES, n), 0).astype(F32)
    order_groups = [row8 + float(k * SUBLANES) for k in range(nk // SUBLANES)]

    def body(a, carry):
        for p in range(n_prob):
            v = work_ref[p]
            m, first = _max_first(v, order_groups)
            work_ref[p] = jnp.where(order == first, NEG_INF, v)
            top_ref[p, pl.ds(a, 1), :] = m
            idx_ref[p, pl.ds(a, 1), :] = first
        return carry

    lax.fori_loop(0, PEER_TOPK, body, 0)
    return [(idx_ref[p], top_ref[p]) for p in range(n_prob)]


def _by_rank(idx, values, fill, shape):
    order = lax.broadcasted_iota(jnp.int32, shape, 0).astype(F32)
    out = jnp.full(shape, fill, F32)
    for a in range(PEER_TOPK):
        out = jnp.where(order == idx[a:a + 1], values[a], out)
    return out


def _candidates(t1, t2):
    n = t1.shape[1]
    row8 = lax.broadcasted_iota(jnp.int32, (SUBLANES, n), 0).astype(F32)
    cands = [t2 + t1[0:1]]
    flats = [row8, row8 + float(SUBLANES)]
    for a in range(1, 8):
        n_a = PEER_TOPK // (a + 1)
        cands.append(jnp.where(row8 < n_a, t2[0:8] + t1[a:a + 1], NEG_INF))
        flats.append(row8 + float(a * PEER_TOPK))
    cands.append(t1[8:16] + t2[0:1])
    flats.append((row8 + 8.0) * float(PEER_TOPK))
    return jnp.concatenate(cands, axis=0), flats


def _select_candidates(cand_ref, flats, n_prob):
    flat = jnp.concatenate(flats, axis=0)

    def body(a, carry):
        for p in range(n_prob):
            v = cand_ref[p]
            _, first = _max_first(v, flats)
            cand_ref[p] = jnp.where(flat == first, NEG_INF, v)
        return carry

    lax.fori_loop(0, PEER_TOPK, body, 0)


def _route_head(s1, s2, i1, t1, i2, t2, cand_left):
    n = s1.shape[1]
    row8 = lax.broadcasted_iota(jnp.int32, (SUBLANES, n), 0).astype(F32)
    cand, _ = _candidates(t1, t2)
    sel = jnp.where(cand_left != cand, 1.0, 0.0)
    cmax = cand[0:1]
    z = jnp.sum(sel * jnp.exp(cand - cmax), axis=0, keepdims=True)
    l_top = jnp.where(row8 == 0.0, jnp.sum(sel[0:16], axis=0, keepdims=True), 0.0)
    for a in range(1, 8):
        l_a = jnp.sum(sel[8 + 8 * a:16 + 8 * a], axis=0, keepdims=True)
        l_top = jnp.where(row8 == float(a), l_a, l_top)
    l_mat = jnp.concatenate([l_top, sel[72:80]], axis=0)
    l1 = _by_rank(i1, [l_mat[a:a + 1] for a in range(PEER_TOPK)], 0.0, s1.shape)
    r2 = _by_rank(i2, [float(a) for a in range(PEER_TOPK)], float(PEER_TOPK), s2.shape)
    c1 = jnp.exp(s1 - t1[0:1]) * (1.0 / z)
    e2 = jnp.exp(s2 - t2[0:1])
    return r2, e2, l1, c1


def _peer_body(x_ref, g_ref, sh_ref, sc_ref, g2_ref, fg_ref, wqh_ref, wql_ref, kh_ref, kl_ref, u_ref, vt_ref,
               o_ref, h2t_scr, s_scr, work_scr, top_scr, idx_scr, cand_scr, r2_scr, e2_scr, l1_scr, c1_scr,
               a_scr, wf_scr, acc_scr):
    j = pl.program_id(1)
    nj = pl.num_programs(1)
    tm = x_ref.shape[0]

    @pl.when(j == 0)
    def _():
        h2 = _rms_mod(x_ref[...], g_ref[...], sh_ref[0], sc_ref[0])
        h2t = h2.T
        h_hi, h_lo = _split_bf16(h2t)
        h2t_scr[...] = h_hi
        qt = _dot3(wqh_ref[...], wql_ref[...], h_hi, h_lo)
        q_hi, q_lo = _split_bf16(qt)
        for hp in range(2 * PEER_HEADS):
            sl = slice(hp * PEER_DHALF, (hp + 1) * PEER_DHALF)
            s_scr[hp] = _dot3(kh_ref[hp], kl_ref[hp], q_hi[sl], q_lo[sl])
        acc_scr[...] = jnp.zeros_like(acc_scr)

        def head_body(hg, carry):
            n_pb = tm // PEER_RB
            units = [(hg * PEER_HG + hh, pb) for hh in range(PEER_HG) for pb in range(n_pb)]
            for k, (h, pb) in enumerate(units):
                ps = slice(pb * PEER_RB, (pb + 1) * PEER_RB)
                work_scr[2 * k] = s_scr[2 * h, :, ps]
                work_scr[2 * k + 1] = s_scr[2 * h + 1, :, ps]
            tops = _topk_keys(work_scr, top_scr, idx_scr, 2 * len(units))
            for k in range(len(units)):
                cand_scr[k], flats = _candidates(tops[2 * k][1], tops[2 * k + 1][1])
            _select_candidates(cand_scr, flats, len(units))
            for k, (h, pb) in enumerate(units):
                ps = slice(pb * PEER_RB, (pb + 1) * PEER_RB)
                (i1, t1), (i2, t2) = tops[2 * k], tops[2 * k + 1]
                r2, e2, l1, c1 = _route_head(s_scr[2 * h, :, ps], s_scr[2 * h + 1, :, ps], i1, t1, i2, t2,
                                             cand_scr[k])
                for q in range(PEER_RB // LANES):
                    qs = slice(q * LANES, (q + 1) * LANES)
                    r2_scr[h, pb * (PEER_RB // LANES) + q] = r2[:, qs]
                    e2_scr[h, pb * (PEER_RB // LANES) + q] = e2[:, qs]
                l1_scr[h, :, ps] = l1
                c1_scr[h, :, ps] = c1
            return carry

        lax.fori_loop(0, PEER_HEADS // PEER_HG, head_body, 0)

    n_lb = tm // LANES

    res = _dot(u_ref[...], h2t_scr[...])
    for q in range(n_lb):
        a_scr[q] = res[:, q * LANES:(q + 1) * LANES]

    def slab_body(s, carry):
        i1 = j * PEER_SLABS + s
        r0 = pl.multiple_of(s * PEER_NKEYS, PEER_NKEYS)
        lrows = [l1_scr[h, pl.ds(i1, 1), :] for h in range(PEER_HEADS)]
        crows = [c1_scr[h, pl.ds(i1, 1), :] for h in range(PEER_HEADS)]
        for lb in range(n_lb):
            ls = slice(lb * LANES, (lb + 1) * LANES)
            w = None
            for h in range(PEER_HEADS):
                term = jnp.where(r2_scr[h, lb] < lrows[h][:, ls], e2_scr[h, lb], 0.0) * crows[h][:, ls]
                w = term if w is None else w + term
            act = _gelu_tanh(a_scr[lb, pl.ds(r0, PEER_NKEYS), :])
            wf_scr[lb, pl.ds(r0, PEER_NKEYS), :] = (w * act).astype(BF16)
        return carry

    lax.fori_loop(0, PEER_SLABS, slab_body, 0)
    rhs = jnp.concatenate([wf_scr[q] for q in range(n_lb)], axis=1)
    acc_scr[...] += _dot(vt_ref[0], rhs)

    @pl.when(j == nj - 1)
    def _():
        x2 = x_ref[...] + g2_ref[0] * acc_scr[...].T
        ms = jnp.mean(x2 * x2, axis=-1, keepdims=True)
        o_ref[...] = x2 * lax.rsqrt(ms + EPS) * fg_ref[...]


def _peer(x2, g, shift, scale, gate, final_g, wq_hi, wq_lo, k_hi, k_lo, u_bf, vt_bf, seq_len):
    t, d = x2.shape
    n_exp = u_bf.shape[0]
    tm = PEER_TM
    per_batch = seq_len // tm
    mod = lambda: pl.BlockSpec((1, 1, d), lambda i, j: (i // per_batch, 0, 0))
    vec = lambda: pl.BlockSpec((1, d), lambda i, j: (0, 0))
    nq = wq_hi.shape[0]
    tab = lambda: pltpu.VMEM((PEER_HEADS, PEER_NKEYS, tm), F32)
    nblk = n_exp // PEER_EBLK
    return pl.pallas_call(
        _peer_body,
        grid=(t // tm, nblk),
        in_specs=[pl.BlockSpec((tm, d), lambda i, j: (i, 0), pipeline_mode=pl.Buffered(1)),
                  vec(), mod(), mod(), mod(), vec(),
                  pl.BlockSpec((nq, d), lambda i, j: (0, 0)),
                  pl.BlockSpec((nq, d), lambda i, j: (0, 0)),
                  pl.BlockSpec((2 * PEER_HEADS, PEER_NKEYS, PEER_DHALF), lambda i, j: (0, 0, 0)),
                  pl.BlockSpec((2 * PEER_HEADS, PEER_NKEYS, PEER_DHALF), lambda i, j: (0, 0, 0)),
                  pl.BlockSpec((PEER_EBLK, d), lambda i, j: (j, 0)),
                  pl.BlockSpec((1, d, PEER_EBLK), lambda i, j: (j, 0, 0))],
        out_specs=pl.BlockSpec((tm, d), lambda i, j: (i, 0)),
        out_shape=jax.ShapeDtypeStruct((t, d), F32),
        scratch_shapes=[pltpu.VMEM((d, tm), BF16),
                        pltpu.VMEM((2 * PEER_HEADS, PEER_NKEYS, tm), F32),
                        pltpu.VMEM((2 * PEER_HG * (tm // PEER_RB), PEER_NKEYS, PEER_RB), F32),
                        pltpu.VMEM((2 * PEER_HG * (tm // PEER_RB), PEER_TOPK, PEER_RB), F32),
                        pltpu.VMEM((2 * PEER_HG * (tm // PEER_RB), PEER_TOPK, PEER_RB), F32),
                        pltpu.VMEM((PEER_HG * (tm // PEER_RB), N_CAND_ROWS, PEER_RB), F32),
                        pltpu.VMEM((PEER_HEADS, tm // LANES, PEER_NKEYS, LANES), F32),
                        pltpu.VMEM((PEER_HEADS, tm // LANES, PEER_NKEYS, LANES), F32),
                        tab(), tab(),
                        pltpu.VMEM((tm // LANES, PEER_EBLK, LANES), F32),
                        pltpu.VMEM((tm // LANES, PEER_EBLK, LANES), BF16),
                        pltpu.VMEM((d, tm), F32)],
        compiler_params=pltpu.CompilerParams(
            dimension_semantics=("parallel", "arbitrary"), vmem_limit_bytes=_vmem_limit(58 << 20)),
        name="peer",
    )(x2, g, shift, scale, gate, final_g, wq_hi, wq_lo, k_hi, k_lo, u_bf, vt_bf)


def kernel(x, c, ctx, c_ctx, mod_w, mod_b, norm1_g, norm2_g, w_in, ret_decay, conv_w, conv_b, lru_wa, lru_ba,
           lru_wx, lru_bx, lru_lambda, w_ret_out, w_lru_out, w_out, peer_wq, peer_keys, peer_u, peer_v, final_g):
    bsz, seq, d = x.shape
    ctx_len = ctx.shape[1]
    l = 0

    pad = (-(bsz + 1)) % SUBLANES
    c_all = jnp.concatenate([c, c_ctx[None, :], jnp.zeros((pad, d), F32)], axis=0)
    mod = _modulation(c_all, mod_w[l], mod_b[l][None, :])
    mod_x = mod[:bsz].reshape(bsz, 1, 6, d)
    sh1x, sc1x, g1x, sh2x, sc2x, g2x = [mod_x[:, :, i, :] for i in range(6)]
    mod_c = mod[bsz:bsz + 1].reshape(1, 1, 6, d)
    sh1c, sc1c = mod_c[:, :, 0, :], mod_c[:, :, 1, :]

    w_in_bf = w_in[l].astype(BF16)
    g1 = norm1_g[l][None, :]
    x2 = x.reshape(bsz * seq, d)
    ctx2 = ctx.reshape(bsz * ctx_len, d)
    px = _in_projection(x2, g1, sh1x, sc1x, w_in_bf, seq, rope=_rope_tables(seq))
    pc = _in_projection(ctx2, g1, sh1c, sc1c, w_in_bf, ctx_len)
    px3 = px.reshape(px.shape[0], bsz, seq, d)
    pc3 = pc.reshape(pc.shape[0], bsz, ctx_len, d)

    lg = jax.nn.log_sigmoid(ret_decay[l].astype(F32))
    zeros_s = jnp.zeros((bsz, RET_HEADS, RET_DK, RET_DV), F32)
    _, _, sf, sb = _retention(pc3, lg, zeros_s, zeros_s)
    o_f, o_b, _, _ = _retention(px3, lg, sf, sb)

    cw = conv_w[l]
    cb = conv_b[l][None, :]
    h_dirs = []
    for dr in range(2):
        wa_bf = lru_wa[l, dr].astype(BF16)
        wx_bf = lru_wx[l, dr].astype(BF16)
        ba = lru_ba[l, dr][None, :]
        bx = lru_bx[l, dr][None, :]
        lam = lru_lambda[l, dr][None, :]
        h0 = jnp.zeros((bsz, 1, cw.shape[1]), F32)
        _, fin = _lru_direction(pc3, cw, cb, wa_bf, ba, wx_bf, bx, lam, h0, reverse=bool(dr))
        hx, _ = _lru_direction(px3, cw, cb, wa_bf, ba, wx_bf, bx, lam, fin, reverse=bool(dr))
        h_dirs.append(hx.reshape(bsz * seq, -1))

    x1 = _merge(x2, o_f.reshape(bsz * seq, -1), o_b.reshape(bsz * seq, -1), h_dirs[0], h_dirs[1], px, g1x,
                w_ret_out[l].astype(BF16), w_lru_out[l].astype(BF16), w_out[l].astype(BF16), seq)

    wq_t = peer_wq[l].T
    wq_hi = wq_t.astype(BF16)
    wq_lo = (wq_t - wq_hi.astype(F32)).astype(BF16)
    keys = peer_keys[l].reshape(2 * PEER_HEADS, PEER_NKEYS, PEER_DHALF)
    k_hi = keys.astype(BF16)
    k_lo = (keys - k_hi.astype(F32)).astype(BF16)
    u_bf = peer_u[l].astype(BF16)
    vt_bf = peer_v[l].astype(BF16).reshape(-1, PEER_EBLK, d).transpose(0, 2, 1)
    out = _peer(x1, norm2_g[l][None, :], sh2x, sc2x, g2x, final_g[None, :], wq_hi, wq_lo, k_hi, k_lo,
                u_bf, vt_bf, seq)
    return out.reshape(bsz, seq, d)
```

```python
import functools
import math

import jax
import jax.numpy as jnp
import numpy as np
from jax import lax
from jax.experimental import pallas as pl
from jax.experimental.pallas import tpu as pltpu

F32 = jnp.float32
BF16 = jnp.bfloat16

EPS = 1e-6
GRID_W = 64
ROPE_BASE = 10000.0
RET_HEADS = 4
RET_DK = 128
RET_DV = 256
RET_CHUNK = 128
LRU_BLOCKS = 8
LRU_BS = 128
LRU_C = 8.0
CONV_W = 4
PEER_HEADS = 8
PEER_DHALF = 64
PEER_NKEYS = 128
PEER_TOPK = 16
SLAB_QK, SLAB_V, SLAB_RGATE, SLAB_LRUX, SLAB_LRUG, SLAB_BGR, SLAB_BGL = range(7)

V7X_VMEM_BYTES = 64 * 1024 * 1024
LANES = 128
SUBLANES = 8

NEG_INF = float("-inf")


def _vmem_limit(nbytes):
    return int(min(max(nbytes, 16 * 1024 * 1024), V7X_VMEM_BYTES - 6 * 1024 * 1024))


def _dot(a, b):
    return jnp.dot(a, b, preferred_element_type=F32)


def _split_bf16(a):
    hi = a.astype(BF16)
    lo = (a - hi.astype(F32)).astype(BF16)
    return hi, lo


def _dot3(a_hi, a_lo, b_hi, b_lo):
    return _dot(a_hi, b_hi) + (_dot(a_hi, b_lo) + _dot(a_lo, b_hi))


def _mod_body(c_ref, w_ref, b_ref, o_ref):
    c = c_ref[...]
    sc = c * jax.nn.sigmoid(c)
    c_hi, c_lo = _split_bf16(sc)
    w_hi, w_lo = _split_bf16(w_ref[...])
    o_ref[...] = _dot3(c_hi, c_lo, w_hi, w_lo) + b_ref[...]


def _modulation(c_all, mod_w, mod_b):
    rows, d = c_all.shape
    n = mod_w.shape[1]
    tn = 1536
    return pl.pallas_call(
        _mod_body,
        grid=(n // tn,),
        in_specs=[pl.BlockSpec((rows, d), lambda j: (0, 0)),
                  pl.BlockSpec((d, tn), lambda j: (0, j)),
                  pl.BlockSpec((1, tn), lambda j: (0, j))],
        out_specs=pl.BlockSpec((rows, tn), lambda j: (0, j)),
        out_shape=jax.ShapeDtypeStruct((rows, n), F32),
        compiler_params=pltpu.CompilerParams(
            dimension_semantics=("arbitrary",), vmem_limit_bytes=_vmem_limit(40 << 20)),
        name="modulation",
    )(c_all, mod_w, mod_b)


INPROJ_TM = 256

def _rms_mod(x, g, shift, scale):
    ms = jnp.mean(x * x, axis=-1, keepdims=True)
    h = x * lax.rsqrt(ms + EPS) * g
    return h * (1.0 + scale) + shift


def _inproj_body(x_ref, g_ref, sh_ref, sc_ref, w_ref, *rest, use_rope):
    if use_rope:
        cos_ref, sin_ref, o_ref = rest
    else:
        o_ref, = rest
    h = _rms_mod(x_ref[...], g_ref[...], sh_ref[0], sc_ref[0]).astype(BF16)
    d = h.shape[1]
    for k in range(o_ref.shape[0]):
        res = _dot(h, w_ref[:, k * d:(k + 1) * d])
        if k == SLAB_QK:
            lane = lax.broadcasted_iota(jnp.int32, (res.shape[0], RET_DK), 1)
            lane_low = (lane % 64) < 32
            parts = []
            for gi in range(d // RET_DK):
                part = res[:, gi * RET_DK:(gi + 1) * RET_DK]
                if use_rope:
                    part = _rope(part, cos_ref[...], sin_ref[...], lane_low)
                if gi >= RET_HEADS:
                    part = part * (RET_DK ** -0.5)
                parts.append(part)
            res = jnp.concatenate(parts, axis=-1)
        o_ref[k] = res


def _in_projection(x2, g, shift, scale, w_bf, seq_len, rope=None):
    t, d = x2.shape
    n = w_bf.shape[1]
    tm = min(INPROJ_TM, seq_len)
    per_batch = seq_len // tm
    if shift.shape[0] == 1:
        mod_map = lambda i: (0, 0, 0)
    else:
        mod_map = lambda i: (i // per_batch, 0, 0)
    vmem = 2 * (d * n * 2 + (n // d) * tm * d * 4 + tm * d * 4) + (4 << 20)
    in_specs = [pl.BlockSpec((tm, d), lambda i: (i, 0)),
                pl.BlockSpec((1, d), lambda i: (0, 0)),
                pl.BlockSpec((1, 1, d), mod_map),
                pl.BlockSpec((1, 1, d), mod_map),
                pl.BlockSpec((d, n), lambda i: (0, 0))]
    args = [x2, g, shift, scale, w_bf]
    if rope is not None:
        in_specs += [pl.BlockSpec((tm, RET_DK), lambda i: (i % per_batch, 0))] * 2
        args += list(rope)
    return pl.pallas_call(
        functools.partial(_inproj_body, use_rope=rope is not None),
        grid=(t // tm,),
        in_specs=in_specs,
        out_specs=pl.BlockSpec((n // d, tm, d), lambda i: (0, i, 0)),
        out_shape=jax.ShapeDtypeStruct((n // d, t, d), F32),
        compiler_params=pltpu.CompilerParams(
            dimension_semantics=("parallel",), vmem_limit_bytes=_vmem_limit(vmem)),
        name="in_projection",
    )(*args)


def _rope_tables(n_tok):
    quarter = RET_DK // 4
    pos = np.arange(n_tok)
    row = (pos // GRID_W).astype(np.float32)
    col = (pos % GRID_W).astype(np.float32)
    inv = np.power(np.float32(ROPE_BASE), -np.arange(quarter, dtype=np.float32) / np.float32(quarter))
    ar = (row[:, None] * inv[None, :]).astype(np.float64)
    ac = (col[:, None] * inv[None, :]).astype(np.float64)
    cos_t = np.concatenate([np.cos(ar), np.cos(ar), np.cos(ac), np.cos(ac)], axis=-1)
    sin_t = np.concatenate([-np.sin(ar), np.sin(ar), -np.sin(ac), np.sin(ac)], axis=-1)
    return jnp.asarray(cos_t, F32), jnp.asarray(sin_t, F32)


def _rope(t, cos_t, sin_t, lane_low):
    swapped = jnp.where(lane_low, pltpu.roll(t, 96, 1), pltpu.roll(t, 32, 1))
    return t * cos_t + swapped * sin_t


def _ret_body(lg_ref, qkf_ref, vf_ref, qkb_ref, vb_ref, s0f_ref, s0b_ref,
              of_ref, ob_ref, sff_ref, sfb_ref, sf_scr, sb_scr, dmat_scr, xi_scr, zeta_scr, cdec_scr, *, n_batch):
    c = pl.program_id(0)
    n = pl.num_programs(0)
    C = RET_CHUNK

    @pl.when(c == 0)
    def _():
        sf_scr[...] = s0f_ref[...]
        sb_scr[...] = s0b_ref[...]
        ii = lax.broadcasted_iota(jnp.int32, (C, C), 0)
        jj = lax.broadcasted_iota(jnp.int32, (C, C), 1)
        col_i = ii.astype(F32)
        for d in range(2):
            for h in range(RET_HEADS):
                lg = lg_ref[d, h]
                if d == 0:
                    diff = (ii - jj).astype(F32)
                    keep = ii >= jj
                    xi = jnp.exp(lg * (col_i + 1.0))
                    zeta = jnp.exp(lg * (C - 1.0 - col_i))
                else:
                    diff = (jj - ii).astype(F32)
                    keep = jj > ii
                    xi = jnp.exp(lg * (C - col_i))
                    zeta = jnp.exp(lg * col_i)
                dmat_scr[d, h] = jnp.where(keep, jnp.exp(lg * jnp.maximum(diff, 0.0)), 0.0)
                xi_scr[d, h] = xi
                zeta_scr[d, h] = zeta
                cdec_scr[d, h] = jnp.exp(jnp.full((SUBLANES, RET_DV), lg * C, F32))

    refs = ((qkf_ref, vf_ref, sf_scr, of_ref), (qkb_ref, vb_ref, sb_scr, ob_ref))
    k_off = RET_HEADS * RET_DK

    def batch_body(b, carry):
        for d in range(2):
            qk_ref, v_ref, s_scr, o_ref = refs[d]
            for h in range(RET_HEADS):
                dmat, xi, zeta = dmat_scr[d, h], xi_scr[d, h], zeta_scr[d, h]
                cdec = cdec_scr[d, h, 0:1, :]
                q = qk_ref[0, b, :, h * RET_DK:(h + 1) * RET_DK]
                k = qk_ref[0, b, :, k_off + h * RET_DK:k_off + (h + 1) * RET_DK]
                v = v_ref[0, b, :, h * RET_DV:(h + 1) * RET_DV].astype(BF16)
                s_old = s_scr[b, h]
                scores = lax.dot_general(q.astype(BF16), k.astype(BF16), (((1,), (1,)), ((), ())),
                                         preferred_element_type=F32) * dmat
                o = _dot(scores.astype(BF16), v) + _dot((q * xi).astype(BF16), s_old.astype(BF16))
                kz_t = (k * zeta).T.astype(BF16)
                s_scr[b, h] = cdec * s_old + _dot(kz_t, v)
                o_ref[b, :, h * RET_DV:(h + 1) * RET_DV] = o.astype(o_ref.dtype)
        return carry

    lax.fori_loop(0, n_batch, batch_body, 0)

    @pl.when(c == n - 1)
    def _():
        sff_ref[...] = sf_scr[...]
        sfb_ref[...] = sb_scr[...]


def _retention(p4, lg, s0f, s0b):
    _, bsz, seq, d = p4.shape
    n = seq // RET_CHUNK
    C = RET_CHUNK
    v_w = RET_HEADS * RET_DV
    fwd = lambda c: c
    bwd = lambda c: n - 1 - c

    def slab_spec(slab, order):
        return pl.BlockSpec((1, bsz, C, d), lambda c: (slab, 0, order(c), 0))

    st_spec = pl.BlockSpec((bsz, RET_HEADS, RET_DK, RET_DV), lambda c: (0, 0, 0, 0))
    st_shape = jax.ShapeDtypeStruct((bsz, RET_HEADS, RET_DK, RET_DV), F32)
    o_shape = jax.ShapeDtypeStruct((bsz, seq, v_w), BF16)
    body = functools.partial(_ret_body, n_batch=bsz)
    return pl.pallas_call(
        body,
        grid=(n,),
        in_specs=[pl.BlockSpec(memory_space=pltpu.SMEM),
                  slab_spec(SLAB_QK, fwd), slab_spec(SLAB_V, fwd),
                  slab_spec(SLAB_QK, bwd), slab_spec(SLAB_V, bwd),
                  st_spec, st_spec],
        out_specs=[pl.BlockSpec((bsz, C, v_w), lambda c: (0, c, 0)),
                   pl.BlockSpec((bsz, C, v_w), lambda c: (0, n - 1 - c, 0)),
                   st_spec, st_spec],
        out_shape=[o_shape, o_shape, st_shape, st_shape],
        scratch_shapes=[pltpu.VMEM((bsz, RET_HEADS, RET_DK, RET_DV), F32),
                        pltpu.VMEM((bsz, RET_HEADS, RET_DK, RET_DV), F32),
                        pltpu.VMEM((2, RET_HEADS, C, C), F32),
                        pltpu.VMEM((2, RET_HEADS, C, RET_DK), F32),
                        pltpu.VMEM((2, RET_HEADS, C, RET_DK), F32),
                        pltpu.VMEM((2, RET_HEADS, SUBLANES, RET_DV), F32)],
        compiler_params=pltpu.CompilerParams(
            dimension_semantics=("arbitrary",), vmem_limit_bytes=_vmem_limit(52 << 20)),
        name="retention",
    )(lg, p4, p4, p4, p4, s0f, s0b)


LRU_TBLK = 1024

def _lru_body(uc_ref, up_ref, un_ref, cw_ref, cb_ref, wa_ref, ba_ref, wx_ref, bx_ref, lam_ref, h0_ref,
              h_ref, fin_ref, ext_scr, a_scr, b_scr, carry_scr, *, reverse, t_blk):
    tb = pl.program_id(1)
    nt = pl.num_programs(1)
    pos = (nt - 1 - tb) if reverse else tb
    T = t_blk

    @pl.when(tb == 0)
    def _():
        carry_scr[...] = h0_ref[0]

    has_prev = (pos > 0).astype(F32)
    has_next = (pos < nt - 1).astype(F32)
    ext_scr[0:8, :] = up_ref[0, 0] * has_prev
    ext_scr[8:8 + T, :] = uc_ref[0, 0]
    ext_scr[8 + T:16 + T, :] = un_ref[0, 0] * has_next
    full = ext_scr[...]
    rows = full.shape[0]
    u = (cb_ref[...]
         + pltpu.roll(full, 2, 0)[8:8 + T] * cw_ref[0:1, :]
         + pltpu.roll(full, 1, 0)[8:8 + T] * cw_ref[1:2, :]
         + full[8:8 + T] * cw_ref[2:3, :]
         + pltpu.roll(full, rows - 1, 0)[8:8 + T] * cw_ref[3:4, :])
    ub = u.astype(BF16)
    rr = []
    xx = []
    for nb in range(LRU_BLOCKS):
        blk = ub[:, nb * LRU_BS:(nb + 1) * LRU_BS]
        rr.append(_dot(blk, wa_ref[nb]))
        xx.append(_dot(blk, wx_ref[nb]))
    r = jax.nn.sigmoid(jnp.concatenate(rr, axis=-1) + ba_ref[...])
    gate_i = jax.nn.sigmoid(jnp.concatenate(xx, axis=-1) + bx_ref[...])
    z = -lam_ref[...]
    softplus = jnp.maximum(z, 0.0) + jnp.log1p(jnp.exp(-jnp.abs(z)))
    log_a = (-LRU_C) * r * softplus
    a = jnp.exp(log_a)
    a_scr[...] = a
    b_scr[...] = jnp.sqrt(1.0 - a * a) * (gate_i * u)

    row = lax.broadcasted_iota(jnp.int32, (SUBLANES, a_scr.shape[1]), 0)
    n_grp = T // SUBLANES

    def step(i, hprev):
        g = (n_grp - 1 - i) if reverse else i
        r0 = pl.multiple_of(g * SUBLANES, SUBLANES)
        a = a_scr[pl.ds(r0, SUBLANES), :]
        b = b_scr[pl.ds(r0, SUBLANES), :]
        for s in (1, 2, 4):
            if reverse:
                valid = row < SUBLANES - s
                shift = SUBLANES - s
            else:
                valid = row >= s
                shift = s
            a_s = jnp.where(valid, pltpu.roll(a, shift, 0), 1.0)
            b_s = jnp.where(valid, pltpu.roll(b, shift, 0), 0.0)
            b = a * b_s + b
            a = a * a_s
        h = a * hprev + b
        h_ref[0, pl.ds(r0, SUBLANES), :] = h
        return h[0:1, :] if reverse else h[SUBLANES - 1:SUBLANES, :]

    hfin = lax.fori_loop(0, n_grp, step, carry_scr[...], unroll=2)
    carry_scr[...] = hfin
    fin_ref[0] = hfin


def _lru_direction(p4, cw, cb, wa_bf, ba, wx_bf, bx, lam, h0, reverse):
    _, bsz, seq, d = p4.shape
    T = min(LRU_TBLK, seq)
    nt = seq // T
    sl = SLAB_LRUX
    hb = T // SUBLANES
    n_halo = seq // SUBLANES
    order = (lambda tb: nt - 1 - tb) if reverse else (lambda tb: tb)
    body = functools.partial(_lru_body, reverse=reverse, t_blk=T)
    vec = lambda: pl.BlockSpec((1, d), lambda b, tb: (0, 0))
    return pl.pallas_call(
        body,
        grid=(bsz, nt),
        in_specs=[pl.BlockSpec((1, 1, T, d), lambda b, tb: (sl, b, order(tb), 0)),
                  pl.BlockSpec((1, 1, SUBLANES, d), lambda b, tb: (sl, b, jnp.maximum(order(tb) * hb - 1, 0), 0)),
                  pl.BlockSpec((1, 1, SUBLANES, d),
                               lambda b, tb: (sl, b, jnp.minimum((order(tb) + 1) * hb, n_halo - 1), 0)),
                  pl.BlockSpec((CONV_W, d), lambda b, tb: (0, 0)),
                  vec(),
                  pl.BlockSpec((LRU_BLOCKS, LRU_BS, LRU_BS), lambda b, tb: (0, 0, 0)),
                  vec(),
                  pl.BlockSpec((LRU_BLOCKS, LRU_BS, LRU_BS), lambda b, tb: (0, 0, 0)),
                  vec(), vec(),
                  pl.BlockSpec((1, 1, d), lambda b, tb: (b, 0, 0))],
        out_specs=[pl.BlockSpec((1, T, d), lambda b, tb: (b, order(tb), 0)),
                   pl.BlockSpec((1, 1, d), lambda b, tb: (b, 0, 0))],
        out_shape=[jax.ShapeDtypeStruct((bsz, seq, d), F32), jax.ShapeDtypeStruct((bsz, 1, d), F32)],
        scratch_shapes=[pltpu.VMEM((T + 2 * SUBLANES, d), F32), pltpu.VMEM((T, d), F32),
                        pltpu.VMEM((T, d), F32), pltpu.VMEM((1, d), F32)],
        compiler_params=pltpu.CompilerParams(
            dimension_semantics=("parallel", "arbitrary"), vmem_limit_bytes=_vmem_limit(40 << 20)),
        name="rglru_rev" if reverse else "rglru_fwd",
    )(p4, p4, p4, cw, cb, wa_bf, ba, wx_bf, bx, lam, h0)


def _merge_body(x_ref, of_ref, ob_ref, hf_ref, hb_ref, rg_ref, lg_ref, bgr_ref, bgl_ref, g1_ref,
                wr_ref, wl_ref, wo_ref, o_ref):
    o = of_ref[...].astype(F32) + ob_ref[...].astype(F32)
    parts = []
    for h in range(RET_HEADS):
        oh = o[:, h * RET_DV:(h + 1) * RET_DV]
        mu = jnp.mean(oh, axis=-1, keepdims=True)
        cen = oh - mu
        var = jnp.mean(cen * cen, axis=-1, keepdims=True)
        parts.append(cen * lax.rsqrt(var + EPS))
    on = jnp.concatenate(parts, axis=-1)
    rg = rg_ref[0]
    ret = _dot((on * (rg * jax.nn.sigmoid(rg))).astype(BF16), wr_ref[...])
    lru = _dot(((hf_ref[...] + hb_ref[...]) * jax.nn.gelu(lg_ref[0])).astype(BF16), wl_ref[...])
    y = jax.nn.sigmoid(bgr_ref[0]) * ret + jax.nn.sigmoid(bgl_ref[0]) * lru
    o_ref[...] = x_ref[...] + g1_ref[0] * _dot(y.astype(BF16), wo_ref[...])


def _merge(x2, o_f, o_b, h_f, h_b, p2, g1, wr_bf, wl_bf, wo_bf, seq_len):
    t, d = x2.shape
    tm = 512
    per_batch = seq_len // tm
    tok = lambda: pl.BlockSpec((tm, d), lambda i: (i, 0))
    pcol = lambda slab: pl.BlockSpec((1, tm, d), lambda i: (slab, i, 0))
    wspec = lambda: pl.BlockSpec((d, d), lambda i: (0, 0))
    return pl.pallas_call(
        _merge_body,
        grid=(t // tm,),
        in_specs=[tok(), tok(), tok(), tok(), tok(),
                  pcol(SLAB_RGATE), pcol(SLAB_LRUG), pcol(SLAB_BGR), pcol(SLAB_BGL),
                  pl.BlockSpec((1, 1, d), lambda i: (i // per_batch, 0, 0)),
                  wspec(), wspec(), wspec()],
        out_specs=tok(),
        out_shape=jax.ShapeDtypeStruct((t, d), F32),
        compiler_params=pltpu.CompilerParams(
            dimension_semantics=("parallel",), vmem_limit_bytes=_vmem_limit(52 << 20)),
        name="merge",
    )(x2, o_f, o_b, h_f, h_b, p2, p2, p2, p2, g1, wr_bf, wl_bf, wo_bf)


PEER_TM = 512
PEER_EBLK = 2048
PEER_SLABS = PEER_EBLK // PEER_NKEYS
PEER_RB = 256
PEER_HG = 2
N_CAND_ROWS = 16 + 7 * 8 + 8


_GELU_K1 = 2.0 * math.sqrt(2.0 / math.pi) / math.log(2.0)
_GELU_K2 = _GELU_K1 * 0.044715


def _gelu_tanh(x):
    return x / (1.0 + jnp.exp2(x * (-_GELU_K1 - _GELU_K2 * (x * x))))


def _max_first(v, order_groups):
    groups = [(v[k * SUBLANES:(k + 1) * SUBLANES], o) for k, o in enumerate(order_groups)]
    while len(groups) > 1:
        nxt = []
        for a in range(0, len(groups) - 1, 2):
            (va, oa), (vb, ob) = groups[a], groups[a + 1]
            later = vb > va
            nxt.append((jnp.where(later, vb, va), jnp.where(later, ob, oa)))
        if len(groups) % 2:
            nxt.append(groups[-1])
        groups = nxt
    v8, o8 = groups[0]
    m = jnp.max(v8, axis=0, keepdims=True)
    first = jnp.min(jnp.where(v8 == m, o8, 1e9), axis=0, keepdims=True)
    return m, first


def _topk_keys(work_ref, top_ref, idx_ref, n_prob):
    _, nk, n = work_ref.shape
    order = lax.broadcasted_iota(jnp.int32, (nk, n), 0).astype(F32)
    row8 = lax.broadcasted_iota(jnp.int32, (SUBLANES, n), 0).astype(F32)
    order_groups = [row8 + float(k * SUBLANES) for k in range(nk // SUBLANES)]

    def body(a, carry):
        for p in range(n_prob):
            v = work_ref[p]
            m, first = _max_first(v, order_groups)
            work_ref[p] = jnp.where(order == first, NEG_INF, v)
            top_ref[p, pl.ds(a, 1), :] = m
            idx_ref[p, pl.ds(a, 1), :] = first
        return carry

    lax.fori_loop(0, PEER_TOPK, body, 0)
    return [(idx_ref[p], top_ref[p]) for p in range(n_prob)]


def _by_rank(idx, values, fill, shape):
    order = lax.broadcasted_iota(jnp.int32, shape, 0).astype(F32)
    out = jnp.full(shape, fill, F32)
    for a in range(PEER_TOPK):
        out = jnp.where(order == idx[a:a + 1], values[a], out)
    return out


def _candidates(t1, t2):
    n = t1.shape[1]
    row8 = lax.broadcasted_iota(jnp.int32, (SUBLANES, n), 0).astype(F32)
    cands = [t2 + t1[0:1]]
    flats = [row8, row8 + float(SUBLANES)]
    for a in range(1, 8):
        n_a = PEER_TOPK // (a + 1)
        cands.append(jnp.where(row8 < n_a, t2[0:8] + t1[a:a + 1], NEG_INF))
        flats.append(row8 + float(a * PEER_TOPK))
    cands.append(t1[8:16] + t2[0:1])
    flats.append((row8 + 8.0) * float(PEER_TOPK))
    return jnp.concatenate(cands, axis=0), flats


def _select_candidates(cand_ref, flats, n_prob):
    flat = jnp.concatenate(flats, axis=0)

    def body(a, carry):
        for p in range(n_prob):
            v = cand_ref[p]
            _, first = _max_first(v, flats)
            cand_ref[p] = jnp.where(flat == first, NEG_INF, v)
        return carry

    lax.fori_loop(0, PEER_TOPK, body, 0)


def _route_head(s1, s2, i1, t1, i2, t2, cand_left):
    n = s1.shape[1]
    row8 = lax.broadcasted_iota(jnp.int32, (SUBLANES, n), 0).astype(F32)
    cand, _ = _candidates(t1, t2)
    sel = jnp.where(cand_left != cand, 1.0, 0.0)
    cmax = cand[0:1]
    z = jnp.sum(sel * jnp.exp(cand - cmax), axis=0, keepdims=True)
    l_top = jnp.where(row8 == 0.0, jnp.sum(sel[0:16], axis=0, keepdims=True), 0.0)
    for a in range(1, 8):
        l_a = jnp.sum(sel[8 + 8 * a:16 + 8 * a], axis=0, keepdims=True)
        l_top = jnp.where(row8 == float(a), l_a, l_top)
    l_mat = jnp.concatenate([l_top, sel[72:80]], axis=0)
    l1 = _by_rank(i1, [l_mat[a:a + 1] for a in range(PEER_TOPK)], 0.0, s1.shape)
    r2 = _by_rank(i2, [float(a) for a in range(PEER_TOPK)], float(PEER_TOPK), s2.shape)
    c1 = jnp.exp(s1 - t1[0:1]) * (1.0 / z)
    e2 = jnp.exp(s2 - t2[0:1])
    return r2, e2, l1, c1


def _peer_body(x_ref, g_ref, sh_ref, sc_ref, g2_ref, fg_ref, wqh_ref, wql_ref, kh_ref, kl_ref, u_ref, vt_ref,
               o_ref, h2t_scr, s_scr, work_scr, top_scr, idx_scr, cand_scr, r2_scr, e2_scr, l1_scr, c1_scr,
               a_scr, wf_scr, acc_scr):
    j = pl.program_id(1)
    nj = pl.num_programs(1)
    tm = x_ref.shape[0]

    @pl.when(j == 0)
    def _():
        h2 = _rms_mod(x_ref[...], g_ref[...], sh_ref[0], sc_ref[0])
        h2t = h2.T
        h_hi, h_lo = _split_bf16(h2t)
        h2t_scr[...] = h_hi
        qt = _dot3(wqh_ref[...], wql_ref[...], h_hi, h_lo)
        q_hi, q_lo = _split_bf16(qt)
        for hp in range(2 * PEER_HEADS):
            sl = slice(hp * PEER_DHALF, (hp + 1) * PEER_DHALF)
            s_scr[hp] = _dot3(kh_ref[hp], kl_ref[hp], q_hi[sl], q_lo[sl])
        acc_scr[...] = jnp.zeros_like(acc_scr)

        def head_body(hg, carry):
            n_pb = tm // PEER_RB
            units = [(hg * PEER_HG + hh, pb) for hh in range(PEER_HG) for pb in range(n_pb)]
            for k, (h, pb) in enumerate(units):
                ps = slice(pb * PEER_RB, (pb + 1) * PEER_RB)
                work_scr[2 * k] = s_scr[2 * h, :, ps]
                work_scr[2 * k + 1] = s_scr[2 * h + 1, :, ps]
            tops = _topk_keys(work_scr, top_scr, idx_scr, 2 * len(units))
            for k in range(len(units)):
                cand_scr[k], flats = _candidates(tops[2 * k][1], tops[2 * k + 1][1])
            _select_candidates(cand_scr, flats, len(units))
            for k, (h, pb) in enumerate(units):
                ps = slice(pb * PEER_RB, (pb + 1) * PEER_RB)
                (i1, t1), (i2, t2) = tops[2 * k], tops[2 * k + 1]
                r2, e2, l1, c1 = _route_head(s_scr[2 * h, :, ps], s_scr[2 * h + 1, :, ps], i1, t1, i2, t2,
                                             cand_scr[k])
                for q in range(PEER_RB // LANES):
                    qs = slice(q * LANES, (q + 1) * LANES)
                    r2_scr[h, pb * (PEER_RB // LANES) + q] = r2[:, qs]
                    e2_scr[h, pb * (PEER_RB // LANES) + q] = e2[:, qs]
                l1_scr[h, :, ps] = l1
                c1_scr[h, :, ps] = c1
            return carry

        lax.fori_loop(0, PEER_HEADS // PEER_HG, head_body, 0)

    n_lb = tm // LANES

    res = _dot(u_ref[...], h2t_scr[...])
    for q in range(n_lb):
        a_scr[q] = res[:, q * LANES:(q + 1) * LANES]

    def slab_body(s, carry):
        i1 = j * PEER_SLABS + s
        r0 = pl.multiple_of(s * PEER_NKEYS, PEER_NKEYS)
        lrows = [l1_scr[h, pl.ds(i1, 1), :] for h in range(PEER_HEADS)]
        crows = [c1_scr[h, pl.ds(i1, 1), :] for h in range(PEER_HEADS)]
        for lb in range(n_lb):
            ls = slice(lb * LANES, (lb + 1) * LANES)
            w = None
            for h in range(PEER_HEADS):
                term = jnp.where(r2_scr[h, lb] < lrows[h][:, ls], e2_scr[h, lb], 0.0) * crows[h][:, ls]
                w = term if w is None else w + term
            act = _gelu_tanh(a_scr[lb, pl.ds(r0, PEER_NKEYS), :])
            wf_scr[lb, pl.ds(r0, PEER_NKEYS), :] = (w * act).astype(BF16)
        return carry

    lax.fori_loop(0, PEER_SLABS, slab_body, 0)
    rhs = jnp.concatenate([wf_scr[q] for q in range(n_lb)], axis=1)
    acc_scr[...] += _dot(vt_ref[0], rhs)

    @pl.when(j == nj - 1)
    def _():
        x2 = x_ref[...] + g2_ref[0] * acc_scr[...].T
        ms = jnp.mean(x2 * x2, axis=-1, keepdims=True)
        o_ref[...] = x2 * lax.rsqrt(ms + EPS) * fg_ref[...]


def _peer(x2, g, shift, scale, gate, final_g, wq_hi, wq_lo, k_hi, k_lo, u_bf, vt_bf, seq_len):
    t, d = x2.shape
    n_exp = u_bf.shape[0]
    tm = PEER_TM
    per_batch = seq_len // tm
    mod = lambda: pl.BlockSpec((1, 1, d), lambda i, j: (i // per_batch, 0, 0))
    vec = lambda: pl.BlockSpec((1, d), lambda i, j: (0, 0))
    nq = wq_hi.shape[0]
    tab = lambda: pltpu.VMEM((PEER_HEADS, PEER_NKEYS, tm), F32)
    nblk = n_exp // PEER_EBLK
    return pl.pallas_call(
        _peer_body,
        grid=(t // tm, nblk),
        in_specs=[pl.BlockSpec((tm, d), lambda i, j: (i, 0), pipeline_mode=pl.Buffered(1)),
                  vec(), mod(), mod(), mod(), vec(),
                  pl.BlockSpec((nq, d), lambda i, j: (0, 0)),
                  pl.BlockSpec((nq, d), lambda i, j: (0, 0)),
                  pl.BlockSpec((2 * PEER_HEADS, PEER_NKEYS, PEER_DHALF), lambda i, j: (0, 0, 0)),
                  pl.BlockSpec((2 * PEER_HEADS, PEER_NKEYS, PEER_DHALF), lambda i, j: (0, 0, 0)),
                  pl.BlockSpec((PEER_EBLK, d), lambda i, j: (j, 0)),
                  pl.BlockSpec((1, d, PEER_EBLK), lambda i, j: (j, 0, 0))],
        out_specs=pl.BlockSpec((tm, d), lambda i, j: (i, 0)),
        out_shape=jax.ShapeDtypeStruct((t, d), F32),
        scratch_shapes=[pltpu.VMEM((d, tm), BF16),
                        pltpu.VMEM((2 * PEER_HEADS, PEER_NKEYS, tm), F32),
                        pltpu.VMEM((2 * PEER_HG * (tm // PEER_RB), PEER_NKEYS, PEER_RB), F32),
                        pltpu.VMEM((2 * PEER_HG * (tm // PEER_RB), PEER_TOPK, PEER_RB), F32),
                        pltpu.VMEM((2 * PEER_HG * (tm // PEER_RB), PEER_TOPK, PEER_RB), F32),
                        pltpu.VMEM((PEER_HG * (tm // PEER_RB), N_CAND_ROWS, PEER_RB), F32),
                        pltpu.VMEM((PEER_HEADS, tm // LANES, PEER_NKEYS, LANES), F32),
                        pltpu.VMEM((PEER_HEADS, tm // LANES, PEER_NKEYS, LANES), F32),
                        tab(), tab(),
                        pltpu.VMEM((tm // LANES, PEER_EBLK, LANES), F32),
                        pltpu.VMEM((tm // LANES, PEER_EBLK, LANES), BF16),
                        pltpu.VMEM((d, tm), F32)],
        compiler_params=pltpu.CompilerParams(
            dimension_semantics=("parallel", "arbitrary"), vmem_limit_bytes=_vmem_limit(58 << 20)),
        name="peer",
    )(x2, g, shift, scale, gate, final_g, wq_hi, wq_lo, k_hi, k_lo, u_bf, vt_bf)


def kernel(x, c, ctx, c_ctx, mod_w, mod_b, norm1_g, norm2_g, w_in, ret_decay, conv_w, conv_b, lru_wa, lru_ba,
           lru_wx, lru_bx, lru_lambda, w_ret_out, w_lru_out, w_out, peer_wq, peer_keys, peer_u, peer_v, final_g):
    bsz, seq, d = x.shape
    ctx_len = ctx.shape[1]
    l = 0

    pad = (-(bsz + 1)) % SUBLANES
    c_all = jnp.concatenate([c, c_ctx[None, :], jnp.zeros((pad, d), F32)], axis=0)
    mod = _modulation(c_all, mod_w[l], mod_b[l][None, :])
    mod_x = mod[:bsz].reshape(bsz, 1, 6, d)
    sh1x, sc1x, g1x, sh2x, sc2x, g2x = [mod_x[:, :, i, :] for i in range(6)]
    mod_c = mod[bsz:bsz + 1].reshape(1, 1, 6, d)
    sh1c, sc1c = mod_c[:, :, 0, :], mod_c[:, :, 1, :]

    w_in_bf = w_in[l].astype(BF16)
    g1 = norm1_g[l][None, :]
    x2 = x.reshape(bsz * seq, d)
    ctx2 = ctx.reshape(bsz * ctx_len, d)
    px = _in_projection(x2, g1, sh1x, sc1x, w_in_bf, seq, rope=_rope_tables(seq))
    pc = _in_projection(ctx2, g1, sh1c, sc1c, w_in_bf, ctx_len)
    px3 = px.reshape(px.shape[0], bsz, seq, d)
    pc3 = pc.reshape(pc.shape[0], bsz, ctx_len, d)

    lg = jax.nn.log_sigmoid(ret_decay[l].astype(F32))
    zeros_s = jnp.zeros((bsz, RET_HEADS, RET_DK, RET_DV), F32)
    _, _, sf, sb = _retention(pc3, lg, zeros_s, zeros_s)
    o_f, o_b, _, _ = _retention(px3, lg, sf, sb)

    cw = conv_w[l]
    cb = conv_b[l][None, :]
    h_dirs = []
    for dr in range(2):
        wa_bf = lru_wa[l, dr].astype(BF16)
        wx_bf = lru_wx[l, dr].astype(BF16)
        ba = lru_ba[l, dr][None, :]
        bx = lru_bx[l, dr][None, :]
        lam = lru_lambda[l, dr][None, :]
        h0 = jnp.zeros((bsz, 1, cw.shape[1]), F32)
        _, fin = _lru_direction(pc3, cw, cb, wa_bf, ba, wx_bf, bx, lam, h0, reverse=bool(dr))
        hx, _ = _lru_direction(px3, cw, cb, wa_bf, ba, wx_bf, bx, lam, fin, reverse=bool(dr))
        h_dirs.append(hx.reshape(bsz * seq, -1))

    x1 = _merge(x2, o_f.reshape(bsz * seq, -1), o_b.reshape(bsz * seq, -1), h_dirs[0], h_dirs[1], px, g1x,
                w_ret_out[l].astype(BF16), w_lru_out[l].astype(BF16), w_out[l].astype(BF16), seq)

    wq_t = peer_wq[l].T
    wq_hi = wq_t.astype(BF16)
    wq_lo = (wq_t - wq_hi.astype(F32)).astype(BF16)
    keys = peer_keys[l].reshape(2 * PEER_HEADS, PEER_NKEYS, PEER_DHALF)
    k_hi = keys.astype(BF16)
    k_lo = (keys - k_hi.astype(F32)).astype(BF16)
    u_bf = peer_u[l].astype(BF16)
    vt_bf = peer_v[l].astype(BF16).reshape(-1, PEER_EBLK, d).transpose(0, 2, 1)
    out = _peer(x1, norm2_g[l][None, :], sh2x, sc2x, g2x, final_g[None, :], wq_hi, wq_lo, k_hi, k_lo,
                u_bf, vt_bf, seq)
    return out.reshape(bsz, seq, d)
```

```python
import functools
import math

import jax
import jax.numpy as jnp
import numpy as np
from jax import lax
from jax.experimental import pallas as pl
from jax.experimental.pallas import tpu as pltpu

F32 = jnp.float32
BF16 = jnp.bfloat16

EPS = 1e-6
GRID_W = 64
ROPE_BASE = 10000.0
RET_HEADS = 4
RET_DK = 128
RET_DV = 256
RET_CHUNK = 128
LRU_BLOCKS = 8
LRU_BS = 128
LRU_C = 8.0
CONV_W = 4
PEER_HEADS = 8
PEER_DHALF = 64
PEER_NKEYS = 128
PEER_TOPK = 16
SLAB_QK, SLAB_V, SLAB_RGATE, SLAB_LRUX, SLAB_LRUG, SLAB_BGR, SLAB_BGL = range(7)
MAIN_SLABS = (SLAB_QK, SLAB_V, SLAB_LRUX)
GATE_SLABS = (SLAB_RGATE, SLAB_LRUG, SLAB_BGR, SLAB_BGL)

V7X_VMEM_BYTES = 64 * 1024 * 1024
LANES = 128
SUBLANES = 8

NEG_INF = float("-inf")


def _vmem_limit(nbytes):
    return int(min(max(nbytes, 16 * 1024 * 1024), V7X_VMEM_BYTES - 6 * 1024 * 1024))


def _dot(a, b):
    return jnp.dot(a, b, preferred_element_type=F32)


def _split_bf16(a):
    hi = a.astype(BF16)
    lo = (a - hi.astype(F32)).astype(BF16)
    return hi, lo


def _dot3(a_hi, a_lo, b_hi, b_lo):
    return _dot(a_hi, b_hi) + (_dot(a_hi, b_lo) + _dot(a_lo, b_hi))


def _mod_body(c_ref, w_ref, b_ref, o_ref):
    c = c_ref[...]
    sc = c * jax.nn.sigmoid(c)
    c_hi, c_lo = _split_bf16(sc)
    w_hi, w_lo = _split_bf16(w_ref[...])
    o_ref[...] = _dot3(c_hi, c_lo, w_hi, w_lo) + b_ref[...]


def _modulation(c_all, mod_w, mod_b):
    rows, d = c_all.shape
    n = mod_w.shape[1]
    tn = 1536
    return pl.pallas_call(
        _mod_body,
        grid=(n // tn,),
        in_specs=[pl.BlockSpec((rows, d), lambda j: (0, 0)),
                  pl.BlockSpec((d, tn), lambda j: (0, j)),
                  pl.BlockSpec((1, tn), lambda j: (0, j))],
        out_specs=pl.BlockSpec((rows, tn), lambda j: (0, j)),
        out_shape=jax.ShapeDtypeStruct((rows, n), F32),
        compiler_params=pltpu.CompilerParams(
            dimension_semantics=("arbitrary",), vmem_limit_bytes=_vmem_limit(40 << 20)),
        name="modulation",
    )(c_all, mod_w, mod_b)


INPROJ_TM = 256

def _rms_mod(x, g, shift, scale):
    ms = jnp.mean(x * x, axis=-1, keepdims=True)
    h = x * lax.rsqrt(ms + EPS) * g
    return h * (1.0 + scale) + shift


def _inproj_body(x_ref, g_ref, sh_ref, sc_ref, w_ref, *rest, use_rope):
    if use_rope:
        cos_ref, sin_ref, o_ref, gate_ref = rest
    else:
        o_ref, gate_ref = rest
    h = _rms_mod(x_ref[...], g_ref[...], sh_ref[0], sc_ref[0]).astype(BF16)
    d = h.shape[1]
    for k in range(len(MAIN_SLABS) + len(GATE_SLABS)):
        res = _dot(h, w_ref[:, k * d:(k + 1) * d])
        if k in GATE_SLABS:
            gate_ref[GATE_SLABS.index(k)] = res.astype(BF16)
            continue
        if k == SLAB_QK:
            lane = lax.broadcasted_iota(jnp.int32, (res.shape[0], RET_DK), 1)
            lane_low = (lane % 64) < 32
            parts = []
            for gi in range(d // RET_DK):
                part = res[:, gi * RET_DK:(gi + 1) * RET_DK]
                if use_rope:
                    part = _rope(part, cos_ref[...], sin_ref[...], lane_low)
                if gi >= RET_HEADS:
                    part = part * (RET_DK ** -0.5)
                parts.append(part)
            res = jnp.concatenate(parts, axis=-1)
        o_ref[MAIN_SLABS.index(k)] = res


def _in_projection(x2, g, shift, scale, w_bf, seq_len, rope=None):
    t, d = x2.shape
    n = w_bf.shape[1]
    tm = min(INPROJ_TM, seq_len)
    per_batch = seq_len // tm
    if shift.shape[0] == 1:
        mod_map = lambda i: (0, 0, 0)
    else:
        mod_map = lambda i: (i // per_batch, 0, 0)
    vmem = 2 * (d * n * 2 + (n // d) * tm * d * 4 + tm * d * 4) + (4 << 20)
    in_specs = [pl.BlockSpec((tm, d), lambda i: (i, 0)),
                pl.BlockSpec((1, d), lambda i: (0, 0)),
                pl.BlockSpec((1, 1, d), mod_map),
                pl.BlockSpec((1, 1, d), mod_map),
                pl.BlockSpec((d, n), lambda i: (0, 0))]
    args = [x2, g, shift, scale, w_bf]
    if rope is not None:
        in_specs += [pl.BlockSpec((tm, RET_DK), lambda i: (i % per_batch, 0))] * 2
        args += list(rope)
    return pl.pallas_call(
        functools.partial(_inproj_body, use_rope=rope is not None),
        grid=(t // tm,),
        in_specs=in_specs,
        out_specs=[pl.BlockSpec((len(MAIN_SLABS), tm, d), lambda i: (0, i, 0)),
                   pl.BlockSpec((len(GATE_SLABS), tm, d), lambda i: (0, i, 0))],
        out_shape=[jax.ShapeDtypeStruct((len(MAIN_SLABS), t, d), F32),
                   jax.ShapeDtypeStruct((len(GATE_SLABS), t, d), BF16)],
        compiler_params=pltpu.CompilerParams(
            dimension_semantics=("parallel",), vmem_limit_bytes=_vmem_limit(vmem)),
        name="in_projection",
    )(*args)


def _rope_tables(n_tok):
    quarter = RET_DK // 4
    pos = np.arange(n_tok)
    row = (pos // GRID_W).astype(np.float32)
    col = (pos % GRID_W).astype(np.float32)
    inv = np.power(np.float32(ROPE_BASE), -np.arange(quarter, dtype=np.float32) / np.float32(quarter))
    ar = (row[:, None] * inv[None, :]).astype(np.float64)
    ac = (col[:, None] * inv[None, :]).astype(np.float64)
    cos_t = np.concatenate([np.cos(ar), np.cos(ar), np.cos(ac), np.cos(ac)], axis=-1)
    sin_t = np.concatenate([-np.sin(ar), np.sin(ar), -np.sin(ac), np.sin(ac)], axis=-1)
    return jnp.asarray(cos_t, F32), jnp.asarray(sin_t, F32)


def _rope(t, cos_t, sin_t, lane_low):
    swapped = jnp.where(lane_low, pltpu.roll(t, 96, 1), pltpu.roll(t, 32, 1))
    return t * cos_t + swapped * sin_t


def _ret_body(lg_ref, qkf_ref, vf_ref, qkb_ref, vb_ref, s0f_ref, s0b_ref,
              of_ref, ob_ref, sff_ref, sfb_ref, sf_scr, sb_scr, dmat_scr, xi_scr, zeta_scr, cdec_scr, *, n_batch):
    c = pl.program_id(0)
    n = pl.num_programs(0)
    C = RET_CHUNK

    @pl.when(c == 0)
    def _():
        sf_scr[...] = s0f_ref[...]
        sb_scr[...] = s0b_ref[...]
        ii = lax.broadcasted_iota(jnp.int32, (C, C), 0)
        jj = lax.broadcasted_iota(jnp.int32, (C, C), 1)
        col_i = ii.astype(F32)
        for d in range(2):
            for h in range(RET_HEADS):
                lg = lg_ref[d, h]
                if d == 0:
                    diff = (ii - jj).astype(F32)
                    keep = ii >= jj
                    xi = jnp.exp(lg * (col_i + 1.0))
                    zeta = jnp.exp(lg * (C - 1.0 - col_i))
                else:
                    diff = (jj - ii).astype(F32)
                    keep = jj > ii
                    xi = jnp.exp(lg * (C - col_i))
                    zeta = jnp.exp(lg * col_i)
                dmat_scr[d, h] = jnp.where(keep, jnp.exp(lg * jnp.maximum(diff, 0.0)), 0.0)
                xi_scr[d, h] = xi
                zeta_scr[d, h] = zeta
                cdec_scr[d, h] = jnp.exp(jnp.full((SUBLANES, RET_DV), lg * C, F32))

    refs = ((qkf_ref, vf_ref, sf_scr, of_ref), (qkb_ref, vb_ref, sb_scr, ob_ref))
    k_off = RET_HEADS * RET_DK

    def batch_body(b, carry):
        for d in range(2):
            qk_ref, v_ref, s_scr, o_ref = refs[d]
            for h in range(RET_HEADS):
                dmat, xi, zeta = dmat_scr[d, h], xi_scr[d, h], zeta_scr[d, h]
                cdec = cdec_scr[d, h, 0:1, :]
                q = qk_ref[0, b, :, h * RET_DK:(h + 1) * RET_DK]
                k = qk_ref[0, b, :, k_off + h * RET_DK:k_off + (h + 1) * RET_DK]
                v = v_ref[0, b, :, h * RET_DV:(h + 1) * RET_DV].astype(BF16)
                s_old = s_scr[b, h]
                scores = lax.dot_general(q.astype(BF16), k.astype(BF16), (((1,), (1,)), ((), ())),
                                         preferred_element_type=F32) * dmat
                o = _dot(scores.astype(BF16), v) + _dot((q * xi).astype(BF16), s_old.astype(BF16))
                kz_t = (k * zeta).T.astype(BF16)
                s_scr[b, h] = cdec * s_old + _dot(kz_t, v)
                o_ref[b, :, h * RET_DV:(h + 1) * RET_DV] = o.astype(o_ref.dtype)
        return carry

    lax.fori_loop(0, n_batch, batch_body, 0)

    @pl.when(c == n - 1)
    def _():
        sff_ref[...] = sf_scr[...]
        sfb_ref[...] = sb_scr[...]


def _retention(p4, lg, s0f, s0b):
    _, bsz, seq, d = p4.shape
    n = seq // RET_CHUNK
    C = RET_CHUNK
    v_w = RET_HEADS * RET_DV
    fwd = lambda c: c
    bwd = lambda c: n - 1 - c

    def slab_spec(slab, order):
        return pl.BlockSpec((1, bsz, C, d), lambda c: (slab, 0, order(c), 0))

    st_spec = pl.BlockSpec((bsz, RET_HEADS, RET_DK, RET_DV), lambda c: (0, 0, 0, 0))
    st_shape = jax.ShapeDtypeStruct((bsz, RET_HEADS, RET_DK, RET_DV), F32)
    o_shape = jax.ShapeDtypeStruct((bsz, seq, v_w), BF16)
    body = functools.partial(_ret_body, n_batch=bsz)
    return pl.pallas_call(
        body,
        grid=(n,),
        in_specs=[pl.BlockSpec(memory_space=pltpu.SMEM),
                  slab_spec(MAIN_SLABS.index(SLAB_QK), fwd), slab_spec(MAIN_SLABS.index(SLAB_V), fwd),
                  slab_spec(MAIN_SLABS.index(SLAB_QK), bwd), slab_spec(MAIN_SLABS.index(SLAB_V), bwd),
                  st_spec, st_spec],
        out_specs=[pl.BlockSpec((bsz, C, v_w), lambda c: (0, c, 0)),
                   pl.BlockSpec((bsz, C, v_w), lambda c: (0, n - 1 - c, 0)),
                   st_spec, st_spec],
        out_shape=[o_shape, o_shape, st_shape, st_shape],
        scratch_shapes=[pltpu.VMEM((bsz, RET_HEADS, RET_DK, RET_DV), F32),
                        pltpu.VMEM((bsz, RET_HEADS, RET_DK, RET_DV), F32),
                        pltpu.VMEM((2, RET_HEADS, C, C), F32),
                        pltpu.VMEM((2, RET_HEADS, C, RET_DK), F32),
                        pltpu.VMEM((2, RET_HEADS, C, RET_DK), F32),
                        pltpu.VMEM((2, RET_HEADS, SUBLANES, RET_DV), F32)],
        compiler_params=pltpu.CompilerParams(
            dimension_semantics=("arbitrary",), vmem_limit_bytes=_vmem_limit(52 << 20)),
        name="retention",
    )(lg, p4, p4, p4, p4, s0f, s0b)


LRU_TBLK = 1024

def _lru_body(uc_ref, up_ref, un_ref, cw_ref, cb_ref, wa_ref, ba_ref, wx_ref, bx_ref, lam_ref, h0_ref,
              h_ref, fin_ref, ext_scr, a_scr, b_scr, carry_scr, *, reverse, t_blk):
    tb = pl.program_id(1)
    nt = pl.num_programs(1)
    pos = (nt - 1 - tb) if reverse else tb
    T = t_blk

    @pl.when(tb == 0)
    def _():
        carry_scr[...] = h0_ref[0]

    has_prev = (pos > 0).astype(F32)
    has_next = (pos < nt - 1).astype(F32)
    ext_scr[0:8, :] = up_ref[0, 0] * has_prev
    ext_scr[8:8 + T, :] = uc_ref[0, 0]
    ext_scr[8 + T:16 + T, :] = un_ref[0, 0] * has_next
    full = ext_scr[...]
    rows = full.shape[0]
    u = (cb_ref[...]
         + pltpu.roll(full, 2, 0)[8:8 + T] * cw_ref[0:1, :]
         + pltpu.roll(full, 1, 0)[8:8 + T] * cw_ref[1:2, :]
         + full[8:8 + T] * cw_ref[2:3, :]
         + pltpu.roll(full, rows - 1, 0)[8:8 + T] * cw_ref[3:4, :])
    ub = u.astype(BF16)
    rr = []
    xx = []
    for nb in range(LRU_BLOCKS):
        blk = ub[:, nb * LRU_BS:(nb + 1) * LRU_BS]
        rr.append(_dot(blk, wa_ref[nb]))
        xx.append(_dot(blk, wx_ref[nb]))
    r = jax.nn.sigmoid(jnp.concatenate(rr, axis=-1) + ba_ref[...])
    gate_i = jax.nn.sigmoid(jnp.concatenate(xx, axis=-1) + bx_ref[...])
    z = -lam_ref[...]
    softplus = jnp.maximum(z, 0.0) + jnp.log1p(jnp.exp(-jnp.abs(z)))
    log_a = (-LRU_C) * r * softplus
    a = jnp.exp(log_a)
    a_scr[...] = a
    b_scr[...] = jnp.sqrt(1.0 - a * a) * (gate_i * u)

    row = lax.broadcasted_iota(jnp.int32, (SUBLANES, a_scr.shape[1]), 0)
    n_grp = T // SUBLANES

    def step(i, hprev):
        g = (n_grp - 1 - i) if reverse else i
        r0 = pl.multiple_of(g * SUBLANES, SUBLANES)
        a = a_scr[pl.ds(r0, SUBLANES), :]
        b = b_scr[pl.ds(r0, SUBLANES), :]
        for s in (1, 2, 4):
            if reverse:
                valid = row < SUBLANES - s
                shift = SUBLANES - s
            else:
                valid = row >= s
                shift = s
            a_s = jnp.where(valid, pltpu.roll(a, shift, 0), 1.0)
            b_s = jnp.where(valid, pltpu.roll(b, shift, 0), 0.0)
            b = a * b_s + b
            a = a * a_s
        h = a * hprev + b
        h_ref[0, pl.ds(r0, SUBLANES), :] = h
        return h[0:1, :] if reverse else h[SUBLANES - 1:SUBLANES, :]

    hfin = lax.fori_loop(0, n_grp, step, carry_scr[...], unroll=2)
    carry_scr[...] = hfin
    fin_ref[0] = hfin


def _lru_direction(p4, cw, cb, wa_bf, ba, wx_bf, bx, lam, h0, reverse):
    _, bsz, seq, d = p4.shape
    T = min(LRU_TBLK, seq)
    nt = seq // T
    sl = MAIN_SLABS.index(SLAB_LRUX)
    hb = T // SUBLANES
    n_halo = seq // SUBLANES
    order = (lambda tb: nt - 1 - tb) if reverse else (lambda tb: tb)
    body = functools.partial(_lru_body, reverse=reverse, t_blk=T)
    vec = lambda: pl.BlockSpec((1, d), lambda b, tb: (0, 0))
    return pl.pallas_call(
        body,
        grid=(bsz, nt),
        in_specs=[pl.BlockSpec((1, 1, T, d), lambda b, tb: (sl, b, order(tb), 0)),
                  pl.BlockSpec((1, 1, SUBLANES, d), lambda b, tb: (sl, b, jnp.maximum(order(tb) * hb - 1, 0), 0)),
                  pl.BlockSpec((1, 1, SUBLANES, d),
                               lambda b, tb: (sl, b, jnp.minimum((order(tb) + 1) * hb, n_halo - 1), 0)),
                  pl.BlockSpec((CONV_W, d), lambda b, tb: (0, 0)),
                  vec(),
                  pl.BlockSpec((LRU_BLOCKS, LRU_BS, LRU_BS), lambda b, tb: (0, 0, 0)),
                  vec(),
                  pl.BlockSpec((LRU_BLOCKS, LRU_BS, LRU_BS), lambda b, tb: (0, 0, 0)),
                  vec(), vec(),
                  pl.BlockSpec((1, 1, d), lambda b, tb: (b, 0, 0))],
        out_specs=[pl.BlockSpec((1, T, d), lambda b, tb: (b, order(tb), 0)),
                   pl.BlockSpec((1, 1, d), lambda b, tb: (b, 0, 0))],
        out_shape=[jax.ShapeDtypeStruct((bsz, seq, d), F32), jax.ShapeDtypeStruct((bsz, 1, d), F32)],
        scratch_shapes=[pltpu.VMEM((T + 2 * SUBLANES, d), F32), pltpu.VMEM((T, d), F32),
                        pltpu.VMEM((T, d), F32), pltpu.VMEM((1, d), F32)],
        compiler_params=pltpu.CompilerParams(
            dimension_semantics=("parallel", "arbitrary"), vmem_limit_bytes=_vmem_limit(40 << 20)),
        name="rglru_rev" if reverse else "rglru_fwd",
    )(p4, p4, p4, cw, cb, wa_bf, ba, wx_bf, bx, lam, h0)


def _merge_body(x_ref, of_ref, ob_ref, hf_ref, hb_ref, rg_ref, lg_ref, bgr_ref, bgl_ref, g1_ref,
                wr_ref, wl_ref, wo_ref, o_ref):
    o = of_ref[...].astype(F32) + ob_ref[...].astype(F32)
    parts = []
    for h in range(RET_HEADS):
        oh = o[:, h * RET_DV:(h + 1) * RET_DV]
        mu = jnp.mean(oh, axis=-1, keepdims=True)
        cen = oh - mu
        var = jnp.mean(cen * cen, axis=-1, keepdims=True)
        parts.append(cen * lax.rsqrt(var + EPS))
    on = jnp.concatenate(parts, axis=-1)
    rg = rg_ref[0].astype(F32)
    ret = _dot((on * (rg * jax.nn.sigmoid(rg))).astype(BF16), wr_ref[...])
    lru = _dot(((hf_ref[...] + hb_ref[...]) * jax.nn.gelu(lg_ref[0].astype(F32))).astype(BF16), wl_ref[...])
    y = jax.nn.sigmoid(bgr_ref[0].astype(F32)) * ret + jax.nn.sigmoid(bgl_ref[0].astype(F32)) * lru
    o_ref[...] = x_ref[...] + g1_ref[0] * _dot(y.astype(BF16), wo_ref[...])


def _merge(x2, o_f, o_b, h_f, h_b, p2, g1, wr_bf, wl_bf, wo_bf, seq_len):
    t, d = x2.shape
    tm = 512
    per_batch = seq_len // tm
    tok = lambda: pl.BlockSpec((tm, d), lambda i: (i, 0))
    pcol = lambda slab: pl.BlockSpec((1, tm, d), lambda i: (slab, i, 0))
    wspec = lambda: pl.BlockSpec((d, d), lambda i: (0, 0))
    return pl.pallas_call(
        _merge_body,
        grid=(t // tm,),
        in_specs=[tok(), tok(), tok(), tok(), tok(),
                  pcol(GATE_SLABS.index(SLAB_RGATE)), pcol(GATE_SLABS.index(SLAB_LRUG)),
                  pcol(GATE_SLABS.index(SLAB_BGR)), pcol(GATE_SLABS.index(SLAB_BGL)),
                  pl.BlockSpec((1, 1, d), lambda i: (i // per_batch, 0, 0)),
                  wspec(), wspec(), wspec()],
        out_specs=tok(),
        out_shape=jax.ShapeDtypeStruct((t, d), F32),
        compiler_params=pltpu.CompilerParams(
            dimension_semantics=("parallel",), vmem_limit_bytes=_vmem_limit(52 << 20)),
        name="merge",
    )(x2, o_f, o_b, h_f, h_b, p2, p2, p2, p2, g1, wr_bf, wl_bf, wo_bf)


PEER_TM = 512
PEER_EBLK = 2048
PEER_SLABS = PEER_EBLK // PEER_NKEYS
PEER_RB = 256
PEER_HG = 2
N_CAND_ROWS = 16 + 7 * 8 + 8


_GELU_K1 = 2.0 * math.sqrt(2.0 / math.pi) / math.log(2.0)
_GELU_K2 = _GELU_K1 * 0.044715


def _gelu_tanh(x):
    return x / (1.0 + jnp.exp2(x * (-_GELU_K1 - _GELU_K2 * (x * x))))


def _max_first(v, order_groups):
    groups = [(v[k * SUBLANES:(k + 1) * SUBLANES], o) for k, o in enumerate(order_groups)]
    while len(groups) > 1:
        nxt = []
        for a in range(0, len(groups) - 1, 2):
            (va, oa), (vb, ob) = groups[a], groups[a + 1]
            later = vb > va
            nxt.append((jnp.where(later, vb, va), jnp.where(later, ob, oa)))
        if len(groups) % 2:
            nxt.append(groups[-1])
        groups = nxt
    v8, o8 = groups[0]
    m = jnp.max(v8, axis=0, keepdims=True)
    first = jnp.min(jnp.where(v8 == m, o8, 1e9), axis=0, keepdims=True)
    return m, first


def _topk_keys(work_ref, top_ref, idx_ref, n_prob):
    _, nk, n = work_ref.shape
    order = lax.broadcasted_iota(jnp.int32, (nk, n), 0).astype(F32)
    row8 = lax.broadcasted_iota(jnp.int32, (SUBLANES, n), 0).astype(F32)
    order_groups = [row8 + float(k * SUBLANES) for k in range(nk // SUBLANES)]

    def body(a, carry):
        for p in range(n_prob):
            v = work_ref[p]
            m, first = _max_first(v, order_groups)
            work_ref[p] = jnp.where(order == first, NEG_INF, v)
            top_ref[p, pl.ds(a, 1), :] = m
            idx_ref[p, pl.ds(a, 1), :] = first
        return carry

    lax.fori_loop(0, PEER_TOPK, body, 0)
    return [(idx_ref[p], top_ref[p]) for p in range(n_prob)]


def _by_rank(idx, values, fill, shape):
    order = lax.broadcasted_iota(jnp.int32, shape, 0).astype(F32)
    out = jnp.full(shape, fill, F32)
    for a in range(PEER_TOPK):
        out = jnp.where(order == idx[a:a + 1], values[a], out)
    return out


def _candidates(t1, t2):
    n = t1.shape[1]
    row8 = lax.broadcasted_iota(jnp.int32, (SUBLANES, n), 0).astype(F32)
    cands = [t2 + t1[0:1]]
    flats = [row8, row8 + float(SUBLANES)]
    for a in range(1, 8):
        n_a = PEER_TOPK // (a + 1)
        cands.append(jnp.where(row8 < n_a, t2[0:8] + t1[a:a + 1], NEG_INF))
        flats.append(row8 + float(a * PEER_TOPK))
    cands.append(t1[8:16] + t2[0:1])
    flats.append((row8 + 8.0) * float(PEER_TOPK))
    return jnp.concatenate(cands, axis=0), flats


def _select_candidates(cand_ref, flats, n_prob):
    flat = jnp.concatenate(flats, axis=0)

    def body(a, carry):
        for p in range(n_prob):
            v = cand_ref[p]
            _, first = _max_first(v, flats)
            cand_ref[p] = jnp.where(flat == first, NEG_INF, v)
        return carry

    lax.fori_loop(0, PEER_TOPK, body, 0)


def _route_head(s1, s2, i1, t1, i2, t2, cand_left):
    n = s1.shape[1]
    row8 = lax.broadcasted_iota(jnp.int32, (SUBLANES, n), 0).astype(F32)
    cand, _ = _candidates(t1, t2)
    sel = jnp.where(cand_left != cand, 1.0, 0.0)
    cmax = cand[0:1]
    z = jnp.sum(sel * jnp.exp(cand - cmax), axis=0, keepdims=True)
    l_top = jnp.where(row8 == 0.0, jnp.sum(sel[0:16], axis=0, keepdims=True), 0.0)
    for a in range(1, 8):
        l_a = jnp.sum(sel[8 + 8 * a:16 + 8 * a], axis=0, keepdims=True)
        l_top = jnp.where(row8 == float(a), l_a, l_top)
    l_mat = jnp.concatenate([l_top, sel[72:80]], axis=0)
    l1 = _by_rank(i1, [l_mat[a:a + 1] for a in range(PEER_TOPK)], 0.0, s1.shape)
    r2 = _by_rank(i2, [float(a) for a in range(PEER_TOPK)], float(PEER_TOPK), s2.shape)
    c1 = jnp.exp(s1 - t1[0:1]) * (1.0 / z)
    e2 = jnp.exp(s2 - t2[0:1])
    return r2, e2, l1, c1


def _peer_body(x_ref, g_ref, sh_ref, sc_ref, g2_ref, fg_ref, wqh_ref, wql_ref, kh_ref, kl_ref, u_ref, vt_ref,
               o_ref, h2t_scr, s_scr, work_scr, top_scr, idx_scr, cand_scr, r2_scr, e2_scr, l1_scr, c1_scr,
               a_scr, wf_scr, acc_scr):
    j = pl.program_id(1)
    nj = pl.num_programs(1)
    tm = x_ref.shape[0]

    @pl.when(j == 0)
    def _():
        h2 = _rms_mod(x_ref[...], g_ref[...], sh_ref[0], sc_ref[0])
        h2t = h2.T
        h_hi, h_lo = _split_bf16(h2t)
        h2t_scr[...] = h_hi
        qt = _dot3(wqh_ref[...], wql_ref[...], h_hi, h_lo)
        q_hi, q_lo = _split_bf16(qt)
        for hp in range(2 * PEER_HEADS):
            sl = slice(hp * PEER_DHALF, (hp + 1) * PEER_DHALF)
            s_scr[hp] = _dot3(kh_ref[hp], kl_ref[hp], q_hi[sl], q_lo[sl])
        acc_scr[...] = jnp.zeros_like(acc_scr)

        def head_body(hg, carry):
            n_pb = tm // PEER_RB
            units = [(hg * PEER_HG + hh, pb) for hh in range(PEER_HG) for pb in range(n_pb)]
            for k, (h, pb) in enumerate(units):
                ps = slice(pb * PEER_RB, (pb + 1) * PEER_RB)
                work_scr[2 * k] = s_scr[2 * h, :, ps]
                work_scr[2 * k + 1] = s_scr[2 * h + 1, :, ps]
            tops = _topk_keys(work_scr, top_scr, idx_scr, 2 * len(units))
            for k in range(len(units)):
                cand_scr[k], flats = _candidates(tops[2 * k][1], tops[2 * k + 1][1])
            _select_candidates(cand_scr, flats, len(units))
            for k, (h, pb) in enumerate(units):
                ps = slice(pb * PEER_RB, (pb + 1) * PEER_RB)
                (i1, t1), (i2, t2) = tops[2 * k], tops[2 * k + 1]
                r2, e2, l1, c1 = _route_head(s_scr[2 * h, :, ps], s_scr[2 * h + 1, :, ps], i1, t1, i2, t2,
                                             cand_scr[k])
                for q in range(PEER_RB // LANES):
                    qs = slice(q * LANES, (q + 1) * LANES)
                    r2_scr[h, pb * (PEER_RB // LANES) + q] = r2[:, qs]
                    e2_scr[h, pb * (PEER_RB // LANES) + q] = e2[:, qs]
                l1_scr[h, :, ps] = l1
                c1_scr[h, :, ps] = c1
            return carry

        lax.fori_loop(0, PEER_HEADS // PEER_HG, head_body, 0)

    n_lb = tm // LANES

    res = _dot(u_ref[...], h2t_scr[...])
    for q in range(n_lb):
        a_scr[q] = res[:, q * LANES:(q + 1) * LANES]

    def slab_body(s, carry):
        i1 = j * PEER_SLABS + s
        r0 = pl.multiple_of(s * PEER_NKEYS, PEER_NKEYS)
        lrows = [l1_scr[h, pl.ds(i1, 1), :] for h in range(PEER_HEADS)]
        crows = [c1_scr[h, pl.ds(i1, 1), :] for h in range(PEER_HEADS)]
        for lb in range(n_lb):
            ls = slice(lb * LANES, (lb + 1) * LANES)
            w = None
            for h in range(PEER_HEADS):
                term = jnp.where(r2_scr[h, lb] < lrows[h][:, ls], e2_scr[h, lb], 0.0) * crows[h][:, ls]
                w = term if w is None else w + term
            act = _gelu_tanh(a_scr[lb, pl.ds(r0, PEER_NKEYS), :])
            wf_scr[lb, pl.ds(r0, PEER_NKEYS), :] = (w * act).astype(BF16)
        return carry

    lax.fori_loop(0, PEER_SLABS, slab_body, 0)
    rhs = jnp.concatenate([wf_scr[q] for q in range(n_lb)], axis=1)
    acc_scr[...] += _dot(vt_ref[0], rhs)

    @pl.when(j == nj - 1)
    def _():
        x2 = x_ref[...] + g2_ref[0] * acc_scr[...].T
        ms = jnp.mean(x2 * x2, axis=-1, keepdims=True)
        o_ref[...] = x2 * lax.rsqrt(ms + EPS) * fg_ref[...]


def _peer(x2, g, shift, scale, gate, final_g, wq_hi, wq_lo, k_hi, k_lo, u_bf, vt_bf, seq_len):
    t, d = x2.shape
    n_exp = u_bf.shape[0]
    tm = PEER_TM
    per_batch = seq_len // tm
    mod = lambda: pl.BlockSpec((1, 1, d), lambda i, j: (i // per_batch, 0, 0))
    vec = lambda: pl.BlockSpec((1, d), lambda i, j: (0, 0))
    nq = wq_hi.shape[0]
    tab = lambda: pltpu.VMEM((PEER_HEADS, PEER_NKEYS, tm), F32)
    nblk = n_exp // PEER_EBLK
    return pl.pallas_call(
        _peer_body,
        grid=(t // tm, nblk),
        in_specs=[pl.BlockSpec((tm, d), lambda i, j: (i, 0), pipeline_mode=pl.Buffered(1)),
                  vec(), mod(), mod(), mod(), vec(),
                  pl.BlockSpec((nq, d), lambda i, j: (0, 0)),
                  pl.BlockSpec((nq, d), lambda i, j: (0, 0)),
                  pl.BlockSpec((2 * PEER_HEADS, PEER_NKEYS, PEER_DHALF), lambda i, j: (0, 0, 0)),
                  pl.BlockSpec((2 * PEER_HEADS, PEER_NKEYS, PEER_DHALF), lambda i, j: (0, 0, 0)),
                  pl.BlockSpec((PEER_EBLK, d), lambda i, j: (j, 0)),
                  pl.BlockSpec((1, d, PEER_EBLK), lambda i, j: (j, 0, 0))],
        out_specs=pl.BlockSpec((tm, d), lambda i, j: (i, 0)),
        out_shape=jax.ShapeDtypeStruct((t, d), F32),
        scratch_shapes=[pltpu.VMEM((d, tm), BF16),
                        pltpu.VMEM((2 * PEER_HEADS, PEER_NKEYS, tm), F32),
                        pltpu.VMEM((2 * PEER_HG * (tm // PEER_RB), PEER_NKEYS, PEER_RB), F32),
                        pltpu.VMEM((2 * PEER_HG * (tm // PEER_RB), PEER_TOPK, PEER_RB), F32),
                        pltpu.VMEM((2 * PEER_HG * (tm // PEER_RB), PEER_TOPK, PEER_RB), F32),
                        pltpu.VMEM((PEER_HG * (tm // PEER_RB), N_CAND_ROWS, PEER_RB), F32),
                        pltpu.VMEM((PEER_HEADS, tm // LANES, PEER_NKEYS, LANES), F32),
                        pltpu.VMEM((PEER_HEADS, tm // LANES, PEER_NKEYS, LANES), F32),
                        tab(), tab(),
                        pltpu.VMEM((tm // LANES, PEER_EBLK, LANES), F32),
                        pltpu.VMEM((tm // LANES, PEER_EBLK, LANES), BF16),
                        pltpu.VMEM((d, tm), F32)],
        compiler_params=pltpu.CompilerParams(
            dimension_semantics=("parallel", "arbitrary"), vmem_limit_bytes=_vmem_limit(58 << 20)),
        name="peer",
    )(x2, g, shift, scale, gate, final_g, wq_hi, wq_lo, k_hi, k_lo, u_bf, vt_bf)


def kernel(x, c, ctx, c_ctx, mod_w, mod_b, norm1_g, norm2_g, w_in, ret_decay, conv_w, conv_b, lru_wa, lru_ba,
           lru_wx, lru_bx, lru_lambda, w_ret_out, w_lru_out, w_out, peer_wq, peer_keys, peer_u, peer_v, final_g):
    bsz, seq, d = x.shape
    ctx_len = ctx.shape[1]
    l = 0

    pad = (-(bsz + 1)) % SUBLANES
    c_all = jnp.concatenate([c, c_ctx[None, :], jnp.zeros((pad, d), F32)], axis=0)
    mod = _modulation(c_all, mod_w[l], mod_b[l][None, :])
    mod_x = mod[:bsz].reshape(bsz, 1, 6, d)
    sh1x, sc1x, g1x, sh2x, sc2x, g2x = [mod_x[:, :, i, :] for i in range(6)]
    mod_c = mod[bsz:bsz + 1].reshape(1, 1, 6, d)
    sh1c, sc1c = mod_c[:, :, 0, :], mod_c[:, :, 1, :]

    w_in_bf = w_in[l].astype(BF16)
    g1 = norm1_g[l][None, :]
    x2 = x.reshape(bsz * seq, d)
    ctx2 = ctx.reshape(bsz * ctx_len, d)
    px, gates_x = _in_projection(x2, g1, sh1x, sc1x, w_in_bf, seq, rope=_rope_tables(seq))
    pc, _ = _in_projection(ctx2, g1, sh1c, sc1c, w_in_bf, ctx_len)
    px3 = px.reshape(px.shape[0], bsz, seq, d)
    pc3 = pc.reshape(pc.shape[0], bsz, ctx_len, d)

    lg = jax.nn.log_sigmoid(ret_decay[l].astype(F32))
    zeros_s = jnp.zeros((bsz, RET_HEADS, RET_DK, RET_DV), F32)
    _, _, sf, sb = _retention(pc3, lg, zeros_s, zeros_s)
    o_f, o_b, _, _ = _retention(px3, lg, sf, sb)

    cw = conv_w[l]
    cb = conv_b[l][None, :]
    h_dirs = []
    for dr in range(2):
        wa_bf = lru_wa[l, dr].astype(BF16)
        wx_bf = lru_wx[l, dr].astype(BF16)
        ba = lru_ba[l, dr][None, :]
        bx = lru_bx[l, dr][None, :]
        lam = lru_lambda[l, dr][None, :]
        h0 = jnp.zeros((bsz, 1, cw.shape[1]), F32)
        _, fin = _lru_direction(pc3, cw, cb, wa_bf, ba, wx_bf, bx, lam, h0, reverse=bool(dr))
        hx, _ = _lru_direction(px3, cw, cb, wa_bf, ba, wx_bf, bx, lam, fin, reverse=bool(dr))
        h_dirs.append(hx.reshape(bsz * seq, -1))

    x1 = _merge(x2, o_f.reshape(bsz * seq, -1), o_b.reshape(bsz * seq, -1), h_dirs[0], h_dirs[1], gates_x, g1x,
                w_ret_out[l].astype(BF16), w_lru_out[l].astype(BF16), w_out[l].astype(BF16), seq)

    wq_t = peer_wq[l].T
    wq_hi = wq_t.astype(BF16)
    wq_lo = (wq_t - wq_hi.astype(F32)).astype(BF16)
    keys = peer_keys[l].reshape(2 * PEER_HEADS, PEER_NKEYS, PEER_DHALF)
    k_hi = keys.astype(BF16)
    k_lo = (keys - k_hi.astype(F32)).astype(BF16)
    u_bf = peer_u[l].astype(BF16)
    vt_bf = peer_v[l].astype(BF16).reshape(-1, PEER_EBLK, d).transpose(0, 2, 1)
    out = _peer(x1, norm2_g[l][None, :], sh2x, sc2x, g2x, final_g[None, :], wq_hi, wq_lo, k_hi, k_lo,
                u_bf, vt_bf, seq)
    return out.reshape(bsz, seq, d)
```
